```python
import math
import jax, jax.numpy as jnp
from jax import lax
import numpy as np

D_MODEL = 1024
BATCH = 2
SEQ = 8192
DEPTH = 1

HEAD_DIM = 64
DIL_GROUPS = ((128, 1), (512, 4), (2048, 16))
DIL_HEADS_PER_GROUP = 4
N_DIL_HEADS = DIL_HEADS_PER_GROUP * len(DIL_GROUPS)
N_SB_HEADS = 8
DIL_WIDTH = N_DIL_HEADS * HEAD_DIM
DIL_OUT_WIDTH = DIL_HEADS_PER_GROUP * HEAD_DIM
SB_WIDTH = N_SB_HEADS * HEAD_DIM
D_FF = 4 * D_MODEL
BLOCK = 128
RMS_EPS = 1e-6
NEG_INF = -1e30
IN_COLS = 3 * DIL_WIDTH + 3 * SB_WIDTH + 2 * D_MODEL
SPLITS = (DIL_WIDTH, 2 * DIL_WIDTH, 3 * DIL_WIDTH,
          3 * DIL_WIDTH + SB_WIDTH, 3 * DIL_WIDTH + 2 * SB_WIDTH, 3 * DIL_WIDTH + 3 * SB_WIDTH,
          3 * DIL_WIDTH + 3 * SB_WIDTH + D_MODEL)

kernel_name = "hybrid_dilated_stickbreaking_gated_block"


def rmsnorm(x, g):
    xf = x.astype(jnp.float32)
    y = xf * lax.rsqrt(jnp.mean(xf * xf, axis=-1, keepdims=True) + RMS_EPS)
    return (y * g.astype(jnp.float32)).astype(x.dtype)


def alibi_slopes(n):
    return jnp.exp2(-8.0 * jnp.arange(1, n + 1, dtype=jnp.float32) / n)


def dilated_window_group(q, k, v, slopes, window, dilation):
    b, s, h, dh = q.shape
    n_steps = window // dilation
    nb = -(-s // (dilation * BLOCK))
    sub_len = nb * BLOCK
    s_pad = sub_len * dilation

    def to_blocks(t):
        t = jnp.pad(t, ((0, 0), (0, s_pad - s), (0, 0), (0, 0)))
        t = t.reshape(b, sub_len, dilation, h, dh)
        t = t.transpose(0, 2, 3, 1, 4)
        return t.reshape(b, dilation, h, nb, BLOCK, dh)

    qb, kb, vb = to_blocks(q), to_blocks(k), to_blocks(v)

    def with_prev(t):
        prev = jnp.pad(t[:, :, :, :-1], ((0, 0), (0, 0), (0, 0), (1, 0), (0, 0), (0, 0)))
        return jnp.concatenate([prev, t], axis=4)

    kk, vv = with_prev(kb), with_prev(vb)
    scores = jnp.einsum('brhnqd,brhnkd->brhnqk', qb, kk).astype(jnp.float32) / math.sqrt(dh)
    qi = jnp.arange(BLOCK)[:, None]
    kj = jnp.arange(2 * BLOCK)[None, :]
    steps = qi + BLOCK - kj
    key_sub_idx = jnp.arange(nb)[:, None, None] * BLOCK + kj[None] - BLOCK
    valid = (steps >= 0) & (steps <= n_steps) & (key_sub_idx >= 0)
    bias = -slopes[:, None, None].astype(jnp.float32) * (steps * dilation).astype(jnp.float32)
    logits = scores + bias[None, None, :, None]
    logits = jnp.where(valid[None, None, None], logits, NEG_INF)
    lse = jax.nn.logsumexp(logits, axis=-1)
    p = jnp.exp(logits - lse[..., None])
    o = jnp.einsum('brhnqk,brhnkd->brhnqd', p.astype(v.dtype), vv)

    def from_blocks(t):
        extra = t.shape[5:]
        t = t.reshape((b, dilation, h, sub_len) + extra)
        t = jnp.moveaxis(t, 3, 1)
        t = t.reshape((b, s_pad, h) + extra)
        return t[:, :s]

    return from_blocks(o), from_blocks(lse)


def dilated_attention(q, k, v):
    b, s = q.shape[:2]
    slopes = alibi_slopes(N_DIL_HEADS)
    outs, lses = [], []
    for g, (window, dilation) in enumerate(DIL_GROUPS):
        sl = slice(g * DIL_HEADS_PER_GROUP, (g + 1) * DIL_HEADS_PER_GROUP)
        o, l = dilated_window_group(q[:, :, sl], k[:, :, sl], v[:, :, sl], slopes[sl], window, dilation)
        outs.append(o)
        lses.append(l)
    o_all = jnp.stack(outs, axis=0).astype(jnp.float32)
    w = jax.nn.softmax(jnp.stack(lses, axis=0), axis=0)
    out = jnp.sum(w[..., None] * o_all, axis=0).astype(q.dtype)
    return out.reshape(b, s, DIL_OUT_WIDTH)


def stick_breaking_attention(q, k, v):
    b, s, h, dh = q.shape
    nb = s // BLOCK
    scale = 1.0 / math.sqrt(dh)
    kt = k.transpose(0, 2, 1, 3)
    vt = v.transpose(0, 2, 1, 3)
    q_blocks = q.transpose(0, 2, 1, 3).reshape(b, h, nb, BLOCK, dh).transpose(2, 0, 1, 3, 4)
    key_pos = jnp.arange(s)

    def one_block(args):
        i, q_blk = args
        z = jnp.einsum('bhqd,bhkd->bhqk', q_blk, kt).astype(jnp.float32) * scale
        q_pos = i * BLOCK + jnp.arange(BLOCK)
        causal = key_pos[None, :] < q_pos[:, None]
        log_one_minus = jnp.where(causal, jax.nn.log_sigmoid(-z), 0.0)
        suffix = lax.cumsum(log_one_minus, axis=3, reverse=True) - log_one_minus
        log_a = jax.nn.log_sigmoid(z) + suffix
        a = jnp.where(causal, jnp.exp(log_a), 0.0)
        return jnp.einsum('bhqk,bhkd->bhqd', a.astype(vt.dtype), vt)

    o_blocks = lax.map(one_block, (jnp.arange(nb), q_blocks))
    return o_blocks.transpose(1, 0, 3, 2, 4).reshape(b, s, h * dh)


def setup_inputs(seed: int = 0) -> dict:
    key = jax.random.key(seed)
    ks = jax.random.split(key, 12)
    f32 = jnp.float32
    x = jax.random.normal(ks[0], (BATCH, SEQ, D_MODEL), f32)
    norm_mix_g = 1.0 + 0.02 * jax.random.normal(ks[1], (DEPTH, D_MODEL), f32)
    w_in = jax.random.normal(ks[2], (DEPTH, D_MODEL, IN_COLS), f32) * D_MODEL ** -0.5
    b_gate = 0.02 * jax.random.normal(ks[3], (DEPTH, 2 * D_MODEL), f32)
    w_up_dil = jax.random.normal(ks[4], (DEPTH, DIL_OUT_WIDTH, D_MODEL), f32) * DIL_OUT_WIDTH ** -0.5
    w_up_sb = jax.random.normal(ks[5], (DEPTH, SB_WIDTH, D_MODEL), f32) * SB_WIDTH ** -0.5
    w_out = jax.random.normal(ks[6], (DEPTH, D_MODEL, D_MODEL), f32) * D_MODEL ** -0.5
    norm_mlp_g = 1.0 + 0.02 * jax.random.normal(ks[7], (DEPTH, D_MODEL), f32)
    w_mlp_in = jax.random.normal(ks[8], (DEPTH, D_MODEL, D_FF), f32) * D_MODEL ** -0.5
    w_mlp_out = jax.random.normal(ks[9], (DEPTH, D_FF, D_MODEL), f32) * D_FF ** -0.5
    norm_final_g = 1.0 + 0.02 * jax.random.normal(ks[10], (D_MODEL,), f32)
    return {"x": x, "norm_mix_g": norm_mix_g, "w_in": w_in, "b_gate": b_gate,
            "w_up_dil": w_up_dil, "w_up_sb": w_up_sb, "w_out": w_out,
            "norm_mlp_g": norm_mlp_g, "w_mlp_in": w_mlp_in, "w_mlp_out": w_mlp_out,
            "norm_final_g": norm_final_g}


def reference(x, norm_mix_g, w_in, b_gate, w_up_dil, w_up_sb, w_out,
              norm_mlp_g, w_mlp_in, w_mlp_out, norm_final_g):
    b, s, _ = x.shape
    for layer in range(DEPTH):
        h = rmsnorm(x, norm_mix_g[layer])
        proj = h @ w_in[layer]
        q_a, k_a, v_a, q_b, k_b, v_b, gl_a, gl_b = jnp.split(proj, SPLITS, axis=-1)
        heads_a = lambda t: t.reshape(b, s, N_DIL_HEADS, HEAD_DIM)
        heads_b = lambda t: t.reshape(b, s, N_SB_HEADS, HEAD_DIM)
        o_a = dilated_attention(heads_a(q_a), heads_a(k_a), heads_a(v_a))
        o_b = stick_breaking_attention(heads_b(q_b), heads_b(k_b), heads_b(v_b))
        bg_a, bg_b = jnp.split(b_gate[layer], 2)
        g_a = jax.nn.sigmoid(gl_a + bg_a)
        g_b = jax.nn.sigmoid(gl_b + bg_b)
        merged = g_a * (o_a @ w_up_dil[layer]) + g_b * (o_b @ w_up_sb[layer])
        x = x + merged @ w_out[layer]
        h2 = rmsnorm(x, norm_mlp_g[layer])
        x = x + jnp.square(jax.nn.relu(h2 @ w_mlp_in[layer])) @ w_mlp_out[layer]
    return rmsnorm(x, norm_final_g)
```

```python
import functools
import math

import jax
import jax.numpy as jnp
from jax import lax
from jax.experimental import pallas as pl
from jax.experimental.pallas import tpu as pltpu

D_MODEL = 1024
HEAD_DIM = 64
DIL_GROUPS = ((128, 1), (512, 4), (2048, 16))
DIL_HEADS_PER_GROUP = 4
N_DIL_HEADS = DIL_HEADS_PER_GROUP * len(DIL_GROUPS)
N_SB_HEADS = 8
DIL_WIDTH = N_DIL_HEADS * HEAD_DIM
DIL_OUT_WIDTH = DIL_HEADS_PER_GROUP * HEAD_DIM
SB_WIDTH = N_SB_HEADS * HEAD_DIM
D_FF = 4 * D_MODEL
BLOCK = 128
RMS_EPS = 1e-6
NEG_INF = -1e30
LOG2E = 1.4426950408889634
LANES = 128

VMEM_LIMIT = 56 * 1024 * 1024

BF16 = jnp.bfloat16
F32 = jnp.float32


def _dot(a, b):
    return jnp.dot(a, b, preferred_element_type=F32)


def _dot_nt(a, b):
    return lax.dot_general(a, b, (((1,), (1,)), ((), ())), preferred_element_type=F32)


def _rms(x, g):
    return x * lax.rsqrt(jnp.mean(x * x, axis=-1, keepdims=True) + RMS_EPS) * g


def _sigmoid(x):
    return 1.0 / (1.0 + jnp.exp(-x))


PROJ_TM = 256


def _proj_kernel(x_ref, g_ref, wqd, wkd, wvd, wqs, wks_t, wvs, wga, wgb, bga, bgb,
                 qd_ref, kd_ref, vd_ref, qs_ref, kst_ref, vs_ref, ga_ref, gb_ref):
    h = _rms(x_ref[...], g_ref[...]).astype(BF16)
    scale = 1.0 / math.sqrt(HEAD_DIM)
    qd_ref[...] = (_dot(h, wqd[...]) * scale).astype(BF16)
    kd_ref[...] = _dot(h, wkd[...]).astype(BF16)
    vd_ref[...] = _dot(h, wvd[...]).astype(BF16)
    qs_ref[...] = (_dot(h, wqs[...]) * (scale * LOG2E)).astype(BF16)
    kst_ref[...] = _dot_nt(wks_t[...], h).astype(BF16)
    vs_ref[...] = _dot(h, wvs[...]).astype(BF16)
    ga_ref[...] = _sigmoid(_dot(h, wga[...]) + bga[...])
    gb_ref[...] = _sigmoid(_dot(h, wgb[...]) + bgb[...])


def _proj(x, g, wqd, wkd, wvd, wqs, wks_t, wvs, wga, wgb, bga, bgb):
    b, s, d = x.shape
    tm = PROJ_TM
    tok = lambda w: pl.BlockSpec((None, tm, w), lambda bi, i: (bi, i, 0))
    full = lambda a: pl.BlockSpec(a.shape, lambda bi, i: (0,) * a.ndim)
    out_shape = (
        jax.ShapeDtypeStruct((b, s, DIL_WIDTH), BF16),
        jax.ShapeDtypeStruct((b, s, DIL_WIDTH), BF16),
        jax.ShapeDtypeStruct((b, s, DIL_WIDTH), BF16),
        jax.ShapeDtypeStruct((b, s, SB_WIDTH), BF16),
        jax.ShapeDtypeStruct((b, SB_WIDTH, s), BF16),
        jax.ShapeDtypeStruct((b, s, SB_WIDTH), BF16),
        jax.ShapeDtypeStruct((b, s, D_MODEL), F32),
        jax.ShapeDtypeStruct((b, s, D_MODEL), F32),
    )
    out_specs = (
        tok(DIL_WIDTH), tok(DIL_WIDTH), tok(DIL_WIDTH), tok(SB_WIDTH),
        pl.BlockSpec((None, SB_WIDTH, tm), lambda bi, i: (bi, 0, i)),
        tok(SB_WIDTH), tok(D_MODEL), tok(D_MODEL),
    )
    weights = (g, wqd, wkd, wvd, wqs, wks_t, wvs, wga, wgb, bga, bgb)
    return pl.pallas_call(
        _proj_kernel,
        grid=(b, s // tm),
        in_specs=[tok(d)] + [full(a) for a in weights],
        out_specs=out_specs,
        out_shape=out_shape,
        compiler_params=pltpu.CompilerParams(
            dimension_semantics=("parallel", "parallel"), vmem_limit_bytes=VMEM_LIMIT),
        name="proj",
    )(x, *weights)


def _alibi_slope(head):
    return 2.0 ** (-8.0 * (head + 1) / N_DIL_HEADS)


def _dil_kernel(q_ref, kp_ref, kc_ref, vp_ref, vc_ref, o_ref, lse_ref, *, group, dilation, n_steps):
    n = pl.program_id(2)
    qi = lax.broadcasted_iota(jnp.int32, (BLOCK, 2 * BLOCK), 0)
    kj = lax.broadcasted_iota(jnp.int32, (BLOCK, 2 * BLOCK), 1)
    steps = qi + BLOCK - kj
    valid = (steps >= 0) & (steps <= n_steps) & ((kj >= BLOCK) | (n > 0))
    dist = (steps * dilation).astype(F32)
    q = q_ref[...]
    k = jnp.concatenate([kp_ref[...], kc_ref[...]], axis=0)
    v = jnp.concatenate([vp_ref[...], vc_ref[...]], axis=0)
    outs, lses = [], []
    for j in range(DIL_HEADS_PER_GROUP):
        sl = slice(j * HEAD_DIM, (j + 1) * HEAD_DIM)
        slope = _alibi_slope(group * DIL_HEADS_PER_GROUP + j)
        logits = _dot_nt(q[:, sl], k[:, sl]) - slope * dist
        logits = jnp.where(valid, logits, NEG_INF)
        m = jnp.max(logits, axis=-1, keepdims=True)
        p = jnp.exp(logits - m)
        ssum = jnp.sum(p, axis=-1, keepdims=True)
        o = _dot(p.astype(BF16), v[:, sl]) / ssum
        outs.append(o)
        lses.append(jnp.broadcast_to(m + jnp.log(ssum), (BLOCK, HEAD_DIM)))
    o_ref[...] = jnp.concatenate(outs, axis=1)
    lse_ref[...] = jnp.concatenate(lses, axis=1)


def _dilated_group(qd, kd, vd, group):
    window, dilation = DIL_GROUPS[group]
    b, s, _ = qd.shape
    sub_len = s // dilation
    nb = sub_len // BLOCK
    assert nb * BLOCK * dilation == s
    groups_per_row = DIL_WIDTH // DIL_OUT_WIDTH
    view = lambda t: t.reshape(b, sub_len, dilation * DIL_WIDTH)
    col = lambda r: r * groups_per_row + group
    cur = pl.BlockSpec((None, BLOCK, DIL_OUT_WIDTH), lambda bi, r, n: (bi, n, col(r)))
    prev = pl.BlockSpec((None, BLOCK, DIL_OUT_WIDTH),
                        lambda bi, r, n: (bi, jnp.maximum(n - 1, 0), col(r)))
    out = pl.BlockSpec((None, BLOCK, DIL_OUT_WIDTH), lambda bi, r, n: (bi, n, r))
    shp = jax.ShapeDtypeStruct((b, sub_len, dilation * DIL_OUT_WIDTH), F32)
    o, lse = pl.pallas_call(
        functools.partial(_dil_kernel, group=group, dilation=dilation,
                          n_steps=window // dilation),
        grid=(b, dilation, nb),
        in_specs=[cur, prev, cur, prev, cur],
        out_specs=(out, out),
        out_shape=(shp, shp),
        compiler_params=pltpu.CompilerParams(
            dimension_semantics=("parallel", "parallel", "parallel"),
            vmem_limit_bytes=VMEM_LIMIT),
        name=f"dilated_d{dilation}",
    )(view(qd), view(kd), view(kd), view(vd), view(vd))
    return o.reshape(b, s, DIL_OUT_WIDTH), lse.reshape(b, s, DIL_OUT_WIDTH)


SB_TQ = 256
SB_TK = 256
HEADS_PER_STEP = LANES // HEAD_DIM


def _sb_kernel(q_ref, kt_ref, v_ref, tri_ref, o_ref, acc_ref, carry_ref):
    qi = pl.program_id(2)
    tri = tri_ref[...]
    n_chunks = SB_TK // LANES

    def tile(h, q_h, ks, diagonal):
        hs = slice(h * HEAD_DIM, (h + 1) * HEAD_DIM)
        z = _dot(q_h, kt_ref[hs, pl.ds(ks, SB_TK)])
        sp = jnp.maximum(z, 0.0) + jnp.log2(1.0 + jnp.exp2(-jnp.abs(z)))
        if diagonal:
            row = lax.broadcasted_iota(jnp.int32, (SB_TQ, SB_TK), 0)
            colm = lax.broadcasted_iota(jnp.int32, (SB_TQ, SB_TK), 1)
            causal = colm < row
            sp = jnp.where(causal, sp, 0.0)
        las = [None] * n_chunks
        for c in reversed(range(n_chunks)):
            cs = slice(c * LANES, (c + 1) * LANES)
            sp_c = sp[:, cs]
            hi = sp_c.astype(BF16)
            lo = (sp_c - hi.astype(F32)).astype(BF16)
            r = _dot(jnp.concatenate([hi, lo], axis=1), tri)
            carry = carry_ref[...]
            las[c] = z[:, cs] - r[:, :LANES] - carry
            carry_ref[...] = carry + r[:, LANES:]
        la = jnp.concatenate(las, axis=1)
        if diagonal:
            la = jnp.where(causal, la, NEG_INF)
        a = jnp.exp2(la).astype(BF16)
        acc_ref[...] += _dot(a, v_ref[pl.ds(ks, SB_TK), :])

    outs = []
    for h in range(HEADS_PER_STEP):
        q_h = q_ref[:, h * HEAD_DIM:(h + 1) * HEAD_DIM]
        acc_ref[...] = jnp.zeros_like(acc_ref)
        carry_ref[...] = jnp.zeros_like(carry_ref)
        tile(h, q_h, pl.multiple_of(qi * SB_TQ, SB_TQ), True)

        def body(i, _):
            ks = pl.multiple_of((qi - 1 - i) * SB_TK, SB_TK)
            tile(h, q_h, ks, False)
            return 0

        lax.fori_loop(0, qi, body, 0)
        outs.append(acc_ref[:, h * HEAD_DIM:(h + 1) * HEAD_DIM])
    o_ref[...] = jnp.concatenate(outs, axis=1).astype(o_ref.dtype)


def _sb_attention(qs, kst, vs):
    b, s, _ = qs.shape
    assert SB_TQ == SB_TK and s % SB_TQ == 0
    t = (jnp.arange(LANES)[:, None] >= jnp.arange(LANES)[None, :]).astype(BF16)
    half = jnp.concatenate([t, jnp.ones((LANES, LANES), BF16)], axis=1)
    tri = jnp.concatenate([half, half], axis=0)
    return pl.pallas_call(
        _sb_kernel,
        grid=(b, SB_WIDTH // LANES, s // SB_TQ),
        in_specs=[
            pl.BlockSpec((None, SB_TQ, LANES), lambda bi, p, i: (bi, i, p)),
            pl.BlockSpec((None, LANES, s), lambda bi, p, i: (bi, p, 0)),
            pl.BlockSpec((None, s, LANES), lambda bi, p, i: (bi, 0, p)),
            pl.BlockSpec(tri.shape, lambda bi, p, i: (0, 0)),
        ],
        out_specs=pl.BlockSpec((None, SB_TQ, LANES), lambda bi, p, i: (bi, i, p)),
        out_shape=jax.ShapeDtypeStruct((b, s, SB_WIDTH), BF16),
        scratch_shapes=[pltpu.VMEM((SB_TQ, LANES), F32), pltpu.VMEM((SB_TQ, LANES), F32)],
        compiler_params=pltpu.CompilerParams(
            dimension_semantics=("parallel", "parallel", "arbitrary"),
            vmem_limit_bytes=VMEM_LIMIT),
        name="stick_breaking",
    )(qs, kst, vs, tri)


MERGE_TM = 256


def _merge_kernel(x_ref, o0, o1, o2, l0, l1, l2, ob_ref, ga_ref, gb_ref,
                  wud, wus, wout, g_ref, x1_ref, h2_ref):
    lse = (l0[...], l1[...], l2[...])
    m = jnp.maximum(jnp.maximum(lse[0], lse[1]), lse[2])
    e = [jnp.exp(l - m) for l in lse]
    o_a = (e[0] * o0[...] + e[1] * o1[...] + e[2] * o2[...]) / (e[0] + e[1] + e[2])
    up_a = _dot(o_a.astype(BF16), wud[...])
    up_b = _dot(ob_ref[...], wus[...])
    merged = ga_ref[...] * up_a + gb_ref[...] * up_b
    x1 = x_ref[...] + _dot(merged.astype(BF16), wout[...])
    x1_ref[...] = x1
    h2_ref[...] = _rms(x1, g_ref[...]).astype(BF16)


def _merge(x, oas, lses, o_b, g_a, g_b, wud, wus, wout, g):
    b, s, d = x.shape
    tm = MERGE_TM
    tok = lambda w: pl.BlockSpec((None, tm, w), lambda bi, i: (bi, i, 0))
    full = lambda a: pl.BlockSpec(a.shape, lambda bi, i: (0,) * a.ndim)
    weights = (wud, wus, wout, g)
    return pl.pallas_call(
        _merge_kernel,
        grid=(b, s // tm),
        in_specs=[tok(d)] + [tok(DIL_OUT_WIDTH)] * 6 + [tok(SB_WIDTH), tok(d), tok(d)]
                 + [full(a) for a in weights],
        out_specs=(tok(d), tok(d)),
        out_shape=(jax.ShapeDtypeStruct((b, s, d), F32), jax.ShapeDtypeStruct((b, s, d), BF16)),
        compiler_params=pltpu.CompilerParams(
            dimension_semantics=("parallel", "parallel"), vmem_limit_bytes=VMEM_LIMIT),
        name="merge",
    )(x, *oas, *lses, o_b, g_a, g_b, *weights)


MLP_TM = 256
MLP_FF_CHUNK = 1024


def _mlp_kernel(x1_ref, h2_ref, w1, w2, g_ref, out_ref, *, final_norm):
    h2 = h2_ref[...]
    y = x1_ref[...]
    for c in range(D_FF // MLP_FF_CHUNK):
        cs = slice(c * MLP_FF_CHUNK, (c + 1) * MLP_FF_CHUNK)
        u = jnp.maximum(_dot(h2, w1[:, cs]), 0.0)
        y = y + _dot((u * u).astype(BF16), w2[cs, :])
    out_ref[...] = _rms(y, g_ref[...]) if final_norm else y


def _mlp(x1, h2, w1, w2, g, final_norm):
    b, s, d = x1.shape
    tm = MLP_TM
    tok = lambda w: pl.BlockSpec((None, tm, w), lambda bi, i: (bi, i, 0))
    full = lambda a: pl.BlockSpec(a.shape, lambda bi, i: (0,) * a.ndim)
    return pl.pallas_call(
        functools.partial(_mlp_kernel, final_norm=final_norm),
        grid=(b, s // tm),
        in_specs=[tok(d), tok(d), full(w1), full(w2), full(g)],
        out_specs=tok(d),
        out_shape=jax.ShapeDtypeStruct((b, s, d), F32),
        compiler_params=pltpu.CompilerParams(
            dimension_semantics=("parallel", "parallel"), vmem_limit_bytes=VMEM_LIMIT),
        name="mlp",
    )(x1, h2, w1, w2, g)


def kernel(x, norm_mix_g, w_in, b_gate, w_up_dil, w_up_sb, w_out, norm_mlp_g, w_mlp_in,
           w_mlp_out, norm_final_g):
    depth = w_in.shape[0]
    row = lambda v: v.reshape(1, -1)
    for layer in range(depth):
        w = w_in[layer].astype(BF16)
        c0 = 3 * DIL_WIDTH
        c1 = c0 + 3 * SB_WIDTH
        wqd, wkd, wvd = (w[:, i * DIL_WIDTH:(i + 1) * DIL_WIDTH] for i in range(3))
        wqs, wks, wvs = (w[:, c0 + i * SB_WIDTH:c0 + (i + 1) * SB_WIDTH] for i in range(3))
        wga, wgb = w[:, c1:c1 + D_MODEL], w[:, c1 + D_MODEL:]
        bga, bgb = row(b_gate[layer][:D_MODEL]), row(b_gate[layer][D_MODEL:])
        qd, kd, vd, qs, kst, vs, g_a, g_b = _proj(
            x, row(norm_mix_g[layer]), wqd, wkd, wvd, wqs, wks.T, wvs, wga, wgb, bga, bgb)
        dil = [_dilated_group(qd, kd, vd, grp) for grp in range(len(DIL_GROUPS))]
        o_b = _sb_attention(qs, kst, vs)
        x1, h2 = _merge(x, [o for o, _ in dil], [l for _, l in dil], o_b, g_a, g_b,
                        w_up_dil[layer].astype(BF16), w_up_sb[layer].astype(BF16),
                        w_out[layer].astype(BF16), row(norm_mlp_g[layer]))
        x = _mlp(x1, h2, w_mlp_in[layer].astype(BF16), w_mlp_out[layer].astype(BF16),
                 row(norm_final_g), final_norm=layer == depth - 1)
    return x
```

```python
import functools
import math

import jax
import jax.numpy as jnp
from jax import lax
from jax.experimental import pallas as pl
from jax.experimental.pallas import tpu as pltpu

D_MODEL = 1024
HEAD_DIM = 64
DIL_GROUPS = ((128, 1), (512, 4), (2048, 16))
DIL_HEADS_PER_GROUP = 4
N_DIL_HEADS = DIL_HEADS_PER_GROUP * len(DIL_GROUPS)
N_SB_HEADS = 8
DIL_WIDTH = N_DIL_HEADS * HEAD_DIM
DIL_OUT_WIDTH = DIL_HEADS_PER_GROUP * HEAD_DIM
SB_WIDTH = N_SB_HEADS * HEAD_DIM
D_FF = 4 * D_MODEL
BLOCK = 128
RMS_EPS = 1e-6
NEG_INF = -1e30
LOG2E = 1.4426950408889634
LANES = 128

VMEM_LIMIT = 56 * 1024 * 1024

BF16 = jnp.bfloat16
F32 = jnp.float32


def _dot(a, b):
    return jnp.dot(a, b, preferred_element_type=F32)


def _dot_nt(a, b):
    return lax.dot_general(a, b, (((1,), (1,)), ((), ())), preferred_element_type=F32)


def _rms(x, g):
    return x * lax.rsqrt(jnp.mean(x * x, axis=-1, keepdims=True) + RMS_EPS) * g


def _sigmoid(x):
    return 1.0 / (1.0 + jnp.exp(-x))


DIL_TILE = 256
PROJ_TM = DIL_TILE


def _residue_major(h_ref, d):
    rows = DIL_TILE // d
    return jnp.concatenate(
        [jnp.concatenate([h_ref[c, pl.ds(r, rows, stride=d), :] for r in range(d)], axis=0)
         for c in range(h_ref.shape[0])], axis=1)


def _proj_kernel(x_ref, g_ref, wd0, wd1, wd2, wqs, wks_t, wvs, wga, wgb, bga, bgb,
                 d0_ref, d1_ref, d2_ref, qs_ref, kst_ref, vs_ref, ga_ref, gb_ref, h_scr):
    h32 = _rms(x_ref[...], g_ref[...])
    for c in range(h_scr.shape[0]):
        h_scr[c] = h32[:, c * LANES:(c + 1) * LANES]
    h = h32.astype(BF16)
    for (_, dilation), w_ref, o_ref in zip(DIL_GROUPS, (wd0, wd1, wd2), (d0_ref, d1_ref, d2_ref)):
        hd = h if dilation == 1 else _residue_major(h_scr, dilation).astype(BF16)
        o_ref[...] = _dot(hd, w_ref[...]).astype(BF16)
    qs_ref[...] = (_dot(h, wqs[...]) * (LOG2E / math.sqrt(HEAD_DIM))).astype(BF16)
    kst_ref[...] = _dot_nt(wks_t[...], h).astype(BF16)
    vs_ref[...] = _dot(h, wvs[...]).astype(BF16)
    ga_ref[...] = _sigmoid(_dot(h, wga[...]) + bga[...])
    gb_ref[...] = _sigmoid(_dot(h, wgb[...]) + bgb[...])


def _proj(x, g, wd, wqs, wks_t, wvs, wga, wgb, bga, bgb):
    b, s, d = x.shape
    tm = PROJ_TM
    tok = lambda w: pl.BlockSpec((None, tm, w), lambda bi, i: (bi, i, 0))
    full = lambda a: pl.BlockSpec(a.shape, lambda bi, i: (0,) * a.ndim)
    dil_width = 3 * DIL_OUT_WIDTH
    out_shape = (
        (jax.ShapeDtypeStruct((b, s, dil_width), BF16),) * len(DIL_GROUPS)
        + (jax.ShapeDtypeStruct((b, s, SB_WIDTH), BF16),
           jax.ShapeDtypeStruct((b, SB_WIDTH, s), BF16),
           jax.ShapeDtypeStruct((b, s, SB_WIDTH), BF16),
           jax.ShapeDtypeStruct((b, s, D_MODEL), F32),
           jax.ShapeDtypeStruct((b, s, D_MODEL), F32)))
    out_specs = (
        (tok(dil_width),) * len(DIL_GROUPS)
        + (tok(SB_WIDTH), pl.BlockSpec((None, SB_WIDTH, tm), lambda bi, i: (bi, 0, i)),
           tok(SB_WIDTH), tok(D_MODEL), tok(D_MODEL)))
    weights = (g, *wd, wqs, wks_t, wvs, wga, wgb, bga, bgb)
    return pl.pallas_call(
        _proj_kernel,
        grid=(b, s // tm),
        in_specs=[tok(d)] + [full(a) for a in weights],
        out_specs=out_specs,
        out_shape=out_shape,
        scratch_shapes=[pltpu.VMEM((d // LANES, tm, LANES), F32)],
        compiler_params=pltpu.CompilerParams(
            dimension_semantics=("parallel", "parallel"), vmem_limit_bytes=VMEM_LIMIT),
        name="proj",
    )(x, *weights)


DIL_QB = 4
DIL_QROWS = DIL_QB * BLOCK


def _alibi_slope(head):
    return 2.0 ** (-8.0 * (head + 1) / N_DIL_HEADS)


def _dil_kernel(q_ref, kp_ref, kc_ref, vp_ref, vc_ref, o_ref, lse_ref, *, group, dilation, n_steps):
    n = pl.program_id(2)
    width = DIL_OUT_WIDTH
    q = q_ref[...].reshape(DIL_QROWS, width)
    k = jnp.concatenate([kp_ref[...].reshape(BLOCK, width), kc_ref[...].reshape(DIL_QROWS, width)], axis=0)
    v = jnp.concatenate([vp_ref[...].reshape(BLOCK, width), vc_ref[...].reshape(DIL_QROWS, width)], axis=0)
    qi = lax.broadcasted_iota(jnp.int32, (BLOCK, 2 * BLOCK), 0)
    kj = lax.broadcasted_iota(jnp.int32, (BLOCK, 2 * BLOCK), 1)
    steps = qi + BLOCK - kj
    valid = (steps >= 0) & (steps <= n_steps)
    has_prev = (kj >= BLOCK) | (n > 0)
    dist = (steps * dilation).astype(F32)
    low = lax.broadcasted_iota(jnp.int32, (1, LANES), 1) < HEAD_DIM
    o_blocks = [[None] * 2 for _ in range(DIL_QB)]
    lse_blocks = [[None] * 2 for _ in range(DIL_QB)]
    for p in range(2):
        ps = slice(p * LANES, (p + 1) * LANES)
        qp, kp, vp = q[:, ps], k[:, ps], v[:, ps]
        sel = (low, jnp.logical_not(low))
        qm = [jnp.where(sel[e], qp, jnp.zeros_like(qp)) for e in range(2)]
        vm = [jnp.where(sel[e], vp, jnp.zeros_like(vp)) for e in range(2)]
        ones = [jnp.broadcast_to(jnp.where(sel[e], 1.0, 0.0).astype(BF16), (2 * BLOCK, LANES))
                for e in range(2)]
        bias = [jnp.where(valid, -_alibi_slope(group * DIL_HEADS_PER_GROUP + 2 * p + e) * dist,
                          NEG_INF) for e in range(2)]
        for j in range(DIL_QB):
            qs = slice(j * BLOCK, (j + 1) * BLOCK)
            ks = slice(j * BLOCK, (j + 2) * BLOCK)
            acc, ms = None, []
            for e in range(2):
                logits = _dot_nt(qm[e][qs], kp[ks]) + bias[e]
                if j == 0:
                    logits = jnp.where(has_prev, logits, NEG_INF)
                m = jnp.max(logits, axis=-1, keepdims=True)
                pr = jnp.exp(logits - m).astype(BF16)
                t = _dot(pr, jnp.concatenate([vm[e][ks], ones[e]], axis=1))
                acc = t if acc is None else acc + t
                ms.append(m)
            num, den = acc[:, :LANES], acc[:, LANES:]
            o_blocks[j][p] = num / den
            lse_blocks[j][p] = jnp.where(low, ms[0], ms[1]) + jnp.log(den)
    for j in range(DIL_QB):
        rows = o_ref.shape[-2] if len(o_ref.shape) == 3 else BLOCK
        tiles = slice(j * BLOCK // rows, (j + 1) * BLOCK // rows)
        for ref, blocks in ((o_ref, o_blocks), (lse_ref, lse_blocks)):
            val = jnp.concatenate(blocks[j], axis=1)
            if len(ref.shape) == 3:
                ref[tiles] = val.reshape(BLOCK // rows, rows, width)
            else:
                ref[j * BLOCK:(j + 1) * BLOCK, :] = val


def _dilated_group(qkv, group):
    window, dilation = DIL_GROUPS[group]
    b, s, _ = qkv.shape
    width = DIL_OUT_WIDTH
    n_grid = s // dilation // DIL_QROWS
    first_prev = lambda n: jnp.maximum(n * DIL_QB - 1, 0)
    if dilation == 1:
        view = qkv
        cur = lambda c: pl.BlockSpec((None, DIL_QROWS, width), lambda bi, r, n: (bi, n, c))
        prev = lambda c: pl.BlockSpec((None, BLOCK, width), lambda bi, r, n: (bi, first_prev(n), c))
        shp = jax.ShapeDtypeStruct((b, s, width), F32)
    else:
        rows = DIL_TILE // dilation
        view = qkv.reshape(b, s // DIL_TILE, dilation, rows, 3 * width)
        cur = lambda c: pl.BlockSpec((None, DIL_QROWS // rows, None, rows, width),
                                     lambda bi, r, n: (bi, n, r, 0, c))
        prev = lambda c: pl.BlockSpec((None, BLOCK // rows, None, rows, width),
                                      lambda bi, r, n: (bi, first_prev(n), r, 0, c))
        shp = jax.ShapeDtypeStruct((b, s // DIL_TILE, dilation, rows, width), F32)
    return pl.pallas_call(
        functools.partial(_dil_kernel, group=group, dilation=dilation,
                          n_steps=window // dilation),
        grid=(b, dilation, n_grid),
        in_specs=[cur(0), prev(1), cur(1), prev(2), cur(2)],
        out_specs=(cur(0), cur(0)),
        out_shape=(shp, shp),
        compiler_params=pltpu.CompilerParams(
            dimension_semantics=("parallel", "parallel", "parallel"),
            vmem_limit_bytes=VMEM_LIMIT),
        name=f"dilated_d{dilation}",
    )(view, view, view, view, view)


SB_TQ = 256
SB_TK = 256
HEADS_PER_STEP = LANES // HEAD_DIM


def _softplus2(z):
    return jnp.maximum(z, 0.0) + jnp.log2(1.0 + jnp.exp2(-jnp.abs(z)))


def _split_hi_lo(sp):
    parts = []
    for c in range(SB_TK // LANES):
        sp_c = sp[:, c * LANES:(c + 1) * LANES]
        hi = sp_c.astype(BF16)
        parts += [hi, (sp_c - hi.astype(F32)).astype(BF16)]
    return jnp.concatenate(parts, axis=1)


def _sb_weights(z, hl, tri, carry):
    n_chunks = SB_TK // LANES
    las = [None] * n_chunks
    for c in reversed(range(n_chunks)):
        r = _dot(hl[:, 2 * c * LANES:2 * (c + 1) * LANES], tri)
        las[c] = z[:, c * LANES:(c + 1) * LANES] - r[:, :LANES] - carry
        carry = carry + r[:, LANES:]
    return jnp.exp2(jnp.concatenate(las, axis=1)).astype(BF16), carry


SB_DEAD = 160.0


def _sb_kernel(q_ref, kt_ref, v_ref, tri_ref, o_ref, acc_ref, carry_ref):
    s = q_ref.shape[0]
    n_q = s // SB_TQ
    heads = range(HEADS_PER_STEP)
    hsl = lambda h: slice(h * HEAD_DIM, (h + 1) * HEAD_DIM)
    tri = tri_ref[...]

    k2 = jnp.square(kt_ref[...].astype(F32))
    k_max = [jnp.sqrt(jnp.max(jnp.sum(k2[hsl(h)], axis=0, keepdims=True))) for h in heads]

    row = lax.broadcasted_iota(jnp.int32, (SB_TQ, SB_TK), 0)
    col = lax.broadcasted_iota(jnp.int32, (SB_TQ, SB_TK), 1)
    causal = col < row

    def tile(h, q_h, kb, diagonal):
        ks = pl.multiple_of(kb * SB_TK, SB_TK)
        z = _dot(q_h, kt_ref[hsl(h), pl.ds(ks, SB_TK)])
        if diagonal:
            z = jnp.where(causal, z, NEG_INF)
        a, carry = _sb_weights(z, _split_hi_lo(_softplus2(z)), tri, carry_ref[h])
        carry_ref[h] = carry
        acc_ref[h] += _dot(a, v_ref[pl.ds(ks, SB_TK), :])
        return carry

    def query_block(qi, _):
        rows = pl.ds(pl.multiple_of(qi * SB_TQ, SB_TQ), SB_TQ)
        acc_ref[...] = jnp.zeros_like(acc_ref)
        carry_ref[...] = jnp.zeros_like(carry_ref)
        q = [q_ref[rows, hsl(h)] for h in heads]
        z_max = [jnp.sqrt(jnp.sum(jnp.square(q[h].astype(F32)), axis=1, keepdims=True))
                 * (k_max[h] * 1.001) + 1.0 for h in heads]

        def alive(carries):
            slack = [jnp.min(carries[h] - z_max[h]) for h in heads]
            return jnp.minimum(slack[0], slack[1]) < SB_DEAD

        carries = [tile(h, q[h], qi, True) for h in heads]

        def cond(state):
            kb, live = state
            return (kb >= 0) & live

        def body(state):
            kb, _ = state
            carries = [tile(h, q[h], kb, False) for h in heads]
            return kb - 1, alive(carries)

        lax.while_loop(cond, body, (qi - 1, alive(carries)))
        o_ref[rows, :] = jnp.concatenate([acc_ref[h, :, hsl(h)] for h in heads],
                                         axis=1).astype(o_ref.dtype)
        return 0

    lax.fori_loop(0, n_q, query_block, 0)


def _sb_attention(qs, kst, vs):
    b, s, _ = qs.shape
    assert SB_TQ == SB_TK and s % SB_TQ == 0 and HEADS_PER_STEP == 2
    t = (jnp.arange(LANES)[:, None] >= jnp.arange(LANES)[None, :]).astype(BF16)
    half = jnp.concatenate([t, jnp.ones((LANES, LANES), BF16)], axis=1)
    tri = jnp.concatenate([half, half], axis=0)
    nh = HEADS_PER_STEP
    return pl.pallas_call(
        _sb_kernel,
        grid=(b, SB_WIDTH // LANES),
        in_specs=[
            pl.BlockSpec((None, s, LANES), lambda bi, p: (bi, 0, p)),
            pl.BlockSpec((None, LANES, s), lambda bi, p: (bi, p, 0)),
            pl.BlockSpec((None, s, LANES), lambda bi, p: (bi, 0, p)),
            pl.BlockSpec(tri.shape, lambda bi, p: (0, 0)),
        ],
        out_specs=pl.BlockSpec((None, s, LANES), lambda bi, p: (bi, 0, p)),
        out_shape=jax.ShapeDtypeStruct((b, s, SB_WIDTH), BF16),
        scratch_shapes=[pltpu.VMEM((nh, SB_TQ, LANES), F32)] * 2,
        compiler_params=pltpu.CompilerParams(
            dimension_semantics=("parallel", "parallel"), vmem_limit_bytes=VMEM_LIMIT),
        name="stick_breaking",
    )(qs, kst, vs, tri)


MERGE_TM = DIL_TILE


def _token_major(ref, scr):
    d, rows, _ = ref.shape
    for c in range(scr.shape[0]):
        for r in range(d):
            scr[c, pl.ds(r, rows, stride=d), :] = ref[r, :, c * LANES:(c + 1) * LANES]
    return jnp.concatenate([scr[c] for c in range(scr.shape[0])], axis=1)


def _merge_kernel(x_ref, o0, o1, o2, l0, l1, l2, ob_ref, ga_ref, gb_ref,
                  wud, wus, wout, g_ref, x1_ref, h2_ref, *scratch):
    outs = [o0[...], _token_major(o1, scratch[0]), _token_major(o2, scratch[1])]
    lse = [l0[...], _token_major(l1, scratch[2]), _token_major(l2, scratch[3])]
    m = jnp.maximum(jnp.maximum(lse[0], lse[1]), lse[2])
    e = [jnp.exp(l - m) for l in lse]
    o_a = (e[0] * outs[0] + e[1] * outs[1] + e[2] * outs[2]) / (e[0] + e[1] + e[2])
    up_a = _dot(o_a.astype(BF16), wud[...])
    up_b = _dot(ob_ref[...], wus[...])
    merged = ga_ref[...] * up_a + gb_ref[...] * up_b
    x1 = x_ref[...] + _dot(merged.astype(BF16), wout[...])
    x1_ref[...] = x1
    h2_ref[...] = _rms(x1, g_ref[...]).astype(BF16)


def _merge(x, oas, lses, o_b, g_a, g_b, wud, wus, wout, g):
    b, s, d = x.shape
    tm = MERGE_TM
    tok = lambda w: pl.BlockSpec((None, tm, w), lambda bi, i: (bi, i, 0))
    full = lambda a: pl.BlockSpec(a.shape, lambda bi, i: (0,) * a.ndim)

    def dil(a):
        if a.ndim == 3:
            return tok(DIL_OUT_WIDTH)
        return pl.BlockSpec((None, None) + a.shape[2:], lambda bi, i: (bi, i, 0, 0, 0))

    weights = (wud, wus, wout, g)
    return pl.pallas_call(
        _merge_kernel,
        grid=(b, s // tm),
        in_specs=[tok(d)] + [dil(a) for a in (*oas, *lses)] + [tok(SB_WIDTH), tok(d), tok(d)]
                 + [full(a) for a in weights],
        out_specs=(tok(d), tok(d)),
        out_shape=(jax.ShapeDtypeStruct((b, s, d), F32), jax.ShapeDtypeStruct((b, s, d), BF16)),
        scratch_shapes=[pltpu.VMEM((DIL_OUT_WIDTH // LANES, tm, LANES), F32)] * 4,
        compiler_params=pltpu.CompilerParams(
            dimension_semantics=("parallel", "parallel"), vmem_limit_bytes=VMEM_LIMIT),
        name="merge",
    )(x, *oas, *lses, o_b, g_a, g_b, *weights)


MLP_TM = 256
MLP_FF_CHUNK = 1024


def _mlp_kernel(x1_ref, h2_ref, w1, w2, g_ref, out_ref, *, final_norm):
    h2 = h2_ref[...]
    y = x1_ref[...]
    for c in range(D_FF // MLP_FF_CHUNK):
        cs = slice(c * MLP_FF_CHUNK, (c + 1) * MLP_FF_CHUNK)
        u = jnp.maximum(_dot(h2, w1[:, cs]), 0.0)
        y = y + _dot((u * u).astype(BF16), w2[cs, :])
    out_ref[...] = _rms(y, g_ref[...]) if final_norm else y


def _mlp(x1, h2, w1, w2, g, final_norm):
    b, s, d = x1.shape
    tm = MLP_TM
    tok = lambda w: pl.BlockSpec((None, tm, w), lambda bi, i: (bi, i, 0))
    full = lambda a: pl.BlockSpec(a.shape, lambda bi, i: (0,) * a.ndim)
    return pl.pallas_call(
        functools.partial(_mlp_kernel, final_norm=final_norm),
        grid=(b, s // tm),
        in_specs=[tok(d), tok(d), full(w1), full(w2), full(g)],
        out_specs=tok(d),
        out_shape=jax.ShapeDtypeStruct((b, s, d), F32),
        compiler_params=pltpu.CompilerParams(
            dimension_semantics=("parallel", "parallel"), vmem_limit_bytes=VMEM_LIMIT),
        name="mlp",
    )(x1, h2, w1, w2, g)


def kernel(x, norm_mix_g, w_in, b_gate, w_up_dil, w_up_sb, w_out, norm_mlp_g, w_mlp_in,
           w_mlp_out, norm_final_g):
    depth = w_in.shape[0]
    row = lambda v: v.reshape(1, -1)
    for layer in range(depth):
        w = w_in[layer]
        c0 = 3 * DIL_WIDTH
        c1 = c0 + 3 * SB_WIDTH
        dil_cols = lambda i, grp: w[:, i * DIL_WIDTH + grp * DIL_OUT_WIDTH:
                                    i * DIL_WIDTH + (grp + 1) * DIL_OUT_WIDTH]
        wd = [jnp.concatenate([dil_cols(0, grp) * (1.0 / math.sqrt(HEAD_DIM)), dil_cols(1, grp),
                               dil_cols(2, grp)], axis=1).astype(BF16)
              for grp in range(len(DIL_GROUPS))]
        w = w.astype(BF16)
        wqs, wks, wvs = (w[:, c0 + i * SB_WIDTH:c0 + (i + 1) * SB_WIDTH] for i in range(3))
        wga, wgb = w[:, c1:c1 + D_MODEL], w[:, c1 + D_MODEL:]
        bga, bgb = row(b_gate[layer][:D_MODEL]), row(b_gate[layer][D_MODEL:])
        *qkv, qs, kst, vs, g_a, g_b = _proj(
            x, row(norm_mix_g[layer]), wd, wqs, wks.T, wvs, wga, wgb, bga, bgb)
        dil = [_dilated_group(qkv[grp], grp) for grp in range(len(DIL_GROUPS))]
        o_b = _sb_attention(qs, kst, vs)
        x1, h2 = _merge(x, [o for o, _ in dil], [l for _, l in dil], o_b, g_a, g_b,
                        w_up_dil[layer].astype(BF16), w_up_sb[layer].astype(BF16),
                        w_out[layer].astype(BF16), row(norm_mlp_g[layer]))
        x = _mlp(x1, h2, w_mlp_in[layer].astype(BF16), w_mlp_out[layer].astype(BF16),
                 row(norm_final_g), final_norm=layer == depth - 1)
    return x
```

```python
import functools
import math

import jax
import jax.numpy as jnp
from jax import lax
from jax.experimental import pallas as pl
from jax.experimental.pallas import tpu as pltpu

D_MODEL = 1024
HEAD_DIM = 64
DIL_GROUPS = ((128, 1), (512, 4), (2048, 16))
DIL_HEADS_PER_GROUP = 4
N_DIL_HEADS = DIL_HEADS_PER_GROUP * len(DIL_GROUPS)
N_SB_HEADS = 8
DIL_WIDTH = N_DIL_HEADS * HEAD_DIM
DIL_OUT_WIDTH = DIL_HEADS_PER_GROUP * HEAD_DIM
SB_WIDTH = N_SB_HEADS * HEAD_DIM
D_FF = 4 * D_MODEL
BLOCK = 128
RMS_EPS = 1e-6
NEG_INF = -1e30
LOG2E = 1.4426950408889634
LANES = 128

VMEM_LIMIT = 56 * 1024 * 1024

BF16 = jnp.bfloat16
F32 = jnp.float32


def _dot(a, b):
    return jnp.dot(a, b, preferred_element_type=F32)


def _dot_nt(a, b):
    return lax.dot_general(a, b, (((1,), (1,)), ((), ())), preferred_element_type=F32)


def _rms(x, g):
    return x * lax.rsqrt(jnp.mean(x * x, axis=-1, keepdims=True) + RMS_EPS) * g


def _sigmoid(x):
    return 1.0 / (1.0 + jnp.exp(-x))


DIL_TILE = 256
PROJ_TM = DIL_TILE


def _residue_major(h_ref, d):
    rows = DIL_TILE // d
    return jnp.concatenate(
        [jnp.concatenate([h_ref[c, pl.ds(r, rows, stride=d), :] for r in range(d)], axis=0)
         for c in range(h_ref.shape[0])], axis=1)


def _proj_kernel(x_ref, g_ref, wd0, wd1, wd2, wqs, wks_t, wvs, wga, wgb, bga, bgb,
                 d0_ref, d1_ref, d2_ref, qs_ref, kst_ref, vs_ref, ga_ref, gb_ref, h_scr):
    h32 = _rms(x_ref[...], g_ref[...])
    for c in range(h_scr.shape[0]):
        h_scr[c] = h32[:, c * LANES:(c + 1) * LANES]
    h = h32.astype(BF16)
    for (_, dilation), w_ref, o_ref in zip(DIL_GROUPS, (wd0, wd1, wd2), (d0_ref, d1_ref, d2_ref)):
        hd = h if dilation == 1 else _residue_major(h_scr, dilation).astype(BF16)
        o_ref[...] = _dot(hd, w_ref[...]).astype(BF16)
    qs_ref[...] = (_dot(h, wqs[...]) * (LOG2E / math.sqrt(HEAD_DIM))).astype(BF16)
    kst_ref[...] = _dot_nt(wks_t[...], h).astype(BF16)
    vs_ref[...] = _dot(h, wvs[...]).astype(BF16)
    ga_ref[...] = _sigmoid(_dot(h, wga[...]) + bga[...])
    gb_ref[...] = _sigmoid(_dot(h, wgb[...]) + bgb[...])


def _proj(x, g, wd, wqs, wks_t, wvs, wga, wgb, bga, bgb):
    b, s, d = x.shape
    tm = PROJ_TM
    tok = lambda w: pl.BlockSpec((None, tm, w), lambda bi, i: (bi, i, 0))
    full = lambda a: pl.BlockSpec(a.shape, lambda bi, i: (0,) * a.ndim)
    dil_width = 3 * DIL_OUT_WIDTH
    out_shape = (
        (jax.ShapeDtypeStruct((b, s, dil_width), BF16),) * len(DIL_GROUPS)
        + (jax.ShapeDtypeStruct((b, s, SB_WIDTH), BF16),
           jax.ShapeDtypeStruct((b, SB_WIDTH, s), BF16),
           jax.ShapeDtypeStruct((b, s, SB_WIDTH), BF16),
           jax.ShapeDtypeStruct((b, s, D_MODEL), F32),
           jax.ShapeDtypeStruct((b, s, D_MODEL), F32)))
    out_specs = (
        (tok(dil_width),) * len(DIL_GROUPS)
        + (tok(SB_WIDTH), pl.BlockSpec((None, SB_WIDTH, tm), lambda bi, i: (bi, 0, i)),
           tok(SB_WIDTH), tok(D_MODEL), tok(D_MODEL)))
    weights = (g, *wd, wqs, wks_t, wvs, wga, wgb, bga, bgb)
    return pl.pallas_call(
        _proj_kernel,
        grid=(b, s // tm),
        in_specs=[tok(d)] + [full(a) for a in weights],
        out_specs=out_specs,
        out_shape=out_shape,
        scratch_shapes=[pltpu.VMEM((d // LANES, tm, LANES), F32)],
        compiler_params=pltpu.CompilerParams(
            dimension_semantics=("parallel", "parallel"), vmem_limit_bytes=VMEM_LIMIT),
        name="proj",
    )(x, *weights)


DIL_QB = 4
DIL_QROWS = DIL_QB * BLOCK


def _alibi_slope(head):
    return 2.0 ** (-8.0 * (head + 1) / N_DIL_HEADS)


def _dil_kernel(q_ref, kp_ref, kc_ref, vp_ref, vc_ref, o_ref, lse_ref, *, group, dilation, n_steps):
    n = pl.program_id(2)
    width = DIL_OUT_WIDTH
    q = q_ref[...].reshape(DIL_QROWS, width)
    k = jnp.concatenate([kp_ref[...].reshape(BLOCK, width), kc_ref[...].reshape(DIL_QROWS, width)], axis=0)
    v = jnp.concatenate([vp_ref[...].reshape(BLOCK, width), vc_ref[...].reshape(DIL_QROWS, width)], axis=0)
    qi = lax.broadcasted_iota(jnp.int32, (BLOCK, 2 * BLOCK), 0)
    kj = lax.broadcasted_iota(jnp.int32, (BLOCK, 2 * BLOCK), 1)
    steps = qi + BLOCK - kj
    valid = (steps >= 0) & (steps <= n_steps)
    has_prev = (kj >= BLOCK) | (n > 0)
    dist = (steps * dilation).astype(F32)
    low = lax.broadcasted_iota(jnp.int32, (1, LANES), 1) < HEAD_DIM
    o_blocks = [[None] * 2 for _ in range(DIL_QB)]
    lse_blocks = [[None] * 2 for _ in range(DIL_QB)]
    for p in range(2):
        ps = slice(p * LANES, (p + 1) * LANES)
        qp, kp, vp = q[:, ps], k[:, ps], v[:, ps]
        sel = (low, jnp.logical_not(low))
        qm = [jnp.where(sel[e], qp, jnp.zeros_like(qp)) for e in range(2)]
        vm = [jnp.where(sel[e], vp, jnp.zeros_like(vp)) for e in range(2)]
        ones = [jnp.broadcast_to(jnp.where(sel[e], 1.0, 0.0).astype(BF16), (2 * BLOCK, LANES))
                for e in range(2)]
        bias = [jnp.where(valid, -_alibi_slope(group * DIL_HEADS_PER_GROUP + 2 * p + e) * dist,
                          NEG_INF) for e in range(2)]
        for j in range(DIL_QB):
            qs = slice(j * BLOCK, (j + 1) * BLOCK)
            ks = slice(j * BLOCK, (j + 2) * BLOCK)
            acc, ms = None, []
            for e in range(2):
                logits = _dot_nt(qm[e][qs], kp[ks]) + bias[e]
                if j == 0:
                    logits = jnp.where(has_prev, logits, NEG_INF)
                m = jnp.max(logits, axis=-1, keepdims=True)
                pr = jnp.exp(logits - m).astype(BF16)
                t = _dot(pr, jnp.concatenate([vm[e][ks], ones[e]], axis=1))
                acc = t if acc is None else acc + t
                ms.append(m)
            num, den = acc[:, :LANES], acc[:, LANES:]
            o_blocks[j][p] = num / den
            lse_blocks[j][p] = jnp.where(low, ms[0], ms[1]) + jnp.log(den)
    for j in range(DIL_QB):
        rows = o_ref.shape[-2] if len(o_ref.shape) == 3 else BLOCK
        tiles = slice(j * BLOCK // rows, (j + 1) * BLOCK // rows)
        for ref, blocks in ((o_ref, o_blocks), (lse_ref, lse_blocks)):
            val = jnp.concatenate(blocks[j], axis=1)
            if len(ref.shape) == 3:
                ref[tiles] = val.reshape(BLOCK // rows, rows, width)
            else:
                ref[j * BLOCK:(j + 1) * BLOCK, :] = val


def _dilated_group(qkv, group):
    window, dilation = DIL_GROUPS[group]
    b, s, _ = qkv.shape
    width = DIL_OUT_WIDTH
    n_grid = s // dilation // DIL_QROWS
    first_prev = lambda n: jnp.maximum(n * DIL_QB - 1, 0)
    if dilation == 1:
        view = qkv
        cur = lambda c: pl.BlockSpec((None, DIL_QROWS, width), lambda bi, r, n: (bi, n, c))
        prev = lambda c: pl.BlockSpec((None, BLOCK, width), lambda bi, r, n: (bi, first_prev(n), c))
        shp = jax.ShapeDtypeStruct((b, s, width), F32)
    else:
        rows = DIL_TILE // dilation
        view = qkv.reshape(b, s // DIL_TILE, dilation, rows, 3 * width)
        cur = lambda c: pl.BlockSpec((None, DIL_QROWS // rows, None, rows, width),
                                     lambda bi, r, n: (bi, n, r, 0, c))
        prev = lambda c: pl.BlockSpec((None, BLOCK // rows, None, rows, width),
                                      lambda bi, r, n: (bi, first_prev(n), r, 0, c))
        shp = jax.ShapeDtypeStruct((b, s // DIL_TILE, dilation, rows, width), F32)
    return pl.pallas_call(
        functools.partial(_dil_kernel, group=group, dilation=dilation,
                          n_steps=window // dilation),
        grid=(b, dilation, n_grid),
        in_specs=[cur(0), prev(1), cur(1), prev(2), cur(2)],
        out_specs=(cur(0), cur(0)),
        out_shape=(shp, shp),
        compiler_params=pltpu.CompilerParams(
            dimension_semantics=("parallel", "parallel", "parallel"),
            vmem_limit_bytes=VMEM_LIMIT),
        name=f"dilated_d{dilation}",
    )(view, view, view, view, view)


SB_TQ = 256
SB_TK = 256
HEADS_PER_STEP = LANES // HEAD_DIM


def _softplus2(z):
    return jnp.maximum(z, 0.0) + jnp.log2(1.0 + jnp.exp2(-jnp.abs(z)))


def _split_hi_lo(sp):
    parts = []
    for c in range(SB_TK // LANES):
        sp_c = sp[:, c * LANES:(c + 1) * LANES]
        hi = sp_c.astype(BF16)
        parts += [hi, (sp_c - hi.astype(F32)).astype(BF16)]
    return jnp.concatenate(parts, axis=1)


def _sb_weights(z, hl, tri, carry):
    n_chunks = SB_TK // LANES
    las = [None] * n_chunks
    for c in reversed(range(n_chunks)):
        r = _dot(hl[:, 2 * c * LANES:2 * (c + 1) * LANES], tri)
        las[c] = z[:, c * LANES:(c + 1) * LANES] - r[:, :LANES] - carry
        carry = carry + r[:, LANES:]
    return jnp.exp2(jnp.concatenate(las, axis=1)).astype(BF16), carry


SB_DEAD = 160.0
SB_BLOCKS_PER_ITER = 1


def _sb_kernel(q_ref, kt_ref, v_ref, tri_ref, o_ref, acc_ref, carry_ref):
    s = q_ref.shape[0]
    n_q = s // SB_TQ
    heads = range(HEADS_PER_STEP)
    hsl = lambda h: slice(h * HEAD_DIM, (h + 1) * HEAD_DIM)
    tri = tri_ref[...]

    k2 = jnp.square(kt_ref[...].astype(F32))
    k_max = [jnp.sqrt(jnp.max(jnp.sum(k2[hsl(h)], axis=0, keepdims=True))) for h in heads]

    row = lax.broadcasted_iota(jnp.int32, (SB_TQ, SB_TK), 0)
    col = lax.broadcasted_iota(jnp.int32, (SB_TQ, SB_TK), 1)
    causal = col < row

    def tile(h, q_h, kb, carry, diagonal=False):
        ks = pl.multiple_of(kb * SB_TK, SB_TK)
        z = _dot(q_h, kt_ref[hsl(h), pl.ds(ks, SB_TK)])
        if diagonal:
            z = jnp.where(causal, z, NEG_INF)
        a, carry = _sb_weights(z, _split_hi_lo(_softplus2(z)), tri, carry)
        return _dot(a, v_ref[pl.ds(ks, SB_TK), :]), carry

    def query_blocks(blocks, first_tiles):
        pending = []
        for slot, qi in enumerate(blocks):
            rows = pl.ds(pl.multiple_of(qi * SB_TQ, SB_TQ), SB_TQ)
            q = [q_ref[rows, hsl(h)] for h in heads]
            z_max = [jnp.sqrt(jnp.sum(jnp.square(q[h].astype(F32)), axis=1, keepdims=True))
                     * (k_max[h] * 1.001) + 1.0 for h in heads]
            carries = []
            for h in heads:
                acc, carry = tile(h, q[h], qi, jnp.zeros((SB_TQ, LANES), F32), diagonal=True)
                for t in range(1, first_tiles):
                    out, carry = tile(h, q[h], qi - t, carry)
                    acc = acc + out
                acc_ref[slot, h] = acc
                carry_ref[slot, h] = carry
                carries.append(carry)
            pending.append((slot, qi, rows, q, z_max, carries))

        for slot, qi, rows, q, z_max, carries in pending:
            def alive(carries):
                slack = [jnp.min(carries[h] - z_max[h]) for h in heads]
                return jnp.minimum(slack[0], slack[1]) < SB_DEAD

            def cond(state):
                kb, live = state
                return (kb >= 0) & live

            def body(state):
                kb, _ = state
                carries = []
                for h in heads:
                    out, carry = tile(h, q[h], kb, carry_ref[slot, h])
                    acc_ref[slot, h] += out
                    carry_ref[slot, h] = carry
                    carries.append(carry)
                return kb - 1, alive(carries)

            lax.while_loop(cond, body, (qi - first_tiles, alive(carries)))
            o_ref[rows, :] = jnp.concatenate([acc_ref[slot, h, :, hsl(h)] for h in heads],
                                             axis=1).astype(o_ref.dtype)

    query_blocks([jnp.int32(0)], 1)
    n_groups = (n_q - 1) // SB_BLOCKS_PER_ITER

    def group(i, _):
        query_blocks([1 + i * SB_BLOCKS_PER_ITER + j for j in range(SB_BLOCKS_PER_ITER)], 2)
        return 0

    lax.fori_loop(0, n_groups, group, 0)
    for qi in range(1 + n_groups * SB_BLOCKS_PER_ITER, n_q):
        query_blocks([jnp.int32(qi)], 2)


def _sb_attention(qs, kst, vs):
    b, s, _ = qs.shape
    assert SB_TQ == SB_TK and s % SB_TQ == 0 and HEADS_PER_STEP == 2
    t = (jnp.arange(LANES)[:, None] >= jnp.arange(LANES)[None, :]).astype(BF16)
    half = jnp.concatenate([t, jnp.ones((LANES, LANES), BF16)], axis=1)
    tri = jnp.concatenate([half, half], axis=0)
    nh = HEADS_PER_STEP
    return pl.pallas_call(
        _sb_kernel,
        grid=(b, SB_WIDTH // LANES),
        in_specs=[
            pl.BlockSpec((None, s, LANES), lambda bi, p: (bi, 0, p)),
            pl.BlockSpec((None, LANES, s), lambda bi, p: (bi, p, 0)),
            pl.BlockSpec((None, s, LANES), lambda bi, p: (bi, 0, p)),
            pl.BlockSpec(tri.shape, lambda bi, p: (0, 0)),
        ],
        out_specs=pl.BlockSpec((None, s, LANES), lambda bi, p: (bi, 0, p)),
        out_shape=jax.ShapeDtypeStruct((b, s, SB_WIDTH), BF16),
        scratch_shapes=[pltpu.VMEM((SB_BLOCKS_PER_ITER, nh, SB_TQ, LANES), F32)] * 2,
        compiler_params=pltpu.CompilerParams(
            dimension_semantics=("parallel", "parallel"), vmem_limit_bytes=VMEM_LIMIT),
        name="stick_breaking",
    )(qs, kst, vs, tri)


MERGE_TM = DIL_TILE
MLP_FF_CHUNK = 1024


def _token_major(ref, scr):
    d, rows, _ = ref.shape
    for c in range(scr.shape[0]):
        for r in range(d):
            scr[c, pl.ds(r, rows, stride=d), :] = ref[r, :, c * LANES:(c + 1) * LANES]
    return jnp.concatenate([scr[c] for c in range(scr.shape[0])], axis=1)


def _tail_kernel(x_ref, o0, o1, o2, l0, l1, l2, ob_ref, ga_ref, gb_ref,
                 wud, wus, wout, g_mlp, w1, w2, g_out, out_ref, *scratch, final_norm):
    outs = [o0[...], _token_major(o1, scratch[0]), _token_major(o2, scratch[1])]
    lse = [l0[...], _token_major(l1, scratch[2]), _token_major(l2, scratch[3])]
    m = jnp.maximum(jnp.maximum(lse[0], lse[1]), lse[2])
    e = [jnp.exp(l - m) for l in lse]
    o_a = (e[0] * outs[0] + e[1] * outs[1] + e[2] * outs[2]) / (e[0] + e[1] + e[2])
    up_a = _dot(o_a.astype(BF16), wud[...])
    up_b = _dot(ob_ref[...], wus[...])
    merged = ga_ref[...] * up_a + gb_ref[...] * up_b
    y = x_ref[...] + _dot(merged.astype(BF16), wout[...])
    h2 = _rms(y, g_mlp[...]).astype(BF16)
    for c in range(D_FF // MLP_FF_CHUNK):
        cs = slice(c * MLP_FF_CHUNK, (c + 1) * MLP_FF_CHUNK)
        u = jnp.maximum(_dot(h2, w1[:, cs]), 0.0)
        y = y + _dot((u * u).astype(BF16), w2[cs, :])
    out_ref[...] = _rms(y, g_out[...]) if final_norm else y


def _tail(x, oas, lses, o_b, g_a, g_b, wud, wus, wout, g_mlp, w1, w2, g_out, final_norm):
    b, s, d = x.shape
    tm = MERGE_TM
    tok = lambda w: pl.BlockSpec((None, tm, w), lambda bi, i: (bi, i, 0))
    full = lambda a: pl.BlockSpec(a.shape, lambda bi, i: (0,) * a.ndim,
                                  pipeline_mode=pl.Buffered(1))

    def dil(a):
        if a.ndim == 3:
            return tok(DIL_OUT_WIDTH)
        return pl.BlockSpec((None, None) + a.shape[2:], lambda bi, i: (bi, i, 0, 0, 0))

    weights = (wud, wus, wout, g_mlp, w1, w2, g_out)
    return pl.pallas_call(
        functools.partial(_tail_kernel, final_norm=final_norm),
        grid=(b, s // tm),
        in_specs=[tok(d)] + [dil(a) for a in (*oas, *lses)] + [tok(SB_WIDTH), tok(d), tok(d)]
                 + [full(a) for a in weights],
        out_specs=tok(d),
        out_shape=jax.ShapeDtypeStruct((b, s, d), F32),
        scratch_shapes=[pltpu.VMEM((DIL_OUT_WIDTH // LANES, tm, LANES), F32)] * 4,
        compiler_params=pltpu.CompilerParams(
            dimension_semantics=("parallel", "parallel"), vmem_limit_bytes=VMEM_LIMIT),
        name="tail",
    )(x, *oas, *lses, o_b, g_a, g_b, *weights)


def kernel(x, norm_mix_g, w_in, b_gate, w_up_dil, w_up_sb, w_out, norm_mlp_g, w_mlp_in,
           w_mlp_out, norm_final_g):
    depth = w_in.shape[0]
    row = lambda v: v.reshape(1, -1)
    for layer in range(depth):
        w = w_in[layer]
        c0 = 3 * DIL_WIDTH
        c1 = c0 + 3 * SB_WIDTH
        dil_cols = lambda i, grp: w[:, i * DIL_WIDTH + grp * DIL_OUT_WIDTH:
                                    i * DIL_WIDTH + (grp + 1) * DIL_OUT_WIDTH]
        wd = [jnp.concatenate([dil_cols(0, grp) * (1.0 / math.sqrt(HEAD_DIM)), dil_cols(1, grp),
                               dil_cols(2, grp)], axis=1).astype(BF16)
              for grp in range(len(DIL_GROUPS))]
        w = w.astype(BF16)
        wqs, wks, wvs = (w[:, c0 + i * SB_WIDTH:c0 + (i + 1) * SB_WIDTH] for i in range(3))
        wga, wgb = w[:, c1:c1 + D_MODEL], w[:, c1 + D_MODEL:]
        bga, bgb = row(b_gate[layer][:D_MODEL]), row(b_gate[layer][D_MODEL:])
        *qkv, qs, kst, vs, g_a, g_b = _proj(
            x, row(norm_mix_g[layer]), wd, wqs, wks.T, wvs, wga, wgb, bga, bgb)
        dil = [_dilated_group(qkv[grp], grp) for grp in range(len(DIL_GROUPS))]
        o_b = _sb_attention(qs, kst, vs)
        x = _tail(x, [o for o, _ in dil], [l for _, l in dil], o_b, g_a, g_b,
                  w_up_dil[layer].astype(BF16), w_up_sb[layer].astype(BF16),
                  w_out[layer].astype(BF16), row(norm_mlp_g[layer]),
                  w_mlp_in[layer].astype(BF16), w_mlp_out[layer].astype(BF16),
                  row(norm_final_g), final_norm=layer == depth - 1)
    return x
```

```python
import functools
import math

import jax
import jax.numpy as jnp
from jax import lax
from jax.experimental import pallas as pl
from jax.experimental.pallas import tpu as pltpu

D_MODEL = 1024
HEAD_DIM = 64
DIL_GROUPS = ((128, 1), (512, 4), (2048, 16))
DIL_HEADS_PER_GROUP = 4
N_DIL_HEADS = DIL_HEADS_PER_GROUP * len(DIL_GROUPS)
N_SB_HEADS = 8
DIL_WIDTH = N_DIL_HEADS * HEAD_DIM
DIL_OUT_WIDTH = DIL_HEADS_PER_GROUP * HEAD_DIM
SB_WIDTH = N_SB_HEADS * HEAD_DIM
D_FF = 4 * D_MODEL
BLOCK = 128
RMS_EPS = 1e-6
NEG_INF = -1e30
LOG2E = 1.4426950408889634
LANES = 128

VMEM_LIMIT = 56 * 1024 * 1024

BF16 = jnp.bfloat16
F32 = jnp.float32


def _dot(a, b):
    return jnp.dot(a, b, preferred_element_type=F32)


def _dot_nt(a, b):
    return lax.dot_general(a, b, (((1,), (1,)), ((), ())), preferred_element_type=F32)


def _rms(x, g):
    return x * lax.rsqrt(jnp.mean(x * x, axis=-1, keepdims=True) + RMS_EPS) * g


def _sigmoid(x):
    return 1.0 / (1.0 + jnp.exp(-x))


DIL_TILE = 256
PROJ_TILES = 2


def _residue_major(h_ref, d):
    rows = DIL_TILE // d
    return jnp.concatenate(
        [jnp.concatenate([h_ref[c, pl.ds(r, rows, stride=d), :] for r in range(d)], axis=0)
         for c in range(h_ref.shape[0])], axis=1)


def _proj_kernel(x_ref, g_ref, wd0, wd1, wd2, wqs, wks_t, wvs, wga, wgb, bga, bgb,
                 d0_ref, d1_ref, d2_ref, qs_ref, kst_ref, vs_ref, ga_ref, gb_ref, *h_scr):
    tiles = range(PROJ_TILES)
    rows = [slice(t * DIL_TILE, (t + 1) * DIL_TILE) for t in tiles]
    h = []
    for t in tiles:
        h32 = _rms(x_ref[rows[t], :], g_ref[...])
        for c in range(h_scr[t].shape[0]):
            h_scr[t][c] = h32[:, c * LANES:(c + 1) * LANES]
        h.append(h32.astype(BF16))
    for (_, dilation), w_ref, o_ref in zip(DIL_GROUPS, (wd0, wd1, wd2), (d0_ref, d1_ref, d2_ref)):
        for t in tiles:
            hd = h[t] if dilation == 1 else _residue_major(h_scr[t], dilation).astype(BF16)
            o_ref[rows[t], :] = _dot(hd, w_ref[...]).astype(BF16)
    for t in tiles:
        qs_ref[rows[t], :] = (_dot(h[t], wqs[...]) * (LOG2E / math.sqrt(HEAD_DIM))).astype(BF16)
    for t in tiles:
        kst_ref[:, rows[t]] = _dot_nt(wks_t[...], h[t]).astype(BF16)
    for t in tiles:
        vs_ref[rows[t], :] = _dot(h[t], wvs[...]).astype(BF16)
    for t in tiles:
        ga_ref[rows[t], :] = _sigmoid(_dot(h[t], wga[...]) + bga[...])
    for t in tiles:
        gb_ref[rows[t], :] = _sigmoid(_dot(h[t], wgb[...]) + bgb[...])


def _proj(x, g, wd, wqs, wks_t, wvs, wga, wgb, bga, bgb):
    b, s, d = x.shape
    tm = PROJ_TILES * DIL_TILE
    tok = lambda w: pl.BlockSpec((None, tm, w), lambda bi, i: (bi, i, 0))
    full = lambda a: pl.BlockSpec(a.shape, lambda bi, i: (0,) * a.ndim,
                                  pipeline_mode=pl.Buffered(1))
    dil_width = 3 * DIL_OUT_WIDTH
    out_shape = (
        (jax.ShapeDtypeStruct((b, s, dil_width), BF16),) * len(DIL_GROUPS)
        + (jax.ShapeDtypeStruct((b, s, SB_WIDTH), BF16),
           jax.ShapeDtypeStruct((b, SB_WIDTH, s), BF16),
           jax.ShapeDtypeStruct((b, s, SB_WIDTH), BF16),
           jax.ShapeDtypeStruct((b, s, D_MODEL), F32),
           jax.ShapeDtypeStruct((b, s, D_MODEL), F32)))
    out_specs = (
        (tok(dil_width),) * len(DIL_GROUPS)
        + (tok(SB_WIDTH), pl.BlockSpec((None, SB_WIDTH, tm), lambda bi, i: (bi, 0, i)),
           tok(SB_WIDTH), tok(D_MODEL), tok(D_MODEL)))
    weights = (g, *wd, wqs, wks_t, wvs, wga, wgb, bga, bgb)
    return pl.pallas_call(
        _proj_kernel,
        grid=(b, s // tm),
        in_specs=[tok(d)] + [full(a) for a in weights],
        out_specs=out_specs,
        out_shape=out_shape,
        scratch_shapes=[pltpu.VMEM((d // LANES, DIL_TILE, LANES), F32)] * PROJ_TILES,
        compiler_params=pltpu.CompilerParams(
            dimension_semantics=("parallel", "parallel"), vmem_limit_bytes=VMEM_LIMIT),
        name="proj",
    )(x, *weights)


DIL_QB = 4
DIL_QROWS = DIL_QB * BLOCK


def _alibi_slope(head):
    return 2.0 ** (-8.0 * (head + 1) / N_DIL_HEADS)


def _dil_kernel(q_ref, kp_ref, kc_ref, vp_ref, vc_ref, o_ref, lse_ref, *, group, dilation, n_steps):
    n = pl.program_id(2)
    width = DIL_OUT_WIDTH
    q = q_ref[...].reshape(DIL_QROWS, width)
    k = jnp.concatenate([kp_ref[...].reshape(BLOCK, width), kc_ref[...].reshape(DIL_QROWS, width)], axis=0)
    v = jnp.concatenate([vp_ref[...].reshape(BLOCK, width), vc_ref[...].reshape(DIL_QROWS, width)], axis=0)
    qi = lax.broadcasted_iota(jnp.int32, (BLOCK, 2 * BLOCK), 0)
    kj = lax.broadcasted_iota(jnp.int32, (BLOCK, 2 * BLOCK), 1)
    steps = qi + BLOCK - kj
    valid = (steps >= 0) & (steps <= n_steps)
    has_prev = (kj >= BLOCK) | (n > 0)
    dist = (steps * dilation).astype(F32)
    low = lax.broadcasted_iota(jnp.int32, (1, LANES), 1) < HEAD_DIM
    o_blocks = [[None] * 2 for _ in range(DIL_QB)]
    lse_blocks = [[None] * 2 for _ in range(DIL_QB)]
    sel = (low, jnp.logical_not(low))
    ones = [jnp.broadcast_to(jnp.where(sel[e], 1.0, 0.0).astype(BF16), (2 * BLOCK, LANES))
            for e in range(2)]
    work = [(p, j, e) for p in range(2) for j in range(DIL_QB) for e in range(2)]
    q_rows = lambda j: slice(j * BLOCK, (j + 1) * BLOCK)
    k_rows = lambda j: slice(j * BLOCK, (j + 2) * BLOCK)
    pair = lambda t, p: t[:, p * LANES:(p + 1) * LANES]
    qm = {(p, e): jnp.where(sel[e], pair(q, p), jnp.zeros((DIL_QROWS, LANES), BF16))
          for p in range(2) for e in range(2)}
    vm = {(p, e): jnp.where(sel[e], pair(v, p), jnp.zeros((BLOCK + DIL_QROWS, LANES), BF16))
          for p in range(2) for e in range(2)}
    bias = {(p, e): jnp.where(valid, -_alibi_slope(group * DIL_HEADS_PER_GROUP + 2 * p + e) * dist,
                              NEG_INF) for p in range(2) for e in range(2)}
    logits = {}
    for p, j, e in work:
        lg = _dot_nt(qm[p, e][q_rows(j)], pair(k, p)[k_rows(j)]) + bias[p, e]
        logits[p, j, e] = jnp.where(has_prev, lg, NEG_INF) if j == 0 else lg
    ms = {w: jnp.max(logits[w], axis=-1, keepdims=True) for w in work}
    pr = {w: jnp.exp(logits[w] - ms[w]).astype(BF16) for w in work}
    t = {(p, j, e): _dot(pr[p, j, e], jnp.concatenate([vm[p, e][k_rows(j)], ones[e]], axis=1))
         for p, j, e in work}
    for p in range(2):
        for j in range(DIL_QB):
            acc = t[p, j, 0] + t[p, j, 1]
            num, den = acc[:, :LANES], acc[:, LANES:]
            o_blocks[j][p] = num / den
            lse_blocks[j][p] = jnp.where(low, ms[p, j, 0], ms[p, j, 1]) + jnp.log(den)
    for j in range(DIL_QB):
        rows = o_ref.shape[-2] if len(o_ref.shape) == 3 else BLOCK
        tiles = slice(j * BLOCK // rows, (j + 1) * BLOCK // rows)
        for ref, blocks in ((o_ref, o_blocks), (lse_ref, lse_blocks)):
            val = jnp.concatenate(blocks[j], axis=1)
            if len(ref.shape) == 3:
                ref[tiles] = val.reshape(BLOCK // rows, rows, width)
            else:
                ref[j * BLOCK:(j + 1) * BLOCK, :] = val


def _dilated_group(qkv, group):
    window, dilation = DIL_GROUPS[group]
    b, s, _ = qkv.shape
    width = DIL_OUT_WIDTH
    n_grid = s // dilation // DIL_QROWS
    first_prev = lambda n: jnp.maximum(n * DIL_QB - 1, 0)
    if dilation == 1:
        view = qkv
        cur = lambda c: pl.BlockSpec((None, DIL_QROWS, width), lambda bi, r, n: (bi, n, c))
        prev = lambda c: pl.BlockSpec((None, BLOCK, width), lambda bi, r, n: (bi, first_prev(n), c))
        shp = jax.ShapeDtypeStruct((b, s, width), F32)
    else:
        rows = DIL_TILE // dilation
        view = qkv.reshape(b, s // DIL_TILE, dilation, rows, 3 * width)
        cur = lambda c: pl.BlockSpec((None, DIL_QROWS // rows, None, rows, width),
                                     lambda bi, r, n: (bi, n, r, 0, c))
        prev = lambda c: pl.BlockSpec((None, BLOCK // rows, None, rows, width),
                                      lambda bi, r, n: (bi, first_prev(n), r, 0, c))
        shp = jax.ShapeDtypeStruct((b, s // DIL_TILE, dilation, rows, width), F32)
    return pl.pallas_call(
        functools.partial(_dil_kernel, group=group, dilation=dilation,
                          n_steps=window // dilation),
        grid=(b, dilation, n_grid),
        in_specs=[cur(0), prev(1), cur(1), prev(2), cur(2)],
        out_specs=(cur(0), cur(0)),
        out_shape=(shp, shp),
        compiler_params=pltpu.CompilerParams(
            dimension_semantics=("parallel", "parallel", "parallel"),
            vmem_limit_bytes=VMEM_LIMIT),
        name=f"dilated_d{dilation}",
    )(view, view, view, view, view)


SB_TQ = 256
SB_TK = 256
HEADS_PER_STEP = LANES // HEAD_DIM


def _softplus2(z):
    return jnp.maximum(z, 0.0) + jnp.log2(1.0 + jnp.exp2(-jnp.abs(z)))


def _split_hi_lo(sp):
    parts = []
    for c in range(SB_TK // LANES):
        sp_c = sp[:, c * LANES:(c + 1) * LANES]
        hi = sp_c.astype(BF16)
        parts += [hi, (sp_c - hi.astype(F32)).astype(BF16)]
    return jnp.concatenate(parts, axis=1)


def _sb_suffix_sums(hl, tri):
    return [_dot(hl[:, 2 * c * LANES:2 * (c + 1) * LANES], tri) for c in range(SB_TK // LANES)]


def _sb_weights(z, sums, carry):
    n_chunks = SB_TK // LANES
    las = [None] * n_chunks
    for c in reversed(range(n_chunks)):
        las[c] = z[:, c * LANES:(c + 1) * LANES] - sums[c][:, :LANES] - carry
        carry = carry + sums[c][:, LANES:]
    return jnp.exp2(jnp.concatenate(las, axis=1)).astype(BF16), carry


SB_DEAD = 160.0
SB_BLOCKS_PER_ITER = 1


def _sb_kernel(q_ref, kt_ref, v_ref, tri_ref, o_ref, acc_ref, carry_ref):
    s = q_ref.shape[0]
    n_q = s // SB_TQ
    heads = range(HEADS_PER_STEP)
    hsl = lambda h: slice(h * HEAD_DIM, (h + 1) * HEAD_DIM)
    tri = tri_ref[...]

    k2 = jnp.square(kt_ref[...].astype(F32))
    k_max = [jnp.sqrt(jnp.max(jnp.sum(k2[hsl(h)], axis=0, keepdims=True))) for h in heads]

    row = lax.broadcasted_iota(jnp.int32, (SB_TQ, SB_TK), 0)
    col = lax.broadcasted_iota(jnp.int32, (SB_TQ, SB_TK), 1)
    causal = col < row

    def key_start(kb):
        return pl.multiple_of(kb * SB_TK, SB_TK)

    def logits(h, q_h, kb, diagonal=False):
        z = _dot(q_h, kt_ref[hsl(h), pl.ds(key_start(kb), SB_TK)])
        return jnp.where(causal, z, NEG_INF) if diagonal else z

    def tile(h, q_h, kb, carry):
        z = logits(h, q_h, kb)
        a, carry = _sb_weights(z, _sb_suffix_sums(_split_hi_lo(_softplus2(z)), tri), carry)
        return _dot(a, v_ref[pl.ds(key_start(kb), SB_TK), :]), carry

    def query_blocks(blocks, first_tiles):
        pending = []
        work = [(slot, h, t) for slot in range(len(blocks)) for h in heads
                for t in range(first_tiles)]
        q, z_max = {}, {}
        for slot, qi in enumerate(blocks):
            rows = pl.ds(pl.multiple_of(qi * SB_TQ, SB_TQ), SB_TQ)
            for h in heads:
                q[slot, h] = q_ref[rows, hsl(h)]
                z_max[slot, h] = (jnp.sqrt(jnp.sum(jnp.square(q[slot, h].astype(F32)), axis=1,
                                                   keepdims=True)) * (k_max[h] * 1.001) + 1.0)
        z = {(slot, h, t): logits(h, q[slot, h], blocks[slot] - t, diagonal=t == 0)
             for slot, h, t in work}
        hl = {w: _split_hi_lo(_softplus2(z[w])) for w in work}
        sums = {w: _sb_suffix_sums(hl[w], tri) for w in work}
        a, carry = {}, {}
        for slot, h, t in work:
            prev = carry[slot, h] if t else jnp.zeros((SB_TQ, LANES), F32)
            a[slot, h, t], carry[slot, h] = _sb_weights(z[slot, h, t], sums[slot, h, t], prev)
        for slot, qi in enumerate(blocks):
            for h in heads:
                acc = None
                for t in range(first_tiles):
                    out = _dot(a[slot, h, t], v_ref[pl.ds(key_start(qi - t), SB_TK), :])
                    acc = out if acc is None else acc + out
                acc_ref[slot, h] = acc
                carry_ref[slot, h] = carry[slot, h]
            rows = pl.ds(pl.multiple_of(qi * SB_TQ, SB_TQ), SB_TQ)
            pending.append((slot, qi, rows, [q[slot, h] for h in heads],
                            [z_max[slot, h] for h in heads], [carry[slot, h] for h in heads]))

        for slot, qi, rows, q, z_max, carries in pending:
            def alive(carries):
                slack = [jnp.min(carries[h] - z_max[h]) for h in heads]
                return jnp.minimum(slack[0], slack[1]) < SB_DEAD

            def cond(state):
                kb, live = state
                return (kb >= 0) & live

            def body(state):
                kb, _ = state
                carries = []
                for h in heads:
                    out, carry = tile(h, q[h], kb, carry_ref[slot, h])
                    acc_ref[slot, h] += out
                    carry_ref[slot, h] = carry
                    carries.append(carry)
                return kb - 1, alive(carries)

            lax.while_loop(cond, body, (qi - first_tiles, alive(carries)))
            o_ref[rows, :] = jnp.concatenate([acc_ref[slot, h, :, hsl(h)] for h in heads],
                                             axis=1).astype(o_ref.dtype)

    query_blocks([jnp.int32(0)], 1)
    n_groups = (n_q - 1) // SB_BLOCKS_PER_ITER

    def group(i, _):
        query_blocks([1 + i * SB_BLOCKS_PER_ITER + j for j in range(SB_BLOCKS_PER_ITER)], 2)
        return 0

    lax.fori_loop(0, n_groups, group, 0)
    for qi in range(1 + n_groups * SB_BLOCKS_PER_ITER, n_q):
        query_blocks([jnp.int32(qi)], 2)


def _sb_attention(qs, kst, vs):
    b, s, _ = qs.shape
    assert SB_TQ == SB_TK and s % SB_TQ == 0 and HEADS_PER_STEP == 2
    t = (jnp.arange(LANES)[:, None] >= jnp.arange(LANES)[None, :]).astype(BF16)
    half = jnp.concatenate([t, jnp.ones((LANES, LANES), BF16)], axis=1)
    tri = jnp.concatenate([half, half], axis=0)
    nh = HEADS_PER_STEP
    return pl.pallas_call(
        _sb_kernel,
        grid=(b, SB_WIDTH // LANES),
        in_specs=[
            pl.BlockSpec((None, s, LANES), lambda bi, p: (bi, 0, p)),
            pl.BlockSpec((None, LANES, s), lambda bi, p: (bi, p, 0)),
            pl.BlockSpec((None, s, LANES), lambda bi, p: (bi, 0, p)),
            pl.BlockSpec(tri.shape, lambda bi, p: (0, 0)),
        ],
        out_specs=pl.BlockSpec((None, s, LANES), lambda bi, p: (bi, 0, p)),
        out_shape=jax.ShapeDtypeStruct((b, s, SB_WIDTH), BF16),
        scratch_shapes=[pltpu.VMEM((SB_BLOCKS_PER_ITER, nh, SB_TQ, LANES), F32)] * 2,
        compiler_params=pltpu.CompilerParams(
            dimension_semantics=("parallel", "parallel"), vmem_limit_bytes=VMEM_LIMIT),
        name="stick_breaking",
    )(qs, kst, vs, tri)


TAIL_TILES = 2
MLP_FF_CHUNK = 1024


def _token_major(ref, scr):
    d, rows, _ = ref.shape
    for c in range(scr.shape[0]):
        for r in range(d):
            scr[c, pl.ds(r, rows, stride=d), :] = ref[r, :, c * LANES:(c + 1) * LANES]
    return jnp.concatenate([scr[c] for c in range(scr.shape[0])], axis=1)


def _tail_kernel(x_ref, o0, o1, o2, l0, l1, l2, ob_ref, ga_ref, gb_ref,
                 wud, wus, wout, g_mlp, w1, w2, g_out, out_ref, *scratch, final_norm):
    tiles = range(TAIL_TILES)
    rows = [slice(t * DIL_TILE, (t + 1) * DIL_TILE) for t in tiles]
    o_a = []
    for t in tiles:
        scr = scratch[4 * t:4 * t + 4]
        outs = [o0[rows[t], :], _token_major(o1.at[t], scr[0]), _token_major(o2.at[t], scr[1])]
        lse = [l0[rows[t], :], _token_major(l1.at[t], scr[2]), _token_major(l2.at[t], scr[3])]
        m = jnp.maximum(jnp.maximum(lse[0], lse[1]), lse[2])
        e = [jnp.exp(l - m) for l in lse]
        o_a.append((e[0] * outs[0] + e[1] * outs[1] + e[2] * outs[2]) / (e[0] + e[1] + e[2]))
    up_a = [_dot(o_a[t].astype(BF16), wud[...]) for t in tiles]
    up_b = [_dot(ob_ref[rows[t], :], wus[...]) for t in tiles]
    merged = [ga_ref[rows[t], :] * up_a[t] + gb_ref[rows[t], :] * up_b[t] for t in tiles]
    y = [x_ref[rows[t], :] + _dot(merged[t].astype(BF16), wout[...]) for t in tiles]
    h2 = [_rms(y[t], g_mlp[...]).astype(BF16) for t in tiles]
    for c in range(D_FF // MLP_FF_CHUNK):
        cs = slice(c * MLP_FF_CHUNK, (c + 1) * MLP_FF_CHUNK)
        u = [jnp.maximum(_dot(h2[t], w1[:, cs]), 0.0) for t in tiles]
        y = [y[t] + _dot((u[t] * u[t]).astype(BF16), w2[cs, :]) for t in tiles]
    for t in tiles:
        out_ref[rows[t], :] = _rms(y[t], g_out[...]) if final_norm else y[t]


def _tail(x, oas, lses, o_b, g_a, g_b, wud, wus, wout, g_mlp, w1, w2, g_out, final_norm):
    b, s, d = x.shape
    tm = TAIL_TILES * DIL_TILE
    tok = lambda w: pl.BlockSpec((None, tm, w), lambda bi, i: (bi, i, 0))
    full = lambda a: pl.BlockSpec(a.shape, lambda bi, i: (0,) * a.ndim,
                                  pipeline_mode=pl.Buffered(1))

    def dil(a):
        if a.ndim == 3:
            return tok(DIL_OUT_WIDTH)
        return pl.BlockSpec((None, TAIL_TILES) + a.shape[2:], lambda bi, i: (bi, i, 0, 0, 0))

    weights = (wud, wus, wout, g_mlp, w1, w2, g_out)
    return pl.pallas_call(
        functools.partial(_tail_kernel, final_norm=final_norm),
        grid=(b, s // tm),
        in_specs=[tok(d)] + [dil(a) for a in (*oas, *lses)] + [tok(SB_WIDTH), tok(d), tok(d)]
                 + [full(a) for a in weights],
        out_specs=tok(d),
        out_shape=jax.ShapeDtypeStruct((b, s, d), F32),
        scratch_shapes=[pltpu.VMEM((DIL_OUT_WIDTH // LANES, DIL_TILE, LANES), F32)]
                       * (4 * TAIL_TILES),
        compiler_params=pltpu.CompilerParams(
            dimension_semantics=("parallel", "parallel"), vmem_limit_bytes=VMEM_LIMIT),
        name="tail",
    )(x, *oas, *lses, o_b, g_a, g_b, *weights)


def kernel(x, norm_mix_g, w_in, b_gate, w_up_dil, w_up_sb, w_out, norm_mlp_g, w_mlp_in,
           w_mlp_out, norm_final_g):
    depth = w_in.shape[0]
    row = lambda v: v.reshape(1, -1)
    for layer in range(depth):
        w = w_in[layer]
        c0 = 3 * DIL_WIDTH
        c1 = c0 + 3 * SB_WIDTH
        dil_cols = lambda i, grp: w[:, i * DIL_WIDTH + grp * DIL_OUT_WIDTH:
                                    i * DIL_WIDTH + (grp + 1) * DIL_OUT_WIDTH]
        wd = [jnp.concatenate([dil_cols(0, grp) * (1.0 / math.sqrt(HEAD_DIM)), dil_cols(1, grp),
                               dil_cols(2, grp)], axis=1).astype(BF16)
              for grp in range(len(DIL_GROUPS))]
        w = w.astype(BF16)
        wqs, wks, wvs = (w[:, c0 + i * SB_WIDTH:c0 + (i + 1) * SB_WIDTH] for i in range(3))
        wga, wgb = w[:, c1:c1 + D_MODEL], w[:, c1 + D_MODEL:]
        bga, bgb = row(b_gate[layer][:D_MODEL]), row(b_gate[layer][D_MODEL:])
        *qkv, qs, kst, vs, g_a, g_b = _proj(
            x, row(norm_mix_g[layer]), wd, wqs, wks.T, wvs, wga, wgb, bga, bgb)
        dil = [_dilated_group(qkv[grp], grp) for grp in range(len(DIL_GROUPS))]
        o_b = _sb_attention(qs, kst, vs)
        x = _tail(x, [o for o, _ in dil], [l for _, l in dil], o_b, g_a, g_b,
                  w_up_dil[layer].astype(BF16), w_up_sb[layer].astype(BF16),
                  w_out[layer].astype(BF16), row(norm_mlp_g[layer]),
                  w_mlp_in[layer].astype(BF16), w_mlp_out[layer].astype(BF16),
                  row(norm_final_g), final_norm=layer == depth - 1)
    return x
```

```python
import functools
import math

import jax
import jax.numpy as jnp
from jax import lax
from jax.experimental import pallas as pl
from jax.experimental.pallas import tpu as pltpu

D_MODEL = 1024
HEAD_DIM = 64
DIL_GROUPS = ((128, 1), (512, 4), (2048, 16))
DIL_HEADS_PER_GROUP = 4
N_DIL_HEADS = DIL_HEADS_PER_GROUP * len(DIL_GROUPS)
N_SB_HEADS = 8
DIL_WIDTH = N_DIL_HEADS * HEAD_DIM
DIL_OUT_WIDTH = DIL_HEADS_PER_GROUP * HEAD_DIM
SB_WIDTH = N_SB_HEADS * HEAD_DIM
D_FF = 4 * D_MODEL
BLOCK = 128
RMS_EPS = 1e-6
NEG_INF = -1e30
LOG2E = 1.4426950408889634
LANES = 128

VMEM_LIMIT = 56 * 1024 * 1024

BF16 = jnp.bfloat16
F32 = jnp.float32


def _dot(a, b):
    return jnp.dot(a, b, preferred_element_type=F32)


def _dot_nt(a, b):
    return lax.dot_general(a, b, (((1,), (1,)), ((), ())), preferred_element_type=F32)


def _rms(x, g):
    return x * lax.rsqrt(jnp.mean(x * x, axis=-1, keepdims=True) + RMS_EPS) * g


def _sigmoid(x):
    return 1.0 / (1.0 + jnp.exp(-x))


DIL_TILE = 256
PROJ_TILES = 2


def _residue_major(h_ref, d):
    rows = DIL_TILE // d
    return jnp.concatenate(
        [jnp.concatenate([h_ref[c, pl.ds(r, rows, stride=d), :] for r in range(d)], axis=0)
         for c in range(h_ref.shape[0])], axis=1)


def _proj_kernel(x_ref, g_ref, wd0, wd1, wd2, wqs, wks_t, wvs, wga, wgb, bga, bgb,
                 d0_ref, d1_ref, d2_ref, qs_ref, kst_ref, vs_ref, ga_ref, gb_ref, *h_scr):
    tiles = range(PROJ_TILES)
    rows = [slice(t * DIL_TILE, (t + 1) * DIL_TILE) for t in tiles]
    h = []
    for t in tiles:
        h32 = _rms(x_ref[rows[t], :], g_ref[...])
        for c in range(h_scr[t].shape[0]):
            h_scr[t][c] = h32[:, c * LANES:(c + 1) * LANES]
        h.append(h32.astype(BF16))
    for (_, dilation), w_ref, o_ref in zip(DIL_GROUPS, (wd0, wd1, wd2), (d0_ref, d1_ref, d2_ref)):
        for t in tiles:
            hd = h[t] if dilation == 1 else _residue_major(h_scr[t], dilation).astype(BF16)
            o_ref[rows[t], :] = _dot(hd, w_ref[...]).astype(BF16)
    for t in tiles:
        qs_ref[rows[t], :] = (_dot(h[t], wqs[...]) * (LOG2E / math.sqrt(HEAD_DIM))).astype(BF16)
    for t in tiles:
        kst_ref[:, rows[t]] = _dot_nt(wks_t[...], h[t]).astype(BF16)
    for t in tiles:
        vs_ref[rows[t], :] = _dot(h[t], wvs[...]).astype(BF16)
    for t in tiles:
        ga_ref[rows[t], :] = _sigmoid(_dot(h[t], wga[...]) + bga[...])
    for t in tiles:
        gb_ref[rows[t], :] = _sigmoid(_dot(h[t], wgb[...]) + bgb[...])


def _proj(x, g, wd, wqs, wks_t, wvs, wga, wgb, bga, bgb):
    b, s, d = x.shape
    tm = PROJ_TILES * DIL_TILE
    tok = lambda w: pl.BlockSpec((None, tm, w), lambda bi, i: (bi, i, 0))
    full = lambda a: pl.BlockSpec(a.shape, lambda bi, i: (0,) * a.ndim,
                                  pipeline_mode=pl.Buffered(1))
    dil_width = 3 * DIL_OUT_WIDTH
    out_shape = (
        (jax.ShapeDtypeStruct((b, s, dil_width), BF16),) * len(DIL_GROUPS)
        + (jax.ShapeDtypeStruct((b, s, SB_WIDTH), BF16),
           jax.ShapeDtypeStruct((b, SB_WIDTH, s), BF16),
           jax.ShapeDtypeStruct((b, s, SB_WIDTH), BF16),
           jax.ShapeDtypeStruct((b, s, D_MODEL), F32),
           jax.ShapeDtypeStruct((b, s, D_MODEL), F32)))
    out_specs = (
        (tok(dil_width),) * len(DIL_GROUPS)
        + (tok(SB_WIDTH), pl.BlockSpec((None, SB_WIDTH, tm), lambda bi, i: (bi, 0, i)),
           tok(SB_WIDTH), tok(D_MODEL), tok(D_MODEL)))
    weights = (g, *wd, wqs, wks_t, wvs, wga, wgb, bga, bgb)
    return pl.pallas_call(
        _proj_kernel,
        grid=(b, s // tm),
        in_specs=[tok(d)] + [full(a) for a in weights],
        out_specs=out_specs,
        out_shape=out_shape,
        scratch_shapes=[pltpu.VMEM((d // LANES, DIL_TILE, LANES), F32)] * PROJ_TILES,
        compiler_params=pltpu.CompilerParams(
            dimension_semantics=("parallel", "parallel"), vmem_limit_bytes=VMEM_LIMIT),
        name="proj",
    )(x, *weights)


DIL_QB = 4
DIL_QROWS = DIL_QB * BLOCK


def _alibi_slope(head):
    return 2.0 ** (-8.0 * (head + 1) / N_DIL_HEADS)


def _dil_kernel(q_ref, kp_ref, kc_ref, vp_ref, vc_ref, o_ref, lse_ref, *, group, dilation, n_steps):
    n = pl.program_id(2)
    width = DIL_OUT_WIDTH
    q = q_ref[...].reshape(DIL_QROWS, width)
    k = jnp.concatenate([kp_ref[...].reshape(BLOCK, width), kc_ref[...].reshape(DIL_QROWS, width)], axis=0)
    v = jnp.concatenate([vp_ref[...].reshape(BLOCK, width), vc_ref[...].reshape(DIL_QROWS, width)], axis=0)
    qi = lax.broadcasted_iota(jnp.int32, (BLOCK, 2 * BLOCK), 0)
    kj = lax.broadcasted_iota(jnp.int32, (BLOCK, 2 * BLOCK), 1)
    steps = qi + BLOCK - kj
    valid = (steps >= 0) & (steps <= n_steps)
    has_prev = (kj >= BLOCK) | (n > 0)
    dist = (steps * dilation).astype(F32)
    low = lax.broadcasted_iota(jnp.int32, (1, LANES), 1) < HEAD_DIM
    o_blocks = [[None] * 2 for _ in range(DIL_QB)]
    lse_blocks = [[None] * 2 for _ in range(DIL_QB)]
    sel = (low, jnp.logical_not(low))
    ones = [jnp.broadcast_to(jnp.where(sel[e], 1.0, 0.0).astype(BF16), (2 * BLOCK, LANES))
            for e in range(2)]
    work = [(p, j, e) for p in range(2) for j in range(DIL_QB) for e in range(2)]
    q_rows = lambda j: slice(j * BLOCK, (j + 1) * BLOCK)
    k_rows = lambda j: slice(j * BLOCK, (j + 2) * BLOCK)
    pair = lambda t, p: t[:, p * LANES:(p + 1) * LANES]
    qm = {(p, e): jnp.where(sel[e], pair(q, p), jnp.zeros((DIL_QROWS, LANES), BF16))
          for p in range(2) for e in range(2)}
    vm = {(p, e): jnp.where(sel[e], pair(v, p), jnp.zeros((BLOCK + DIL_QROWS, LANES), BF16))
          for p in range(2) for e in range(2)}
    bias = {(p, e): jnp.where(valid, -_alibi_slope(group * DIL_HEADS_PER_GROUP + 2 * p + e) * dist,
                              NEG_INF) for p in range(2) for e in range(2)}
    logits = {}
    for p, j, e in work:
        lg = _dot_nt(qm[p, e][q_rows(j)], pair(k, p)[k_rows(j)]) + bias[p, e]
        logits[p, j, e] = jnp.where(has_prev, lg, NEG_INF) if j == 0 else lg
    ms = {w: jnp.max(logits[w], axis=-1, keepdims=True) for w in work}
    pr = {w: jnp.exp(logits[w] - ms[w]).astype(BF16) for w in work}
    t = {(p, j, e): _dot(pr[p, j, e], jnp.concatenate([vm[p, e][k_rows(j)], ones[e]], axis=1))
         for p, j, e in work}
    for p in range(2):
        for j in range(DIL_QB):
            acc = t[p, j, 0] + t[p, j, 1]
            num, den = acc[:, :LANES], acc[:, LANES:]
            o_blocks[j][p] = num / den
            lse_blocks[j][p] = jnp.where(low, ms[p, j, 0], ms[p, j, 1]) + jnp.log(den)
    for j in range(DIL_QB):
        rows = o_ref.shape[-2] if len(o_ref.shape) == 3 else BLOCK
        tiles = slice(j * BLOCK // rows, (j + 1) * BLOCK // rows)
        for ref, blocks in ((o_ref, o_blocks), (lse_ref, lse_blocks)):
            val = jnp.concatenate(blocks[j], axis=1)
            if len(ref.shape) == 3:
                ref[tiles] = val.reshape(BLOCK // rows, rows, width)
            else:
                ref[j * BLOCK:(j + 1) * BLOCK, :] = val


def _dilated_group(qkv, group):
    window, dilation = DIL_GROUPS[group]
    b, s, _ = qkv.shape
    width = DIL_OUT_WIDTH
    n_grid = s // dilation // DIL_QROWS
    first_prev = lambda n: jnp.maximum(n * DIL_QB - 1, 0)
    if dilation == 1:
        view = qkv
        cur = lambda c: pl.BlockSpec((None, DIL_QROWS, width), lambda bi, r, n: (bi, n, c))
        prev = lambda c: pl.BlockSpec((None, BLOCK, width), lambda bi, r, n: (bi, first_prev(n), c))
        shp = jax.ShapeDtypeStruct((b, s, width), F32)
    else:
        rows = DIL_TILE // dilation
        view = qkv.reshape(b, s // DIL_TILE, dilation, rows, 3 * width)
        cur = lambda c: pl.BlockSpec((None, DIL_QROWS // rows, None, rows, width),
                                     lambda bi, r, n: (bi, n, r, 0, c))
        prev = lambda c: pl.BlockSpec((None, BLOCK // rows, None, rows, width),
                                      lambda bi, r, n: (bi, first_prev(n), r, 0, c))
        shp = jax.ShapeDtypeStruct((b, s // DIL_TILE, dilation, rows, width), F32)
    return pl.pallas_call(
        functools.partial(_dil_kernel, group=group, dilation=dilation,
                          n_steps=window // dilation),
        grid=(b, dilation, n_grid),
        in_specs=[cur(0), prev(1), cur(1), prev(2), cur(2)],
        out_specs=(cur(0), cur(0)),
        out_shape=(shp, shp),
        compiler_params=pltpu.CompilerParams(
            dimension_semantics=("parallel", "parallel", "parallel"),
            vmem_limit_bytes=VMEM_LIMIT),
        name=f"dilated_d{dilation}",
    )(view, view, view, view, view)


SB_TQ = 128
SB_TK = 128
SB_WINDOW = 384
SB_DEAD = 160.0
SB_BLOCKS_PER_ITER = 2
HEADS_PER_STEP = LANES // HEAD_DIM


def _softplus2(z):
    return jnp.maximum(z, 0.0) + jnp.log2(1.0 + jnp.exp2(-jnp.abs(z)))


def _split_hi_lo(sp):
    parts = []
    for c in range(sp.shape[1] // LANES):
        sp_c = sp[:, c * LANES:(c + 1) * LANES]
        hi = sp_c.astype(BF16)
        parts += [hi, (sp_c - hi.astype(F32)).astype(BF16)]
    return jnp.concatenate(parts, axis=1)


def _sb_suffix_sums(hl, tri):
    return [_dot(hl[:, 2 * c * LANES:2 * (c + 1) * LANES], tri)
            for c in range(hl.shape[1] // (2 * LANES))]


def _sb_weights(z, sums, carry):
    las = [None] * len(sums)
    for c in reversed(range(len(sums))):
        las[c] = z[:, c * LANES:(c + 1) * LANES] - sums[c][:, :LANES] - carry
        carry = carry + sums[c][:, LANES:]
    return jnp.exp2(jnp.concatenate(las, axis=1)).astype(BF16), carry


def _sb_kernel(q_ref, kt_ref, v_ref, tri_ref, o_ref, acc_ref, carry_ref):
    s = q_ref.shape[0]
    n_q = s // SB_TQ
    heads = range(HEADS_PER_STEP)
    hsl = lambda h: slice(h * HEAD_DIM, (h + 1) * HEAD_DIM)
    tri = tri_ref[...]

    k2 = jnp.square(kt_ref[...].astype(F32))
    k_max = [jnp.sqrt(jnp.max(jnp.sum(k2[hsl(h)], axis=0, keepdims=True))) for h in heads]

    col_minus_row = (lax.broadcasted_iota(jnp.int32, (SB_TQ, SB_WINDOW), 1)
                     - lax.broadcasted_iota(jnp.int32, (SB_TQ, SB_WINDOW), 0))

    def query_blocks(blocks):
        items = [(slot, h) for slot in range(len(blocks)) for h in heads]
        rows, start, causal, q, z_max = {}, {}, {}, {}, {}
        for slot, qi in enumerate(blocks):
            first = qi * SB_TQ
            lead = SB_WINDOW - SB_TQ
            if isinstance(qi, int):
                start[slot] = max(first - lead, 0)
                rows[slot] = pl.ds(first, SB_TQ)
            else:
                start[slot] = pl.multiple_of(jnp.maximum(first - lead, 0), LANES)
                rows[slot] = pl.ds(pl.multiple_of(first, SB_TQ), SB_TQ)
            causal[slot] = col_minus_row < (first - start[slot] if isinstance(qi, int) else lead)
            for h in heads:
                q[slot, h] = q_ref[rows[slot], hsl(h)]
                z_max[slot, h] = (jnp.sqrt(jnp.sum(jnp.square(q[slot, h].astype(F32)), axis=1,
                                                   keepdims=True)) * (k_max[h] * 1.001) + 1.0)
        z = {(slot, h): jnp.where(causal[slot],
                                  _dot(q[slot, h], kt_ref[hsl(h), pl.ds(start[slot], SB_WINDOW)]),
                                  NEG_INF) for slot, h in items}
        hl = {w: _split_hi_lo(_softplus2(z[w])) for w in items}
        sums = {w: _sb_suffix_sums(hl[w], tri) for w in items}
        a, carry = {}, {}
        for w in items:
            a[w], carry[w] = _sb_weights(z[w], sums[w], jnp.zeros((SB_TQ, LANES), F32))
        for slot, h in items:
            acc_ref[slot, h] = _dot(a[slot, h], v_ref[pl.ds(start[slot], SB_WINDOW), :])
            carry_ref[slot, h] = carry[slot, h]

        def alive(slot, carries):
            slack = [carries[h] - z_max[slot, h] for h in heads]
            return jnp.min(jnp.minimum(slack[0], slack[1])) < SB_DEAD

        lives = [alive(slot, [carry[slot, h] for h in heads]) for slot in range(len(blocks))]
        for slot in range(len(blocks)):
            def cond(state):
                ks, live = state
                return (ks >= 0) & live

            def body(state):
                ks, _ = state
                ks = pl.multiple_of(ks, SB_TK)
                carries = []
                for h in heads:
                    zt = _dot(q[slot, h], kt_ref[hsl(h), pl.ds(ks, SB_TK)])
                    at, ct = _sb_weights(zt, _sb_suffix_sums(_split_hi_lo(_softplus2(zt)), tri),
                                         carry_ref[slot, h])
                    acc_ref[slot, h] += _dot(at, v_ref[pl.ds(ks, SB_TK), :])
                    carry_ref[slot, h] = ct
                    carries.append(ct)
                return ks - SB_TK, alive(slot, carries)

            first_left = jnp.asarray(start[slot] - SB_TK, jnp.int32)
            lax.while_loop(cond, body, (first_left, lives[slot]))
            o_ref[rows[slot], :] = jnp.concatenate(
                [acc_ref[slot, h, :, hsl(h)] for h in heads], axis=1).astype(o_ref.dtype)

    n_lead = (SB_WINDOW - SB_TQ) // SB_TQ
    for qi in range(n_lead):
        query_blocks([qi])
    n_groups = (n_q - n_lead) // SB_BLOCKS_PER_ITER

    def group(i, _):
        query_blocks([n_lead + i * SB_BLOCKS_PER_ITER + j for j in range(SB_BLOCKS_PER_ITER)])
        return 0

    lax.fori_loop(0, n_groups, group, 0)
    for qi in range(n_lead + n_groups * SB_BLOCKS_PER_ITER, n_q):
        query_blocks([qi])


def _sb_attention(qs, kst, vs):
    b, s, _ = qs.shape
    assert s % SB_TQ == 0 and s >= SB_WINDOW and SB_WINDOW % LANES == 0 and HEADS_PER_STEP == 2
    t = (jnp.arange(LANES)[:, None] >= jnp.arange(LANES)[None, :]).astype(BF16)
    half = jnp.concatenate([t, jnp.ones((LANES, LANES), BF16)], axis=1)
    tri = jnp.concatenate([half, half], axis=0)
    nh = HEADS_PER_STEP
    return pl.pallas_call(
        _sb_kernel,
        grid=(b, SB_WIDTH // LANES),
        in_specs=[
            pl.BlockSpec((None, s, LANES), lambda bi, p: (bi, 0, p)),
            pl.BlockSpec((None, LANES, s), lambda bi, p: (bi, p, 0)),
            pl.BlockSpec((None, s, LANES), lambda bi, p: (bi, 0, p)),
            pl.BlockSpec(tri.shape, lambda bi, p: (0, 0)),
        ],
        out_specs=pl.BlockSpec((None, s, LANES), lambda bi, p: (bi, 0, p)),
        out_shape=jax.ShapeDtypeStruct((b, s, SB_WIDTH), BF16),
        scratch_shapes=[pltpu.VMEM((SB_BLOCKS_PER_ITER, nh, SB_TQ, LANES), F32)] * 2,
        compiler_params=pltpu.CompilerParams(
            dimension_semantics=("parallel", "parallel"), vmem_limit_bytes=VMEM_LIMIT),
        name="stick_breaking",
    )(qs, kst, vs, tri)


TAIL_TILES = 2
MLP_FF_CHUNK = 1024


def _token_major(ref, scr):
    d, rows, _ = ref.shape
    for c in range(scr.shape[0]):
        for r in range(d):
            scr[c, pl.ds(r, rows, stride=d), :] = ref[r, :, c * LANES:(c + 1) * LANES]
    return jnp.concatenate([scr[c] for c in range(scr.shape[0])], axis=1)


def _tail_kernel(x_ref, o0, o1, o2, l0, l1, l2, ob_ref, ga_ref, gb_ref,
                 wud, wus, wout, g_mlp, w1, w2, g_out, out_ref, *scratch, final_norm):
    tiles = range(TAIL_TILES)
    rows = [slice(t * DIL_TILE, (t + 1) * DIL_TILE) for t in tiles]
    o_a = []
    for t in tiles:
        scr = scratch[4 * t:4 * t + 4]
        outs = [o0[rows[t], :], _token_major(o1.at[t], scr[0]), _token_major(o2.at[t], scr[1])]
        lse = [l0[rows[t], :], _token_major(l1.at[t], scr[2]), _token_major(l2.at[t], scr[3])]
        m = jnp.maximum(jnp.maximum(lse[0], lse[1]), lse[2])
        e = [jnp.exp(l - m) for l in lse]
        o_a.append((e[0] * outs[0] + e[1] * outs[1] + e[2] * outs[2]) / (e[0] + e[1] + e[2]))
    up_a = [_dot(o_a[t].astype(BF16), wud[...]) for t in tiles]
    up_b = [_dot(ob_ref[rows[t], :], wus[...]) for t in tiles]
    merged = [ga_ref[rows[t], :] * up_a[t] + gb_ref[rows[t], :] * up_b[t] for t in tiles]
    y = [x_ref[rows[t], :] + _dot(merged[t].astype(BF16), wout[...]) for t in tiles]
    h2 = [_rms(y[t], g_mlp[...]).astype(BF16) for t in tiles]
    for c in range(D_FF // MLP_FF_CHUNK):
        cs = slice(c * MLP_FF_CHUNK, (c + 1) * MLP_FF_CHUNK)
        u = [jnp.maximum(_dot(h2[t], w1[:, cs]), 0.0) for t in tiles]
        y = [y[t] + _dot((u[t] * u[t]).astype(BF16), w2[cs, :]) for t in tiles]
    for t in tiles:
        out_ref[rows[t], :] = _rms(y[t], g_out[...]) if final_norm else y[t]


def _tail(x, oas, lses, o_b, g_a, g_b, wud, wus, wout, g_mlp, w1, w2, g_out, final_norm):
    b, s, d = x.shape
    tm = TAIL_TILES * DIL_TILE
    tok = lambda w: pl.BlockSpec((None, tm, w), lambda bi, i: (bi, i, 0))
    full = lambda a: pl.BlockSpec(a.shape, lambda bi, i: (0,) * a.ndim,
                                  pipeline_mode=pl.Buffered(1))

    def dil(a):
        if a.ndim == 3:
            return tok(DIL_OUT_WIDTH)
        return pl.BlockSpec((None, TAIL_TILES) + a.shape[2:], lambda bi, i: (bi, i, 0, 0, 0))

    weights = (wud, wus, wout, g_mlp, w1, w2, g_out)
    return pl.pallas_call(
        functools.partial(_tail_kernel, final_norm=final_norm),
        grid=(b, s // tm),
        in_specs=[tok(d)] + [dil(a) for a in (*oas, *lses)] + [tok(SB_WIDTH), tok(d), tok(d)]
                 + [full(a) for a in weights],
        out_specs=tok(d),
        out_shape=jax.ShapeDtypeStruct((b, s, d), F32),
        scratch_shapes=[pltpu.VMEM((DIL_OUT_WIDTH // LANES, DIL_TILE, LANES), F32)]
                       * (4 * TAIL_TILES),
        compiler_params=pltpu.CompilerParams(
            dimension_semantics=("parallel", "parallel"), vmem_limit_bytes=VMEM_LIMIT),
        name="tail",
    )(x, *oas, *lses, o_b, g_a, g_b, *weights)


def kernel(x, norm_mix_g, w_in, b_gate, w_up_dil, w_up_sb, w_out, norm_mlp_g, w_mlp_in,
           w_mlp_out, norm_final_g):
    depth = w_in.shape[0]
    row = lambda v: v.reshape(1, -1)
    for layer in range(depth):
        w = w_in[layer]
        c0 = 3 * DIL_WIDTH
        c1 = c0 + 3 * SB_WIDTH
        dil_cols = lambda i, grp: w[:, i * DIL_WIDTH + grp * DIL_OUT_WIDTH:
                                    i * DIL_WIDTH + (grp + 1) * DIL_OUT_WIDTH]
        wd = [jnp.concatenate([dil_cols(0, grp) * (1.0 / math.sqrt(HEAD_DIM)), dil_cols(1, grp),
                               dil_cols(2, grp)], axis=1).astype(BF16)
              for grp in range(len(DIL_GROUPS))]
        w = w.astype(BF16)
        wqs, wks, wvs = (w[:, c0 + i * SB_WIDTH:c0 + (i + 1) * SB_WIDTH] for i in range(3))
        wga, wgb = w[:, c1:c1 + D_MODEL], w[:, c1 + D_MODEL:]
        bga, bgb = row(b_gate[layer][:D_MODEL]), row(b_gate[layer][D_MODEL:])
        *qkv, qs, kst, vs, g_a, g_b = _proj(
            x, row(norm_mix_g[layer]), wd, wqs, wks.T, wvs, wga, wgb, bga, bgb)
        dil = [_dilated_group(qkv[grp], grp) for grp in range(len(DIL_GROUPS))]
        o_b = _sb_attention(qs, kst, vs)
        x = _tail(x, [o for o, _ in dil], [l for _, l in dil], o_b, g_a, g_b,
                  w_up_dil[layer].astype(BF16), w_up_sb[layer].astype(BF16),
                  w_out[layer].astype(BF16), row(norm_mlp_g[layer]),
                  w_mlp_in[layer].astype(BF16), w_mlp_out[layer].astype(BF16),
                  row(norm_final_g), final_norm=layer == depth - 1)
    return x
```

```python
import functools
import math

import jax
import jax.numpy as jnp
from jax import lax
from jax.experimental import pallas as pl
from jax.experimental.pallas import tpu as pltpu

D_MODEL = 1024
HEAD_DIM = 64
DIL_GROUPS = ((128, 1), (512, 4), (2048, 16))
DIL_HEADS_PER_GROUP = 4
N_DIL_HEADS = DIL_HEADS_PER_GROUP * len(DIL_GROUPS)
N_SB_HEADS = 8
DIL_WIDTH = N_DIL_HEADS * HEAD_DIM
DIL_OUT_WIDTH = DIL_HEADS_PER_GROUP * HEAD_DIM
SB_WIDTH = N_SB_HEADS * HEAD_DIM
D_FF = 4 * D_MODEL
BLOCK = 128
RMS_EPS = 1e-6
NEG_INF = -1e30
LOG2E = 1.4426950408889634
LANES = 128

VMEM_LIMIT = 56 * 1024 * 1024

BF16 = jnp.bfloat16
F32 = jnp.float32


def _dot(a, b):
    return jnp.dot(a, b, preferred_element_type=F32)


def _dot_nt(a, b):
    return lax.dot_general(a, b, (((1,), (1,)), ((), ())), preferred_element_type=F32)


def _rms(x, g):
    return x * lax.rsqrt(jnp.mean(x * x, axis=-1, keepdims=True) + RMS_EPS) * g


def _sigmoid(x):
    return 1.0 / (1.0 + jnp.exp(-x))


DIL_TILE = 256
PROJ_TILES = 2


def _residue_major(h_ref, d):
    rows = DIL_TILE // d
    return jnp.concatenate(
        [jnp.concatenate([h_ref[c, pl.ds(r, rows, stride=d), :] for r in range(d)], axis=0)
         for c in range(h_ref.shape[0])], axis=1)


def _proj_kernel(x_ref, g_ref, wd0, wd1, wd2, wqs, wks, wvs, wga, wgb, bga, bgb,
                 d0_ref, d1_ref, d2_ref, qs_ref, ks_ref, vs_ref, ga_ref, gb_ref, *h_scr):
    tiles = range(PROJ_TILES)
    rows = [slice(t * DIL_TILE, (t + 1) * DIL_TILE) for t in tiles]
    h = []
    for t in tiles:
        h32 = _rms(x_ref[rows[t], :], g_ref[...])
        for c in range(h_scr[t].shape[0]):
            h_scr[t][c] = h32[:, c * LANES:(c + 1) * LANES]
        h.append(h32.astype(BF16))
    for (_, dilation), w_ref, o_ref in zip(DIL_GROUPS, (wd0, wd1, wd2), (d0_ref, d1_ref, d2_ref)):
        for t in tiles:
            hd = h[t] if dilation == 1 else _residue_major(h_scr[t], dilation).astype(BF16)
            o_ref[rows[t], :] = _dot(hd, w_ref[...]).astype(BF16)
    for t in tiles:
        qs_ref[rows[t], :] = (_dot(h[t], wqs[...]) * (LOG2E / math.sqrt(HEAD_DIM))).astype(BF16)
    for t in tiles:
        ks_ref[rows[t], :] = _dot(h[t], wks[...]).astype(BF16)
    for t in tiles:
        vs_ref[rows[t], :] = _dot(h[t], wvs[...]).astype(BF16)
    for t in tiles:
        ga_ref[rows[t], :] = _sigmoid(_dot(h[t], wga[...]) + bga[...])
    for t in tiles:
        gb_ref[rows[t], :] = _sigmoid(_dot(h[t], wgb[...]) + bgb[...])


def _proj(x, g, wd, wqs, wks, wvs, wga, wgb, bga, bgb):
    b, s, d = x.shape
    tm = PROJ_TILES * DIL_TILE
    tok = lambda w: pl.BlockSpec((None, tm, w), lambda bi, i: (bi, i, 0))
    full = lambda a: pl.BlockSpec(a.shape, lambda bi, i: (0,) * a.ndim,
                                  pipeline_mode=pl.Buffered(1))
    dil_width = 3 * DIL_OUT_WIDTH
    out_shape = (
        (jax.ShapeDtypeStruct((b, s, dil_width), BF16),) * len(DIL_GROUPS)
        + (jax.ShapeDtypeStruct((b, s, SB_WIDTH), BF16),) * 3
        + (jax.ShapeDtypeStruct((b, s, D_MODEL), F32),) * 2)
    out_specs = ((tok(dil_width),) * len(DIL_GROUPS) + (tok(SB_WIDTH),) * 3
                 + (tok(D_MODEL),) * 2)
    weights = (g, *wd, wqs, wks, wvs, wga, wgb, bga, bgb)
    return pl.pallas_call(
        _proj_kernel,
        grid=(b, s // tm),
        in_specs=[tok(d)] + [full(a) for a in weights],
        out_specs=out_specs,
        out_shape=out_shape,
        scratch_shapes=[pltpu.VMEM((d // LANES, DIL_TILE, LANES), F32)] * PROJ_TILES,
        compiler_params=pltpu.CompilerParams(
            dimension_semantics=("parallel", "parallel"), vmem_limit_bytes=VMEM_LIMIT),
        name="proj",
    )(x, *weights)


DIL_QB = 4
DIL_QROWS = DIL_QB * BLOCK


def _alibi_slope(head):
    return 2.0 ** (-8.0 * (head + 1) / N_DIL_HEADS)


def _dil_kernel(q_ref, kp_ref, kc_ref, vp_ref, vc_ref, o_ref, lse_ref, *, group, dilation, n_steps):
    n = pl.program_id(2)
    width = DIL_OUT_WIDTH
    q = q_ref[...].reshape(DIL_QROWS, width)
    k = jnp.concatenate([kp_ref[...].reshape(BLOCK, width), kc_ref[...].reshape(DIL_QROWS, width)], axis=0)
    v = jnp.concatenate([vp_ref[...].reshape(BLOCK, width), vc_ref[...].reshape(DIL_QROWS, width)], axis=0)
    qi = lax.broadcasted_iota(jnp.int32, (BLOCK, 2 * BLOCK), 0)
    kj = lax.broadcasted_iota(jnp.int32, (BLOCK, 2 * BLOCK), 1)
    steps = qi + BLOCK - kj
    valid = (steps >= 0) & (steps <= n_steps)
    has_prev = (kj >= BLOCK) | (n > 0)
    dist = (steps * dilation).astype(F32)
    low = lax.broadcasted_iota(jnp.int32, (1, LANES), 1) < HEAD_DIM
    o_blocks = [[None] * 2 for _ in range(DIL_QB)]
    lse_blocks = [[None] * 2 for _ in range(DIL_QB)]
    sel = (low, jnp.logical_not(low))
    ones = [jnp.broadcast_to(jnp.where(sel[e], 1.0, 0.0).astype(BF16), (2 * BLOCK, LANES))
            for e in range(2)]
    work = [(p, j, e) for p in range(2) for j in range(DIL_QB) for e in range(2)]
    q_rows = lambda j: slice(j * BLOCK, (j + 1) * BLOCK)
    k_rows = lambda j: slice(j * BLOCK, (j + 2) * BLOCK)
    pair = lambda t, p: t[:, p * LANES:(p + 1) * LANES]
    qm = {(p, e): jnp.where(sel[e], pair(q, p), jnp.zeros((DIL_QROWS, LANES), BF16))
          for p in range(2) for e in range(2)}
    vm = {(p, e): jnp.where(sel[e], pair(v, p), jnp.zeros((BLOCK + DIL_QROWS, LANES), BF16))
          for p in range(2) for e in range(2)}
    bias = {(p, e): jnp.where(valid, -_alibi_slope(group * DIL_HEADS_PER_GROUP + 2 * p + e) * dist,
                              NEG_INF) for p in range(2) for e in range(2)}
    logits = {}
    for p, j, e in work:
        lg = _dot_nt(qm[p, e][q_rows(j)], pair(k, p)[k_rows(j)]) + bias[p, e]
        logits[p, j, e] = jnp.where(has_prev, lg, NEG_INF) if j == 0 else lg
    ms = {w: jnp.max(logits[w], axis=-1, keepdims=True) for w in work}
    pr = {w: jnp.exp(logits[w] - ms[w]).astype(BF16) for w in work}
    t = {(p, j, e): _dot(pr[p, j, e], jnp.concatenate([vm[p, e][k_rows(j)], ones[e]], axis=1))
         for p, j, e in work}
    for p in range(2):
        for j in range(DIL_QB):
            acc = t[p, j, 0] + t[p, j, 1]
            num, den = acc[:, :LANES], acc[:, LANES:]
            o_blocks[j][p] = num / den
            lse_blocks[j][p] = jnp.where(low, ms[p, j, 0], ms[p, j, 1]) + jnp.log(den)
    for j in range(DIL_QB):
        rows = o_ref.shape[-2] if len(o_ref.shape) == 3 else BLOCK
        tiles = slice(j * BLOCK // rows, (j + 1) * BLOCK // rows)
        for ref, blocks in ((o_ref, o_blocks), (lse_ref, lse_blocks)):
            val = jnp.concatenate(blocks[j], axis=1)
            if len(ref.shape) == 3:
                ref[tiles] = val.reshape(BLOCK // rows, rows, width)
            else:
                ref[j * BLOCK:(j + 1) * BLOCK, :] = val


def _dilated_group(qkv, group):
    window, dilation = DIL_GROUPS[group]
    b, s, _ = qkv.shape
    width = DIL_OUT_WIDTH
    n_grid = s // dilation // DIL_QROWS
    first_prev = lambda n: jnp.maximum(n * DIL_QB - 1, 0)
    if dilation == 1:
        view = qkv
        cur = lambda c: pl.BlockSpec((None, DIL_QROWS, width), lambda bi, r, n: (bi, n, c))
        prev = lambda c: pl.BlockSpec((None, BLOCK, width), lambda bi, r, n: (bi, first_prev(n), c))
        shp = jax.ShapeDtypeStruct((b, s, width), F32)
    else:
        rows = DIL_TILE // dilation
        view = qkv.reshape(b, s // DIL_TILE, dilation, rows, 3 * width)
        cur = lambda c: pl.BlockSpec((None, DIL_QROWS // rows, None, rows, width),
                                     lambda bi, r, n: (bi, n, r, 0, c))
        prev = lambda c: pl.BlockSpec((None, BLOCK // rows, None, rows, width),
                                      lambda bi, r, n: (bi, first_prev(n), r, 0, c))
        shp = jax.ShapeDtypeStruct((b, s // DIL_TILE, dilation, rows, width), F32)
    return pl.pallas_call(
        functools.partial(_dil_kernel, group=group, dilation=dilation,
                          n_steps=window // dilation),
        grid=(b, dilation, n_grid),
        in_specs=[cur(0), prev(1), cur(1), prev(2), cur(2)],
        out_specs=(cur(0), cur(0)),
        out_shape=(shp, shp),
        compiler_params=pltpu.CompilerParams(
            dimension_semantics=("parallel", "parallel", "parallel"),
            vmem_limit_bytes=VMEM_LIMIT),
        name=f"dilated_d{dilation}",
    )(view, view, view, view, view)


SB_TQ = 64
SB_TK = 128
SB_WINDOW = 256
SB_DEAD = 160.0
SB_BLOCKS_PER_ITER = 5
HEADS_PER_STEP = LANES // HEAD_DIM


def _softplus2(z):
    return jnp.maximum(z, 0.0) + jnp.log2(1.0 + jnp.exp2(-jnp.abs(z)))


def _split_hi_lo(sp):
    parts = []
    for c in range(sp.shape[1] // LANES):
        sp_c = sp[:, c * LANES:(c + 1) * LANES]
        hi = sp_c.astype(BF16)
        parts += [hi, (sp_c - hi.astype(F32)).astype(BF16)]
    return jnp.concatenate(parts, axis=1)


def _sb_suffix_sums(hl, tri):
    return [_dot(hl[:, 2 * c * LANES:2 * (c + 1) * LANES], tri)
            for c in range(hl.shape[1] // (2 * LANES))]


def _sb_weights(z, sums, carry):
    las = [None] * len(sums)
    for c in reversed(range(len(sums))):
        las[c] = z[:, c * LANES:(c + 1) * LANES] - sums[c][:, :LANES] - carry
        carry = carry + sums[c][:, LANES:]
    return jnp.exp2(jnp.concatenate(las, axis=1)).astype(BF16), carry


def _sb_kernel(q_ref, k_ref, v_ref, tri_ref, o_ref, acc_ref, carry_ref):
    s = q_ref.shape[0]
    n_q = s // SB_TQ
    heads = range(HEADS_PER_STEP)
    tri = tri_ref[...]
    lane = lax.broadcasted_iota(jnp.int32, (1, LANES), 1)
    own = [(lane >= h * HEAD_DIM) & (lane < (h + 1) * HEAD_DIM) for h in heads]

    k2 = jnp.square(k_ref[...].astype(F32))
    k_max = [jnp.sqrt(jnp.max(jnp.sum(jnp.where(own[h], k2, 0.0), axis=1, keepdims=True)))
             for h in heads]

    col_minus_row = (lax.broadcasted_iota(jnp.int32, (SB_TQ, SB_WINDOW), 1)
                     - lax.broadcasted_iota(jnp.int32, (SB_TQ, SB_WINDOW), 0))
    lead = SB_WINDOW - SB_TQ

    def query_blocks(blocks):
        items = [(slot, h) for slot in range(len(blocks)) for h in heads]
        rows, start, causal, q, z_max = {}, {}, {}, {}, {}
        for slot, qi in enumerate(blocks):
            first = qi * SB_TQ
            if isinstance(qi, int):
                start[slot] = max(first - lead, 0)
                rows[slot] = pl.ds(first, SB_TQ)
            else:
                start[slot] = pl.multiple_of(jnp.maximum(first - lead, 0), SB_TQ)
                rows[slot] = pl.ds(pl.multiple_of(first, SB_TQ), SB_TQ)
            causal[slot] = col_minus_row < (first - start[slot] if isinstance(qi, int) else lead)
            q_pair = q_ref[rows[slot], :]
            for h in heads:
                q[slot, h] = jnp.where(own[h], q_pair, jnp.zeros_like(q_pair))
                z_max[slot, h] = (jnp.sqrt(jnp.sum(jnp.square(q[slot, h].astype(F32)), axis=1,
                                                   keepdims=True)) * (k_max[h] * 1.001) + 1.0)
        z = {(slot, h): jnp.where(causal[slot],
                                  _dot_nt(q[slot, h], k_ref[pl.ds(start[slot], SB_WINDOW), :]),
                                  NEG_INF) for slot, h in items}
        hl = {w: _split_hi_lo(_softplus2(z[w])) for w in items}
        sums = {w: _sb_suffix_sums(hl[w], tri) for w in items}
        a, carry = {}, {}
        for w in items:
            a[w], carry[w] = _sb_weights(z[w], sums[w], jnp.zeros((SB_TQ, LANES), F32))
        for slot, h in items:
            acc_ref[slot, h] = _dot(a[slot, h], v_ref[pl.ds(start[slot], SB_WINDOW), :])
            carry_ref[slot, h] = carry[slot, h]

        def alive(slot, carries):
            slack = [carries[h] - z_max[slot, h] for h in heads]
            return jnp.min(jnp.minimum(slack[0], slack[1])) < SB_DEAD

        lives = [alive(slot, [carry[slot, h] for h in heads]) for slot in range(len(blocks))]
        for slot in range(len(blocks)):
            def cond(state):
                ke, live = state
                return (ke > 0) & live

            def body(state):
                ke, _ = state
                ks = pl.multiple_of(jnp.maximum(ke - SB_TK, 0), SB_TQ)
                fresh = (lax.broadcasted_iota(jnp.int32, (SB_TQ, SB_TK), 1) + ks) < ke
                carries = []
                for h in heads:
                    zt = jnp.where(fresh, _dot_nt(q[slot, h], k_ref[pl.ds(ks, SB_TK), :]), NEG_INF)
                    at, ct = _sb_weights(zt, _sb_suffix_sums(_split_hi_lo(_softplus2(zt)), tri),
                                         carry_ref[slot, h])
                    acc_ref[slot, h] += _dot(at, v_ref[pl.ds(ks, SB_TK), :])
                    carry_ref[slot, h] = ct
                    carries.append(ct)
                return ks, alive(slot, carries)

            lax.while_loop(cond, body, (jnp.asarray(start[slot], jnp.int32), lives[slot]))
            o_ref[rows[slot], :] = jnp.where(own[0], acc_ref[slot, 0],
                                             acc_ref[slot, 1]).astype(o_ref.dtype)

    n_lead = lead // SB_TQ
    for qi in range(n_lead):
        query_blocks([qi])
    n_groups = (n_q - n_lead) // SB_BLOCKS_PER_ITER

    def group(i, _):
        query_blocks([n_lead + i * SB_BLOCKS_PER_ITER + j for j in range(SB_BLOCKS_PER_ITER)])
        return 0

    lax.fori_loop(0, n_groups, group, 0)
    for qi in range(n_lead + n_groups * SB_BLOCKS_PER_ITER, n_q):
        query_blocks([qi])


def _sb_attention(qs, ks, vs):
    b, s, _ = qs.shape
    assert s % SB_TQ == 0 and s >= SB_WINDOW and SB_WINDOW % LANES == 0 and HEADS_PER_STEP == 2
    t = (jnp.arange(LANES)[:, None] >= jnp.arange(LANES)[None, :]).astype(BF16)
    half = jnp.concatenate([t, jnp.ones((LANES, LANES), BF16)], axis=1)
    tri = jnp.concatenate([half, half], axis=0)
    nh = HEADS_PER_STEP
    seq = pl.BlockSpec((None, s, LANES), lambda bi, p: (bi, 0, p))
    return pl.pallas_call(
        _sb_kernel,
        grid=(b, SB_WIDTH // LANES),
        in_specs=[seq, seq, seq, pl.BlockSpec(tri.shape, lambda bi, p: (0, 0))],
        out_specs=seq,
        out_shape=jax.ShapeDtypeStruct((b, s, SB_WIDTH), BF16),
        scratch_shapes=[pltpu.VMEM((SB_BLOCKS_PER_ITER, nh, SB_TQ, LANES), F32)] * 2,
        compiler_params=pltpu.CompilerParams(
            dimension_semantics=("parallel", "parallel"), vmem_limit_bytes=VMEM_LIMIT),
        name="stick_breaking",
    )(qs, ks, vs, tri)


TAIL_TILES = 2
MLP_FF_CHUNK = 1024


def _token_major(ref, scr):
    d, rows, _ = ref.shape
    for c in range(scr.shape[0]):
        for r in range(d):
            scr[c, pl.ds(r, rows, stride=d), :] = ref[r, :, c * LANES:(c + 1) * LANES]
    return jnp.concatenate([scr[c] for c in range(scr.shape[0])], axis=1)


def _tail_kernel(x_ref, o0, o1, o2, l0, l1, l2, ob_ref, ga_ref, gb_ref,
                 wud, wus, wout, g_mlp, w1, w2, g_out, out_ref, *scratch, final_norm):
    tiles = range(TAIL_TILES)
    rows = [slice(t * DIL_TILE, (t + 1) * DIL_TILE) for t in tiles]
    o_a = []
    for t in tiles:
        scr = scratch[4 * t:4 * t + 4]
        outs = [o0[rows[t], :], _token_major(o1.at[t], scr[0]), _token_major(o2.at[t], scr[1])]
        lse = [l0[rows[t], :], _token_major(l1.at[t], scr[2]), _token_major(l2.at[t], scr[3])]
        m = jnp.maximum(jnp.maximum(lse[0], lse[1]), lse[2])
        e = [jnp.exp(l - m) for l in lse]
        o_a.append((e[0] * outs[0] + e[1] * outs[1] + e[2] * outs[2]) / (e[0] + e[1] + e[2]))
    up_a = [_dot(o_a[t].astype(BF16), wud[...]) for t in tiles]
    up_b = [_dot(ob_ref[rows[t], :], wus[...]) for t in tiles]
    merged = [ga_ref[rows[t], :] * up_a[t] + gb_ref[rows[t], :] * up_b[t] for t in tiles]
    y = [x_ref[rows[t], :] + _dot(merged[t].astype(BF16), wout[...]) for t in tiles]
    h2 = [_rms(y[t], g_mlp[...]).astype(BF16) for t in tiles]
    for c in range(D_FF // MLP_FF_CHUNK):
        cs = slice(c * MLP_FF_CHUNK, (c + 1) * MLP_FF_CHUNK)
        u = [jnp.maximum(_dot(h2[t], w1[:, cs]), 0.0) for t in tiles]
        y = [y[t] + _dot((u[t] * u[t]).astype(BF16), w2[cs, :]) for t in tiles]
    for t in tiles:
        out_ref[rows[t], :] = _rms(y[t], g_out[...]) if final_norm else y[t]


def _tail(x, oas, lses, o_b, g_a, g_b, wud, wus, wout, g_mlp, w1, w2, g_out, final_norm):
    b, s, d = x.shape
    tm = TAIL_TILES * DIL_TILE
    tok = lambda w: pl.BlockSpec((None, tm, w), lambda bi, i: (bi, i, 0))
    full = lambda a: pl.BlockSpec(a.shape, lambda bi, i: (0,) * a.ndim,
                                  pipeline_mode=pl.Buffered(1))

    def dil(a):
        if a.ndim == 3:
            return tok(DIL_OUT_WIDTH)
        return pl.BlockSpec((None, TAIL_TILES) + a.shape[2:], lambda bi, i: (bi, i, 0, 0, 0))

    weights = (wud, wus, wout, g_mlp, w1, w2, g_out)
    return pl.pallas_call(
        functools.partial(_tail_kernel, final_norm=final_norm),
        grid=(b, s // tm),
        in_specs=[tok(d)] + [dil(a) for a in (*oas, *lses)] + [tok(SB_WIDTH), tok(d), tok(d)]
                 + [full(a) for a in weights],
        out_specs=tok(d),
        out_shape=jax.ShapeDtypeStruct((b, s, d), F32),
        scratch_shapes=[pltpu.VMEM((DIL_OUT_WIDTH // LANES, DIL_TILE, LANES), F32)]
                       * (4 * TAIL_TILES),
        compiler_params=pltpu.CompilerParams(
            dimension_semantics=("parallel", "parallel"), vmem_limit_bytes=VMEM_LIMIT),
        name="tail",
    )(x, *oas, *lses, o_b, g_a, g_b, *weights)


def kernel(x, norm_mix_g, w_in, b_gate, w_up_dil, w_up_sb, w_out, norm_mlp_g, w_mlp_in,
           w_mlp_out, norm_final_g):
    depth = w_in.shape[0]
    row = lambda v: v.reshape(1, -1)
    for layer in range(depth):
        w = w_in[layer]
        c0 = 3 * DIL_WIDTH
        c1 = c0 + 3 * SB_WIDTH
        dil_cols = lambda i, grp: w[:, i * DIL_WIDTH + grp * DIL_OUT_WIDTH:
                                    i * DIL_WIDTH + (grp + 1) * DIL_OUT_WIDTH]
        wd = [jnp.concatenate([dil_cols(0, grp) * (1.0 / math.sqrt(HEAD_DIM)), dil_cols(1, grp),
                               dil_cols(2, grp)], axis=1).astype(BF16)
              for grp in range(len(DIL_GROUPS))]
        w = w.astype(BF16)
        wqs, wks, wvs = (w[:, c0 + i * SB_WIDTH:c0 + (i + 1) * SB_WIDTH] for i in range(3))
        wga, wgb = w[:, c1:c1 + D_MODEL], w[:, c1 + D_MODEL:]
        bga, bgb = row(b_gate[layer][:D_MODEL]), row(b_gate[layer][D_MODEL:])
        *qkv, qs, ks, vs, g_a, g_b = _proj(
            x, row(norm_mix_g[layer]), wd, wqs, wks, wvs, wga, wgb, bga, bgb)
        dil = [_dilated_group(qkv[grp], grp) for grp in range(len(DIL_GROUPS))]
        o_b = _sb_attention(qs, ks, vs)
        x = _tail(x, [o for o, _ in dil], [l for _, l in dil], o_b, g_a, g_b,
                  w_up_dil[layer].astype(BF16), w_up_sb[layer].astype(BF16),
                  w_out[layer].astype(BF16), row(norm_mlp_g[layer]),
                  w_mlp_in[layer].astype(BF16), w_mlp_out[layer].astype(BF16),
                  row(norm_final_g), final_norm=layer == depth - 1)
    return x
```

```python
import functools
import math

import jax
import jax.numpy as jnp
from jax import lax
from jax.experimental import pallas as pl
from jax.experimental.pallas import tpu as pltpu

D_MODEL = 1024
HEAD_DIM = 64
DIL_GROUPS = ((128, 1), (512, 4), (2048, 16))
DIL_HEADS_PER_GROUP = 4
N_DIL_HEADS = DIL_HEADS_PER_GROUP * len(DIL_GROUPS)
N_SB_HEADS = 8
DIL_WIDTH = N_DIL_HEADS * HEAD_DIM
DIL_OUT_WIDTH = DIL_HEADS_PER_GROUP * HEAD_DIM
SB_WIDTH = N_SB_HEADS * HEAD_DIM
D_FF = 4 * D_MODEL
BLOCK = 128
RMS_EPS = 1e-6
NEG_INF = -1e30
LOG2E = 1.4426950408889634
LANES = 128

VMEM_LIMIT = 56 * 1024 * 1024

BF16 = jnp.bfloat16
F32 = jnp.float32


def _dot(a, b):
    return jnp.dot(a, b, preferred_element_type=F32)


def _dot_nt(a, b):
    return lax.dot_general(a, b, (((1,), (1,)), ((), ())), preferred_element_type=F32)


def _rms(x, g):
    return x * lax.rsqrt(jnp.mean(x * x, axis=-1, keepdims=True) + RMS_EPS) * g


def _sigmoid(x):
    return 1.0 / (1.0 + jnp.exp(-x))


DIL_TILE = 256
PROJ_TILES = 2


def _residue_major(h_ref, d):
    rows = DIL_TILE // d
    return jnp.concatenate(
        [jnp.concatenate([h_ref[c, pl.ds(r, rows, stride=d), :] for r in range(d)], axis=0)
         for c in range(h_ref.shape[0])], axis=1)


def _proj_kernel(x_ref, g_ref, wd0, wd1, wd2, wqs, wks, wvs, wga, wgb, bga, bgb, head_sel,
                 d0_ref, d1_ref, d2_ref, qs_ref, ks_ref, vs_ref, ga_ref, gb_ref, kn_ref, *h_scr):
    tiles = range(PROJ_TILES)
    rows = [slice(t * DIL_TILE, (t + 1) * DIL_TILE) for t in tiles]
    h = []
    for t in tiles:
        h32 = _rms(x_ref[rows[t], :], g_ref[...])
        for c in range(h_scr[t].shape[0]):
            h_scr[t][c] = h32[:, c * LANES:(c + 1) * LANES]
        h.append(h32.astype(BF16))
    for (_, dilation), w_ref, o_ref in zip(DIL_GROUPS, (wd0, wd1, wd2), (d0_ref, d1_ref, d2_ref)):
        for t in tiles:
            hd = h[t] if dilation == 1 else _residue_major(h_scr[t], dilation).astype(BF16)
            o_ref[rows[t], :] = _dot(hd, w_ref[...]).astype(BF16)
    for t in tiles:
        qs_ref[rows[t], :] = (_dot(h[t], wqs[...]) * (LOG2E / math.sqrt(HEAD_DIM))).astype(BF16)
    k_norm2 = None
    for t in tiles:
        k = _dot(h[t], wks[...]).astype(BF16)
        ks_ref[rows[t], :] = k
        n2 = jnp.max(_dot(jnp.square(k.astype(F32)).astype(BF16), head_sel[...]), axis=0,
                     keepdims=True)
        k_norm2 = n2 if k_norm2 is None else jnp.maximum(k_norm2, n2)
    kn_ref[...] = jnp.broadcast_to(k_norm2, kn_ref.shape)
    for t in tiles:
        vs_ref[rows[t], :] = _dot(h[t], wvs[...]).astype(BF16)
    for t in tiles:
        ga_ref[rows[t], :] = _sigmoid(_dot(h[t], wga[...]) + bga[...])
    for t in tiles:
        gb_ref[rows[t], :] = _sigmoid(_dot(h[t], wgb[...]) + bgb[...])


def _proj(x, g, wd, wqs, wks, wvs, wga, wgb, bga, bgb):
    b, s, d = x.shape
    tm = PROJ_TILES * DIL_TILE
    tok = lambda w: pl.BlockSpec((None, tm, w), lambda bi, i: (bi, i, 0))
    full = lambda a: pl.BlockSpec(a.shape, lambda bi, i: (0,) * a.ndim,
                                  pipeline_mode=pl.Buffered(1))
    dil_width = 3 * DIL_OUT_WIDTH
    out_shape = (
        (jax.ShapeDtypeStruct((b, s, dil_width), BF16),) * len(DIL_GROUPS)
        + (jax.ShapeDtypeStruct((b, s, SB_WIDTH), BF16),) * 3
        + (jax.ShapeDtypeStruct((b, s, D_MODEL), F32),) * 2
        + (jax.ShapeDtypeStruct((b, s // tm, 8, LANES), F32),))
    out_specs = ((tok(dil_width),) * len(DIL_GROUPS) + (tok(SB_WIDTH),) * 3
                 + (tok(D_MODEL),) * 2
                 + (pl.BlockSpec((None, None, 8, LANES), lambda bi, i: (bi, i, 0, 0)),))
    head_sel = (jnp.arange(SB_WIDTH)[:, None] // HEAD_DIM == jnp.arange(LANES)[None, :]).astype(BF16)
    weights = (g, *wd, wqs, wks, wvs, wga, wgb, bga, bgb, head_sel)
    return pl.pallas_call(
        _proj_kernel,
        grid=(b, s // tm),
        in_specs=[tok(d)] + [full(a) for a in weights],
        out_specs=out_specs,
        out_shape=out_shape,
        scratch_shapes=[pltpu.VMEM((d // LANES, DIL_TILE, LANES), F32)] * PROJ_TILES,
        compiler_params=pltpu.CompilerParams(
            dimension_semantics=("parallel", "parallel"), vmem_limit_bytes=VMEM_LIMIT),
        name="proj",
    )(x, *weights)


DIL_MAX_QB = 8


def _alibi_slope(head):
    return 2.0 ** (-8.0 * (head + 1) / N_DIL_HEADS)


def _dil_kernel(q_ref, kp_ref, kc_ref, vp_ref, vc_ref, o_ref, lse_ref, *, group, dilation, n_steps,
                qb):
    n = pl.program_id(2)
    width = DIL_OUT_WIDTH
    qrows = qb * BLOCK
    q = q_ref[...].reshape(qrows, width)
    k = jnp.concatenate([kp_ref[...].reshape(BLOCK, width), kc_ref[...].reshape(qrows, width)], axis=0)
    v = jnp.concatenate([vp_ref[...].reshape(BLOCK, width), vc_ref[...].reshape(qrows, width)], axis=0)
    qi = lax.broadcasted_iota(jnp.int32, (BLOCK, 2 * BLOCK), 0)
    kj = lax.broadcasted_iota(jnp.int32, (BLOCK, 2 * BLOCK), 1)
    steps = qi + BLOCK - kj
    valid = (steps >= 0) & (steps <= n_steps)
    has_prev = (kj >= BLOCK) | (n > 0)
    dist = (steps * dilation).astype(F32)
    low = lax.broadcasted_iota(jnp.int32, (1, LANES), 1) < HEAD_DIM
    o_blocks = [[None] * 2 for _ in range(qb)]
    lse_blocks = [[None] * 2 for _ in range(qb)]
    sel = (low, jnp.logical_not(low))
    ones = [jnp.broadcast_to(jnp.where(sel[e], 1.0, 0.0).astype(BF16), (2 * BLOCK, LANES))
            for e in range(2)]
    work = [(p, j, e) for p in range(2) for j in range(qb) for e in range(2)]
    q_rows = lambda j: slice(j * BLOCK, (j + 1) * BLOCK)
    k_rows = lambda j: slice(j * BLOCK, (j + 2) * BLOCK)
    pair = lambda t, p: t[:, p * LANES:(p + 1) * LANES]
    qm = {(p, e): jnp.where(sel[e], pair(q, p), jnp.zeros((qrows, LANES), BF16))
          for p in range(2) for e in range(2)}
    vm = {(p, e): jnp.where(sel[e], pair(v, p), jnp.zeros((BLOCK + qrows, LANES), BF16))
          for p in range(2) for e in range(2)}
    bias = {(p, e): jnp.where(valid, -_alibi_slope(group * DIL_HEADS_PER_GROUP + 2 * p + e) * dist,
                              NEG_INF) for p in range(2) for e in range(2)}
    logits = {}
    for p, j, e in work:
        lg = _dot_nt(qm[p, e][q_rows(j)], pair(k, p)[k_rows(j)]) + bias[p, e]
        logits[p, j, e] = jnp.where(has_prev, lg, NEG_INF) if j == 0 else lg
    ms = {w: jnp.max(logits[w], axis=-1, keepdims=True) for w in work}
    pr = {w: jnp.exp(logits[w] - ms[w]).astype(BF16) for w in work}
    t = {(p, j, e): _dot(pr[p, j, e], jnp.concatenate([vm[p, e][k_rows(j)], ones[e]], axis=1))
         for p, j, e in work}
    for p in range(2):
        for j in range(qb):
            acc = t[p, j, 0] + t[p, j, 1]
            num, den = acc[:, :LANES], acc[:, LANES:]
            o_blocks[j][p] = num / den
            lse_blocks[j][p] = jnp.where(low, ms[p, j, 0], ms[p, j, 1]) + jnp.log(den)
    for j in range(qb):
        rows = o_ref.shape[-2] if len(o_ref.shape) == 3 else BLOCK
        tiles = slice(j * BLOCK // rows, (j + 1) * BLOCK // rows)
        for ref, blocks in ((o_ref, o_blocks), (lse_ref, lse_blocks)):
            val = jnp.concatenate(blocks[j], axis=1)
            if len(ref.shape) == 3:
                ref[tiles] = val.reshape(BLOCK // rows, rows, width)
            else:
                ref[j * BLOCK:(j + 1) * BLOCK, :] = val


def _dilated_group(qkv, group):
    window, dilation = DIL_GROUPS[group]
    b, s, _ = qkv.shape
    width = DIL_OUT_WIDTH
    qb = min(DIL_MAX_QB, s // dilation // BLOCK)
    qrows = qb * BLOCK
    n_grid = s // dilation // qrows
    first_prev = lambda n: jnp.maximum(n * qb - 1, 0)
    if dilation == 1:
        view = qkv
        cur = lambda c: pl.BlockSpec((None, qrows, width), lambda bi, r, n: (bi, n, c))
        prev = lambda c: pl.BlockSpec((None, BLOCK, width), lambda bi, r, n: (bi, first_prev(n), c))
        shp = jax.ShapeDtypeStruct((b, s, width), F32)
    else:
        rows = DIL_TILE // dilation
        view = qkv.reshape(b, s // DIL_TILE, dilation, rows, 3 * width)
        cur = lambda c: pl.BlockSpec((None, qrows // rows, None, rows, width),
                                     lambda bi, r, n: (bi, n, r, 0, c))
        prev = lambda c: pl.BlockSpec((None, BLOCK // rows, None, rows, width),
                                      lambda bi, r, n: (bi, first_prev(n), r, 0, c))
        shp = jax.ShapeDtypeStruct((b, s // DIL_TILE, dilation, rows, width), F32)
    return pl.pallas_call(
        functools.partial(_dil_kernel, group=group, dilation=dilation,
                          n_steps=window // dilation, qb=qb),
        grid=(b, dilation, n_grid),
        in_specs=[cur(0), prev(1), cur(1), prev(2), cur(2)],
        out_specs=(cur(0), cur(0)),
        out_shape=(shp, shp),
        compiler_params=pltpu.CompilerParams(
            dimension_semantics=("parallel", "parallel", "parallel"),
            vmem_limit_bytes=VMEM_LIMIT),
        name=f"dilated_d{dilation}",
    )(view, view, view, view, view)


SB_TQ = 64
SB_TK = 128
SB_WINDOW = 256
SB_DEAD = 160.0
SB_BLOCKS_PER_ITER = 10
HEADS_PER_STEP = LANES // HEAD_DIM


def _softplus2(z):
    return jnp.maximum(z, 0.0) + jnp.log2(1.0 + jnp.exp2(-jnp.abs(z)))


def _split_hi_lo(sp):
    parts = []
    for c in range(sp.shape[1] // LANES):
        sp_c = sp[:, c * LANES:(c + 1) * LANES]
        hi = sp_c.astype(BF16)
        parts += [hi, (sp_c - hi.astype(F32)).astype(BF16)]
    return jnp.concatenate(parts, axis=1)


def _sb_suffix_sums(hl, tri):
    return [_dot(hl[:, 2 * c * LANES:2 * (c + 1) * LANES], tri)
            for c in range(hl.shape[1] // (2 * LANES))]


def _sb_weights(z, sums, carry):
    las = [None] * len(sums)
    for c in reversed(range(len(sums))):
        las[c] = z[:, c * LANES:(c + 1) * LANES] - sums[c][:, :LANES] - carry
        carry = carry + sums[c][:, LANES:]
    return jnp.exp2(jnp.concatenate(las, axis=1)).astype(BF16), carry


def _sb_kernel(q_ref, k_ref, v_ref, tri_ref, kn_ref, o_ref, acc_ref, carry_ref):
    s = q_ref.shape[0]
    n_q = s // SB_TQ
    heads = range(HEADS_PER_STEP)
    tri = tri_ref[...]
    lane = lax.broadcasted_iota(jnp.int32, (1, LANES), 1)
    own = [(lane >= h * HEAD_DIM) & (lane < (h + 1) * HEAD_DIM) for h in heads]

    kn = jnp.max(kn_ref[...], axis=(0, 1), keepdims=True).reshape(1, LANES)
    k_max = [jnp.sqrt(jnp.max(jnp.where(lane == HEADS_PER_STEP * pl.program_id(1) + h, kn, 0.0)))
             * (1.0 + 2.0 ** -8) for h in heads]

    col_minus_row = (lax.broadcasted_iota(jnp.int32, (SB_TQ, SB_WINDOW), 1)
                     - lax.broadcasted_iota(jnp.int32, (SB_TQ, SB_WINDOW), 0))
    lead = SB_WINDOW - SB_TQ

    def query_blocks(blocks):
        items = [(slot, h) for slot in range(len(blocks)) for h in heads]
        rows, start, causal, q, z_max = {}, {}, {}, {}, {}
        for slot, qi in enumerate(blocks):
            first = qi * SB_TQ
            if isinstance(qi, int):
                start[slot] = max(first - lead, 0)
                rows[slot] = pl.ds(first, SB_TQ)
            else:
                start[slot] = pl.multiple_of(jnp.maximum(first - lead, 0), SB_TQ)
                rows[slot] = pl.ds(pl.multiple_of(first, SB_TQ), SB_TQ)
            causal[slot] = col_minus_row < (first - start[slot] if isinstance(qi, int) else lead)
            q_pair = q_ref[rows[slot], :]
            for h in heads:
                q[slot, h] = jnp.where(own[h], q_pair, jnp.zeros_like(q_pair))
                z_max[slot, h] = (jnp.sqrt(jnp.sum(jnp.square(q[slot, h].astype(F32)), axis=1,
                                                   keepdims=True)) * (k_max[h] * 1.001) + 1.0)
        z = {(slot, h): jnp.where(causal[slot],
                                  _dot_nt(q[slot, h], k_ref[pl.ds(start[slot], SB_WINDOW), :]),
                                  NEG_INF) for slot, h in items}
        hl = {w: _split_hi_lo(_softplus2(z[w])) for w in items}
        sums = {w: _sb_suffix_sums(hl[w], tri) for w in items}
        a, carry = {}, {}
        for w in items:
            a[w], carry[w] = _sb_weights(z[w], sums[w], jnp.zeros((SB_TQ, LANES), F32))
        for slot, h in items:
            acc_ref[slot, h] = _dot(a[slot, h], v_ref[pl.ds(start[slot], SB_WINDOW), :])
            carry_ref[slot, h] = carry[slot, h]

        def alive(slot, carries):
            slack = [carries[h] - z_max[slot, h] for h in heads]
            return jnp.min(jnp.minimum(slack[0], slack[1])) < SB_DEAD

        lives = [alive(slot, [carry[slot, h] for h in heads]) for slot in range(len(blocks))]
        for slot in range(len(blocks)):
            def cond(state):
                ke, live = state
                return (ke > 0) & live

            def body(state):
                ke, _ = state
                ks = pl.multiple_of(jnp.maximum(ke - SB_TK, 0), SB_TQ)
                fresh = (lax.broadcasted_iota(jnp.int32, (SB_TQ, SB_TK), 1) + ks) < ke
                carries = []
                for h in heads:
                    zt = jnp.where(fresh, _dot_nt(q[slot, h], k_ref[pl.ds(ks, SB_TK), :]), NEG_INF)
                    at, ct = _sb_weights(zt, _sb_suffix_sums(_split_hi_lo(_softplus2(zt)), tri),
                                         carry_ref[slot, h])
                    acc_ref[slot, h] += _dot(at, v_ref[pl.ds(ks, SB_TK), :])
                    carry_ref[slot, h] = ct
                    carries.append(ct)
                return ks, alive(slot, carries)

            lax.while_loop(cond, body, (jnp.asarray(start[slot], jnp.int32), lives[slot]))
            o_ref[rows[slot], :] = jnp.where(own[0], acc_ref[slot, 0],
                                             acc_ref[slot, 1]).astype(o_ref.dtype)

    n_lead = lead // SB_TQ
    query_blocks(list(range(n_lead)))
    n_groups = (n_q - n_lead) // SB_BLOCKS_PER_ITER

    def group(i, _):
        query_blocks([n_lead + i * SB_BLOCKS_PER_ITER + j for j in range(SB_BLOCKS_PER_ITER)])
        return 0

    lax.fori_loop(0, n_groups, group, 0)
    rest = list(range(n_lead + n_groups * SB_BLOCKS_PER_ITER, n_q))
    if rest:
        query_blocks(rest)


def _sb_attention(qs, ks, vs, k_norm2):
    b, s, _ = qs.shape
    assert s % SB_TQ == 0 and s >= SB_WINDOW and SB_WINDOW % LANES == 0 and HEADS_PER_STEP == 2
    t = (jnp.arange(LANES)[:, None] >= jnp.arange(LANES)[None, :]).astype(BF16)
    half = jnp.concatenate([t, jnp.ones((LANES, LANES), BF16)], axis=1)
    tri = jnp.concatenate([half, half], axis=0)
    nh = HEADS_PER_STEP
    seq = pl.BlockSpec((None, s, LANES), lambda bi, p: (bi, 0, p))
    return pl.pallas_call(
        _sb_kernel,
        grid=(b, SB_WIDTH // LANES),
        in_specs=[seq, seq, seq, pl.BlockSpec(tri.shape, lambda bi, p: (0, 0)),
                  pl.BlockSpec((None,) + k_norm2.shape[1:], lambda bi, p: (bi, 0, 0, 0))],
        out_specs=seq,
        out_shape=jax.ShapeDtypeStruct((b, s, SB_WIDTH), BF16),
        scratch_shapes=[pltpu.VMEM((SB_BLOCKS_PER_ITER, nh, SB_TQ, LANES), F32)] * 2,
        compiler_params=pltpu.CompilerParams(
            dimension_semantics=("parallel", "parallel"), vmem_limit_bytes=VMEM_LIMIT),
        name="stick_breaking",
    )(qs, ks, vs, tri, k_norm2)


TAIL_TILES = 2
MLP_FF_CHUNK = 1024


def _token_major(ref, scr):
    d, rows, _ = ref.shape
    for c in range(scr.shape[0]):
        for r in range(d):
            scr[c, pl.ds(r, rows, stride=d), :] = ref[r, :, c * LANES:(c + 1) * LANES]
    return jnp.concatenate([scr[c] for c in range(scr.shape[0])], axis=1)


def _tail_kernel(x_ref, o0, o1, o2, l0, l1, l2, ob_ref, ga_ref, gb_ref,
                 wud, wus, wout, g_mlp, w1, w2, g_out, out_ref, *scratch, final_norm):
    tiles = range(TAIL_TILES)
    rows = [slice(t * DIL_TILE, (t + 1) * DIL_TILE) for t in tiles]
    o_a = []
    for t in tiles:
        scr = scratch[4 * t:4 * t + 4]
        outs = [o0[rows[t], :], _token_major(o1.at[t], scr[0]), _token_major(o2.at[t], scr[1])]
        lse = [l0[rows[t], :], _token_major(l1.at[t], scr[2]), _token_major(l2.at[t], scr[3])]
        m = jnp.maximum(jnp.maximum(lse[0], lse[1]), lse[2])
        e = [jnp.exp(l - m) for l in lse]
        o_a.append((e[0] * outs[0] + e[1] * outs[1] + e[2] * outs[2]) / (e[0] + e[1] + e[2]))
    up_a = [_dot(o_a[t].astype(BF16), wud[...]) for t in tiles]
    up_b = [_dot(ob_ref[rows[t], :], wus[...]) for t in tiles]
    merged = [ga_ref[rows[t], :] * up_a[t] + gb_ref[rows[t], :] * up_b[t] for t in tiles]
    y = [x_ref[rows[t], :] + _dot(merged[t].astype(BF16), wout[...]) for t in tiles]
    h2 = [_rms(y[t], g_mlp[...]).astype(BF16) for t in tiles]
    for c in range(D_FF // MLP_FF_CHUNK):
        cs = slice(c * MLP_FF_CHUNK, (c + 1) * MLP_FF_CHUNK)
        u = [jnp.maximum(_dot(h2[t], w1[:, cs]), 0.0) for t in tiles]
        y = [y[t] + _dot((u[t] * u[t]).astype(BF16), w2[cs, :]) for t in tiles]
    for t in tiles:
        out_ref[rows[t], :] = _rms(y[t], g_out[...]) if final_norm else y[t]


def _tail(x, oas, lses, o_b, g_a, g_b, wud, wus, wout, g_mlp, w1, w2, g_out, final_norm):
    b, s, d = x.shape
    tm = TAIL_TILES * DIL_TILE
    tok = lambda w: pl.BlockSpec((None, tm, w), lambda bi, i: (bi, i, 0))
    full = lambda a: pl.BlockSpec(a.shape, lambda bi, i: (0,) * a.ndim,
                                  pipeline_mode=pl.Buffered(1))

    def dil(a):
        if a.ndim == 3:
            return tok(DIL_OUT_WIDTH)
        return pl.BlockSpec((None, TAIL_TILES) + a.shape[2:], lambda bi, i: (bi, i, 0, 0, 0))

    weights = (wud, wus, wout, g_mlp, w1, w2, g_out)
    return pl.pallas_call(
        functools.partial(_tail_kernel, final_norm=final_norm),
        grid=(b, s // tm),
        in_specs=[tok(d)] + [dil(a) for a in (*oas, *lses)] + [tok(SB_WIDTH), tok(d), tok(d)]
                 + [full(a) for a in weights],
        out_specs=tok(d),
        out_shape=jax.ShapeDtypeStruct((b, s, d), F32),
        scratch_shapes=[pltpu.VMEM((DIL_OUT_WIDTH // LANES, DIL_TILE, LANES), F32)]
                       * (4 * TAIL_TILES),
        compiler_params=pltpu.CompilerParams(
            dimension_semantics=("parallel", "parallel"), vmem_limit_bytes=VMEM_LIMIT),
        name="tail",
    )(x, *oas, *lses, o_b, g_a, g_b, *weights)


def kernel(x, norm_mix_g, w_in, b_gate, w_up_dil, w_up_sb, w_out, norm_mlp_g, w_mlp_in,
           w_mlp_out, norm_final_g):
    depth = w_in.shape[0]
    row = lambda v: v.reshape(1, -1)
    for layer in range(depth):
        w = w_in[layer]
        c0 = 3 * DIL_WIDTH
        c1 = c0 + 3 * SB_WIDTH
        dil_cols = lambda i, grp: w[:, i * DIL_WIDTH + grp * DIL_OUT_WIDTH:
                                    i * DIL_WIDTH + (grp + 1) * DIL_OUT_WIDTH]
        wd = [jnp.concatenate([dil_cols(0, grp) * (1.0 / math.sqrt(HEAD_DIM)), dil_cols(1, grp),
                               dil_cols(2, grp)], axis=1).astype(BF16)
              for grp in range(len(DIL_GROUPS))]
        w = w.astype(BF16)
        wqs, wks, wvs = (w[:, c0 + i * SB_WIDTH:c0 + (i + 1) * SB_WIDTH] for i in range(3))
        wga, wgb = w[:, c1:c1 + D_MODEL], w[:, c1 + D_MODEL:]
        bga, bgb = row(b_gate[layer][:D_MODEL]), row(b_gate[layer][D_MODEL:])
        *qkv, qs, ks, vs, g_a, g_b, k_norm2 = _proj(
            x, row(norm_mix_g[layer]), wd, wqs, wks, wvs, wga, wgb, bga, bgb)
        dil = [_dilated_group(qkv[grp], grp) for grp in range(len(DIL_GROUPS))]
        o_b = _sb_attention(qs, ks, vs, k_norm2)
        x = _tail(x, [o for o, _ in dil], [l for _, l in dil], o_b, g_a, g_b,
                  w_up_dil[layer].astype(BF16), w_up_sb[layer].astype(BF16),
                  w_out[layer].astype(BF16), row(norm_mlp_g[layer]),
                  w_mlp_in[layer].astype(BF16), w_mlp_out[layer].astype(BF16),
                  row(norm_final_g), final_norm=layer == depth - 1)
    return x
```

```python
import functools
import math

import jax
import jax.numpy as jnp
from jax import lax
from jax.experimental import pallas as pl
from jax.experimental.pallas import tpu as pltpu

D_MODEL = 1024
HEAD_DIM = 64
DIL_GROUPS = ((128, 1), (512, 4), (2048, 16))
DIL_HEADS_PER_GROUP = 4
N_DIL_HEADS = DIL_HEADS_PER_GROUP * len(DIL_GROUPS)
N_SB_HEADS = 8
DIL_WIDTH = N_DIL_HEADS * HEAD_DIM
DIL_OUT_WIDTH = DIL_HEADS_PER_GROUP * HEAD_DIM
SB_WIDTH = N_SB_HEADS * HEAD_DIM
D_FF = 4 * D_MODEL
BLOCK = 128
RMS_EPS = 1e-6
NEG_INF = -1e30
LOG2E = 1.4426950408889634
LANES = 128

VMEM_LIMIT = 56 * 1024 * 1024

BF16 = jnp.bfloat16
F32 = jnp.float32


def _dot(a, b):
    return jnp.dot(a, b, preferred_element_type=F32)


def _dot_nt(a, b):
    return lax.dot_general(a, b, (((1,), (1,)), ((), ())), preferred_element_type=F32)


def _rms(x, g):
    return x * lax.rsqrt(jnp.mean(x * x, axis=-1, keepdims=True) + RMS_EPS) * g


def _sigmoid(x):
    return 1.0 / (1.0 + jnp.exp(-x))


DIL_TILE = 256
PROJ_TILES = 2


def _residue_major(h_ref, d):
    rows = DIL_TILE // d
    return jnp.concatenate(
        [jnp.concatenate([h_ref[c, pl.ds(r, rows, stride=d), :] for r in range(d)], axis=0)
         for c in range(h_ref.shape[0])], axis=1)


def _proj_kernel(x_ref, g_ref, wd0, wd1, wd2, wqs, wks, wvs, wga, wgb, bga, bgb, head_sel,
                 d0_ref, d1_ref, d2_ref, qs_ref, ks_ref, vs_ref, ga_ref, gb_ref, kn_ref, *h_scr):
    tiles = range(PROJ_TILES)
    rows = [slice(t * DIL_TILE, (t + 1) * DIL_TILE) for t in tiles]
    h = []
    for t in tiles:
        h32 = _rms(x_ref[rows[t], :], g_ref[...])
        for c in range(h_scr[t].shape[0]):
            h_scr[t][c] = h32[:, c * LANES:(c + 1) * LANES]
        h.append(h32.astype(BF16))
    for (_, dilation), w_ref, o_ref in zip(DIL_GROUPS, (wd0, wd1, wd2), (d0_ref, d1_ref, d2_ref)):
        for t in tiles:
            hd = h[t] if dilation == 1 else _residue_major(h_scr[t], dilation).astype(BF16)
            o_ref[rows[t], :] = _dot(hd, w_ref[...]).astype(BF16)
    for t in tiles:
        qs_ref[rows[t], :] = (_dot(h[t], wqs[...]) * (LOG2E / math.sqrt(HEAD_DIM))).astype(BF16)
    k_norm2 = None
    for t in tiles:
        k = _dot(h[t], wks[...]).astype(BF16)
        ks_ref[rows[t], :] = k
        n2 = jnp.max(_dot(jnp.square(k.astype(F32)).astype(BF16), head_sel[...]), axis=0,
                     keepdims=True)
        k_norm2 = n2 if k_norm2 is None else jnp.maximum(k_norm2, n2)
    kn_ref[...] = jnp.broadcast_to(k_norm2, kn_ref.shape)
    for t in tiles:
        vs_ref[rows[t], :] = _dot(h[t], wvs[...]).astype(BF16)
    for t in tiles:
        ga_ref[rows[t], :] = _sigmoid(_dot(h[t], wga[...]) + bga[...])
    for t in tiles:
        gb_ref[rows[t], :] = _sigmoid(_dot(h[t], wgb[...]) + bgb[...])


def _proj(x, g, wd, wqs, wks, wvs, wga, wgb, bga, bgb):
    b, s, d = x.shape
    tm = PROJ_TILES * DIL_TILE
    tok = lambda w: pl.BlockSpec((None, tm, w), lambda bi, i: (bi, i, 0))
    full = lambda a: pl.BlockSpec(a.shape, lambda bi, i: (0,) * a.ndim,
                                  pipeline_mode=pl.Buffered(1))
    dil_width = 3 * DIL_OUT_WIDTH
    out_shape = (
        (jax.ShapeDtypeStruct((b, s, dil_width), BF16),) * len(DIL_GROUPS)
        + (jax.ShapeDtypeStruct((b, s, SB_WIDTH), BF16),) * 3
        + (jax.ShapeDtypeStruct((b, s, D_MODEL), F32),) * 2
        + (jax.ShapeDtypeStruct((b, s // tm, 8, LANES), F32),))
    out_specs = ((tok(dil_width),) * len(DIL_GROUPS) + (tok(SB_WIDTH),) * 3
                 + (tok(D_MODEL),) * 2
                 + (pl.BlockSpec((None, None, 8, LANES), lambda bi, i: (bi, i, 0, 0)),))
    head_sel = (jnp.arange(SB_WIDTH)[:, None] // HEAD_DIM == jnp.arange(LANES)[None, :]).astype(BF16)
    weights = (g, *wd, wqs, wks, wvs, wga, wgb, bga, bgb, head_sel)
    return pl.pallas_call(
        _proj_kernel,
        grid=(b, s // tm),
        in_specs=[tok(d)] + [full(a) for a in weights],
        out_specs=out_specs,
        out_shape=out_shape,
        scratch_shapes=[pltpu.VMEM((d // LANES, DIL_TILE, LANES), F32)] * PROJ_TILES,
        compiler_params=pltpu.CompilerParams(
            dimension_semantics=("parallel", "parallel"), vmem_limit_bytes=VMEM_LIMIT),
        name="proj",
    )(x, *weights)


DIL_MAX_QB = 8


def _alibi_slope(head):
    return 2.0 ** (-8.0 * (head + 1) / N_DIL_HEADS)


def _dil_kernel(q_ref, kp_ref, kc_ref, vp_ref, vc_ref, o_ref, lse_ref, *, group, dilation, n_steps,
                qb):
    n = pl.program_id(2)
    width = DIL_OUT_WIDTH
    qrows = qb * BLOCK
    q = q_ref[...].reshape(qrows, width)
    k = jnp.concatenate([kp_ref[...].reshape(BLOCK, width), kc_ref[...].reshape(qrows, width)], axis=0)
    v = jnp.concatenate([vp_ref[...].reshape(BLOCK, width), vc_ref[...].reshape(qrows, width)], axis=0)
    qi = lax.broadcasted_iota(jnp.int32, (BLOCK, 2 * BLOCK), 0)
    kj = lax.broadcasted_iota(jnp.int32, (BLOCK, 2 * BLOCK), 1)
    steps = qi + BLOCK - kj
    valid = (steps >= 0) & (steps <= n_steps)
    has_prev = (kj >= BLOCK) | (n > 0)
    dist = (steps * dilation).astype(F32)
    low = lax.broadcasted_iota(jnp.int32, (1, LANES), 1) < HEAD_DIM
    o_blocks = [[None] * 2 for _ in range(qb)]
    lse_blocks = [[None] * 2 for _ in range(qb)]
    sel = (low, jnp.logical_not(low))
    ones = [jnp.broadcast_to(jnp.where(sel[e], 1.0, 0.0).astype(BF16), (2 * BLOCK, LANES))
            for e in range(2)]
    work = [(p, j, e) for p in range(2) for j in range(qb) for e in range(2)]
    q_rows = lambda j: slice(j * BLOCK, (j + 1) * BLOCK)
    k_rows = lambda j: slice(j * BLOCK, (j + 2) * BLOCK)
    pair = lambda t, p: t[:, p * LANES:(p + 1) * LANES]
    qm = {(p, e): jnp.where(sel[e], pair(q, p), jnp.zeros((qrows, LANES), BF16))
          for p in range(2) for e in range(2)}
    vm = {(p, e): jnp.where(sel[e], pair(v, p), jnp.zeros((BLOCK + qrows, LANES), BF16))
          for p in range(2) for e in range(2)}
    bias = {(p, e): jnp.where(valid, -_alibi_slope(group * DIL_HEADS_PER_GROUP + 2 * p + e) * dist,
                              NEG_INF) for p in range(2) for e in range(2)}
    logits = {}
    for p, j, e in work:
        lg = _dot_nt(qm[p, e][q_rows(j)], pair(k, p)[k_rows(j)]) + bias[p, e]
        logits[p, j, e] = jnp.where(has_prev, lg, NEG_INF) if j == 0 else lg
    ms = {w: jnp.max(logits[w], axis=-1, keepdims=True) for w in work}
    pr = {w: jnp.exp(logits[w] - ms[w]).astype(BF16) for w in work}
    t = {(p, j, e): _dot(pr[p, j, e], jnp.concatenate([vm[p, e][k_rows(j)], ones[e]], axis=1))
         for p, j, e in work}
    for p in range(2):
        for j in range(qb):
            acc = t[p, j, 0] + t[p, j, 1]
            num, den = acc[:, :LANES], acc[:, LANES:]
            o_blocks[j][p] = num / den
            lse_blocks[j][p] = jnp.where(low, ms[p, j, 0], ms[p, j, 1]) + jnp.log(den)
    for j in range(qb):
        rows = o_ref.shape[-2] if len(o_ref.shape) == 3 else BLOCK
        tiles = slice(j * BLOCK // rows, (j + 1) * BLOCK // rows)
        for ref, blocks in ((o_ref, o_blocks), (lse_ref, lse_blocks)):
            val = jnp.concatenate(blocks[j], axis=1)
            if len(ref.shape) == 3:
                ref[tiles] = val.reshape(BLOCK // rows, rows, width)
            else:
                ref[j * BLOCK:(j + 1) * BLOCK, :] = val


def _dilated_group(qkv, group):
    window, dilation = DIL_GROUPS[group]
    b, s, _ = qkv.shape
    width = DIL_OUT_WIDTH
    qb = min(DIL_MAX_QB, s // dilation // BLOCK)
    qrows = qb * BLOCK
    n_grid = s // dilation // qrows
    first_prev = lambda n: jnp.maximum(n * qb - 1, 0)
    if dilation == 1:
        view = qkv
        cur = lambda c: pl.BlockSpec((None, qrows, width), lambda bi, r, n: (bi, n, c))
        prev = lambda c: pl.BlockSpec((None, BLOCK, width), lambda bi, r, n: (bi, first_prev(n), c))
        shp = jax.ShapeDtypeStruct((b, s, width), F32)
    else:
        rows = DIL_TILE // dilation
        view = qkv.reshape(b, s // DIL_TILE, dilation, rows, 3 * width)
        cur = lambda c: pl.BlockSpec((None, qrows // rows, None, rows, width),
                                     lambda bi, r, n: (bi, n, r, 0, c))
        prev = lambda c: pl.BlockSpec((None, BLOCK // rows, None, rows, width),
                                      lambda bi, r, n: (bi, first_prev(n), r, 0, c))
        shp = jax.ShapeDtypeStruct((b, s // DIL_TILE, dilation, rows, width), F32)
    return pl.pallas_call(
        functools.partial(_dil_kernel, group=group, dilation=dilation,
                          n_steps=window // dilation, qb=qb),
        grid=(b, dilation, n_grid),
        in_specs=[cur(0), prev(1), cur(1), prev(2), cur(2)],
        out_specs=(cur(0), cur(0)),
        out_shape=(shp, shp),
        compiler_params=pltpu.CompilerParams(
            dimension_semantics=("parallel", "parallel", "parallel"),
            vmem_limit_bytes=VMEM_LIMIT),
        name=f"dilated_d{dilation}",
    )(view, view, view, view, view)


SB_TQ = 64
SB_TK = 128
SB_WINDOW = 256
SB_DEAD = 160.0
SB_BLOCKS_PER_ITER = 10
HEADS_PER_STEP = LANES // HEAD_DIM


def _softplus2(z):
    return jnp.maximum(z, 0.0) + jnp.log2(1.0 + jnp.exp2(-jnp.abs(z)))


def _split_hi_lo(sp):
    parts = []
    for c in range(sp.shape[1] // LANES):
        sp_c = sp[:, c * LANES:(c + 1) * LANES]
        hi = sp_c.astype(BF16)
        parts += [hi, (sp_c - hi.astype(F32)).astype(BF16)]
    return jnp.concatenate(parts, axis=1)


def _sb_suffix_sums(hl, tri):
    return [_dot(hl[:, 2 * c * LANES:2 * (c + 1) * LANES], tri)
            for c in range(hl.shape[1] // (2 * LANES))]


def _sb_weights(z, sums, carry):
    las = [None] * len(sums)
    for c in reversed(range(len(sums))):
        las[c] = z[:, c * LANES:(c + 1) * LANES] - sums[c][:, :LANES] - carry
        carry = carry + sums[c][:, LANES:]
    return jnp.exp2(jnp.concatenate(las, axis=1)).astype(BF16), carry


def _sb_kernel(q_ref, k_ref, v_ref, tri_ref, kn_ref, o_ref, acc_ref, carry_ref):
    s = q_ref.shape[0]
    n_q = s // SB_TQ
    heads = range(HEADS_PER_STEP)
    tri = tri_ref[...]
    lane = lax.broadcasted_iota(jnp.int32, (1, LANES), 1)
    own = [(lane >= h * HEAD_DIM) & (lane < (h + 1) * HEAD_DIM) for h in heads]

    kn = jnp.max(kn_ref[...], axis=(0, 1), keepdims=True).reshape(1, LANES)
    k_max = [jnp.sqrt(jnp.max(jnp.where(lane == HEADS_PER_STEP * pl.program_id(1) + h, kn, 0.0)))
             * (1.0 + 2.0 ** -8) for h in heads]

    col_minus_row = (lax.broadcasted_iota(jnp.int32, (SB_TQ, SB_WINDOW), 1)
                     - lax.broadcasted_iota(jnp.int32, (SB_TQ, SB_WINDOW), 0))
    lead = SB_WINDOW - SB_TQ

    def query_blocks(blocks):
        items = [(slot, h) for slot in range(len(blocks)) for h in heads]
        rows, start, causal, q, z_max = {}, {}, {}, {}, {}
        for slot, qi in enumerate(blocks):
            first = qi * SB_TQ
            if isinstance(qi, int):
                start[slot] = max(first - lead, 0)
                rows[slot] = pl.ds(first, SB_TQ)
            else:
                start[slot] = pl.multiple_of(jnp.maximum(first - lead, 0), SB_TQ)
                rows[slot] = pl.ds(pl.multiple_of(first, SB_TQ), SB_TQ)
            causal[slot] = col_minus_row < (first - start[slot] if isinstance(qi, int) else lead)
            q_pair = q_ref[rows[slot], :]
            for h in heads:
                q[slot, h] = jnp.where(own[h], q_pair, jnp.zeros_like(q_pair))
                z_max[slot, h] = (jnp.sqrt(jnp.sum(jnp.square(q[slot, h].astype(F32)), axis=1,
                                                   keepdims=True)) * (k_max[h] * 1.001) + 1.0)
        z = {(slot, h): jnp.where(causal[slot],
                                  _dot_nt(q[slot, h], k_ref[pl.ds(start[slot], SB_WINDOW), :]),
                                  NEG_INF) for slot, h in items}
        hl = {w: _split_hi_lo(_softplus2(z[w])) for w in items}
        sums = {w: _sb_suffix_sums(hl[w], tri) for w in items}
        a, carry = {}, {}
        for w in items:
            a[w], carry[w] = _sb_weights(z[w], sums[w], jnp.zeros((SB_TQ, LANES), F32))
        for slot, h in items:
            acc_ref[slot, h] = _dot(a[slot, h], v_ref[pl.ds(start[slot], SB_WINDOW), :])
            carry_ref[slot, h] = carry[slot, h]

        def slack(slot, carries):
            per_head = [carries[h] - z_max[slot, h] for h in heads]
            return jnp.min(jnp.minimum(per_head[0], per_head[1]), keepdims=True)

        slacks = [slack(slot, [carry[slot, h] for h in heads]) for slot in range(len(blocks))]

        @pl.when(functools.reduce(jnp.minimum, slacks)[0, 0] < SB_DEAD)
        def _():
            for slot in range(len(blocks)):
                def cond(state):
                    ke, live = state
                    return (ke > 0) & live

                def body(state):
                    ke, _ = state
                    ks = pl.multiple_of(jnp.maximum(ke - SB_TK, 0), SB_TQ)
                    fresh = (lax.broadcasted_iota(jnp.int32, (SB_TQ, SB_TK), 1) + ks) < ke
                    carries = []
                    for h in heads:
                        zt = jnp.where(fresh, _dot_nt(q[slot, h], k_ref[pl.ds(ks, SB_TK), :]),
                                       NEG_INF)
                        at, ct = _sb_weights(
                            zt, _sb_suffix_sums(_split_hi_lo(_softplus2(zt)), tri),
                            carry_ref[slot, h])
                        acc_ref[slot, h] += _dot(at, v_ref[pl.ds(ks, SB_TK), :])
                        carry_ref[slot, h] = ct
                        carries.append(ct)
                    return ks, slack(slot, carries)[0, 0] < SB_DEAD

                lax.while_loop(cond, body, (jnp.asarray(start[slot], jnp.int32),
                                            slacks[slot][0, 0] < SB_DEAD))

        for slot in range(len(blocks)):
            o_ref[rows[slot], :] = jnp.where(own[0], acc_ref[slot, 0],
                                             acc_ref[slot, 1]).astype(o_ref.dtype)

    n_lead = lead // SB_TQ
    query_blocks(list(range(n_lead)))
    n_groups = (n_q - n_lead) // SB_BLOCKS_PER_ITER

    def group(i, _):
        query_blocks([n_lead + i * SB_BLOCKS_PER_ITER + j for j in range(SB_BLOCKS_PER_ITER)])
        return 0

    lax.fori_loop(0, n_groups, group, 0)
    rest = list(range(n_lead + n_groups * SB_BLOCKS_PER_ITER, n_q))
    if rest:
        query_blocks(rest)


def _sb_attention(qs, ks, vs, k_norm2):
    b, s, _ = qs.shape
    assert s % SB_TQ == 0 and s >= SB_WINDOW and SB_WINDOW % LANES == 0 and HEADS_PER_STEP == 2
    t = (jnp.arange(LANES)[:, None] >= jnp.arange(LANES)[None, :]).astype(BF16)
    half = jnp.concatenate([t, jnp.ones((LANES, LANES), BF16)], axis=1)
    tri = jnp.concatenate([half, half], axis=0)
    nh = HEADS_PER_STEP
    seq = pl.BlockSpec((None, s, LANES), lambda bi, p: (bi, 0, p))
    return pl.pallas_call(
        _sb_kernel,
        grid=(b, SB_WIDTH // LANES),
        in_specs=[seq, seq, seq, pl.BlockSpec(tri.shape, lambda bi, p: (0, 0)),
                  pl.BlockSpec((None,) + k_norm2.shape[1:], lambda bi, p: (bi, 0, 0, 0))],
        out_specs=seq,
        out_shape=jax.ShapeDtypeStruct((b, s, SB_WIDTH), BF16),
        scratch_shapes=[pltpu.VMEM((SB_BLOCKS_PER_ITER, nh, SB_TQ, LANES), F32)] * 2,
        compiler_params=pltpu.CompilerParams(
            dimension_semantics=("parallel", "parallel"), vmem_limit_bytes=VMEM_LIMIT),
        name="stick_breaking",
    )(qs, ks, vs, tri, k_norm2)


TAIL_TILES = 2
MLP_FF_CHUNK = 1024


def _token_major(ref, scr):
    d, rows, _ = ref.shape
    for c in range(scr.shape[0]):
        for r in range(d):
            scr[c, pl.ds(r, rows, stride=d), :] = ref[r, :, c * LANES:(c + 1) * LANES]
    return jnp.concatenate([scr[c] for c in range(scr.shape[0])], axis=1)


def _tail_kernel(x_ref, o0, o1, o2, l0, l1, l2, ob_ref, ga_ref, gb_ref,
                 wud, wus, wout, g_mlp, w1, w2, g_out, out_ref, *scratch, final_norm):
    tiles = range(TAIL_TILES)
    rows = [slice(t * DIL_TILE, (t + 1) * DIL_TILE) for t in tiles]
    o_a = []
    for t in tiles:
        scr = scratch[4 * t:4 * t + 4]
        outs = [o0[rows[t], :], _token_major(o1.at[t], scr[0]), _token_major(o2.at[t], scr[1])]
        lse = [l0[rows[t], :], _token_major(l1.at[t], scr[2]), _token_major(l2.at[t], scr[3])]
        m = jnp.maximum(jnp.maximum(lse[0], lse[1]), lse[2])
        e = [jnp.exp(l - m) for l in lse]
        o_a.append((e[0] * outs[0] + e[1] * outs[1] + e[2] * outs[2]) / (e[0] + e[1] + e[2]))
    up_a = [_dot(o_a[t].astype(BF16), wud[...]) for t in tiles]
    up_b = [_dot(ob_ref[rows[t], :], wus[...]) for t in tiles]
    merged = [ga_ref[rows[t], :] * up_a[t] + gb_ref[rows[t], :] * up_b[t] for t in tiles]
    y = [x_ref[rows[t], :] + _dot(merged[t].astype(BF16), wout[...]) for t in tiles]
    h2 = [_rms(y[t], g_mlp[...]).astype(BF16) for t in tiles]
    for c in range(D_FF // MLP_FF_CHUNK):
        cs = slice(c * MLP_FF_CHUNK, (c + 1) * MLP_FF_CHUNK)
        u = [jnp.maximum(_dot(h2[t], w1[:, cs]), 0.0) for t in tiles]
        y = [y[t] + _dot((u[t] * u[t]).astype(BF16), w2[cs, :]) for t in tiles]
    for t in tiles:
        out_ref[rows[t], :] = _rms(y[t], g_out[...]) if final_norm else y[t]


def _tail(x, oas, lses, o_b, g_a, g_b, wud, wus, wout, g_mlp, w1, w2, g_out, final_norm):
    b, s, d = x.shape
    tm = TAIL_TILES * DIL_TILE
    tok = lambda w: pl.BlockSpec((None, tm, w), lambda bi, i: (bi, i, 0))
    full = lambda a: pl.BlockSpec(a.shape, lambda bi, i: (0,) * a.ndim,
                                  pipeline_mode=pl.Buffered(1))

    def dil(a):
        if a.ndim == 3:
            return tok(DIL_OUT_WIDTH)
        return pl.BlockSpec((None, TAIL_TILES) + a.shape[2:], lambda bi, i: (bi, i, 0, 0, 0))

    weights = (wud, wus, wout, g_mlp, w1, w2, g_out)
    return pl.pallas_call(
        functools.partial(_tail_kernel, final_norm=final_norm),
        grid=(b, s // tm),
        in_specs=[tok(d)] + [dil(a) for a in (*oas, *lses)] + [tok(SB_WIDTH), tok(d), tok(d)]
                 + [full(a) for a in weights],
        out_specs=tok(d),
        out_shape=jax.ShapeDtypeStruct((b, s, d), F32),
        scratch_shapes=[pltpu.VMEM((DIL_OUT_WIDTH // LANES, DIL_TILE, LANES), F32)]
                       * (4 * TAIL_TILES),
        compiler_params=pltpu.CompilerParams(
            dimension_semantics=("parallel", "parallel"), vmem_limit_bytes=VMEM_LIMIT),
        name="tail",
    )(x, *oas, *lses, o_b, g_a, g_b, *weights)


def kernel(x, norm_mix_g, w_in, b_gate, w_up_dil, w_up_sb, w_out, norm_mlp_g, w_mlp_in,
           w_mlp_out, norm_final_g):
    depth = w_in.shape[0]
    row = lambda v: v.reshape(1, -1)
    for layer in range(depth):
        w = w_in[layer]
        c0 = 3 * DIL_WIDTH
        c1 = c0 + 3 * SB_WIDTH
        dil_cols = lambda i, grp: w[:, i * DIL_WIDTH + grp * DIL_OUT_WIDTH:
                                    i * DIL_WIDTH + (grp + 1) * DIL_OUT_WIDTH]
        wd = [jnp.concatenate([dil_cols(0, grp) * (1.0 / math.sqrt(HEAD_DIM)), dil_cols(1, grp),
                               dil_cols(2, grp)], axis=1).astype(BF16)
              for grp in range(len(DIL_GROUPS))]
        w = w.astype(BF16)
        wqs, wks, wvs = (w[:, c0 + i * SB_WIDTH:c0 + (i + 1) * SB_WIDTH] for i in range(3))
        wga, wgb = w[:, c1:c1 + D_MODEL], w[:, c1 + D_MODEL:]
        bga, bgb = row(b_gate[layer][:D_MODEL]), row(b_gate[layer][D_MODEL:])
        *qkv, qs, ks, vs, g_a, g_b, k_norm2 = _proj(
            x, row(norm_mix_g[layer]), wd, wqs, wks, wvs, wga, wgb, bga, bgb)
        dil = [_dilated_group(qkv[grp], grp) for grp in range(len(DIL_GROUPS))]
        o_b = _sb_attention(qs, ks, vs, k_norm2)
        x = _tail(x, [o for o, _ in dil], [l for _, l in dil], o_b, g_a, g_b,
                  w_up_dil[layer].astype(BF16), w_up_sb[layer].astype(BF16),
                  w_out[layer].astype(BF16), row(norm_mlp_g[layer]),
                  w_mlp_in[layer].astype(BF16), w_mlp_out[layer].astype(BF16),
                  row(norm_final_g), final_norm=layer == depth - 1)
    return x
```

```python
import functools
import math

import jax
import jax.numpy as jnp
from jax import lax
from jax.experimental import pallas as pl
from jax.experimental.pallas import tpu as pltpu

D_MODEL = 1024
HEAD_DIM = 64
DIL_GROUPS = ((128, 1), (512, 4), (2048, 16))
DIL_HEADS_PER_GROUP = 4
N_DIL_HEADS = DIL_HEADS_PER_GROUP * len(DIL_GROUPS)
N_SB_HEADS = 8
DIL_WIDTH = N_DIL_HEADS * HEAD_DIM
DIL_OUT_WIDTH = DIL_HEADS_PER_GROUP * HEAD_DIM
SB_WIDTH = N_SB_HEADS * HEAD_DIM
D_FF = 4 * D_MODEL
BLOCK = 128
RMS_EPS = 1e-6
NEG_INF = -1e30
LOG2E = 1.4426950408889634
LANES = 128

VMEM_LIMIT = 56 * 1024 * 1024

BF16 = jnp.bfloat16
F32 = jnp.float32


def _dot(a, b):
    return jnp.dot(a, b, preferred_element_type=F32)


def _dot_nt(a, b):
    return lax.dot_general(a, b, (((1,), (1,)), ((), ())), preferred_element_type=F32)


def _rms(x, g):
    return x * lax.rsqrt(jnp.mean(x * x, axis=-1, keepdims=True) + RMS_EPS) * g


def _sigmoid(x):
    return 1.0 / (1.0 + jnp.exp(-x))


DIL_TILE = 256
PROJ_TILES = 2


def _residue_major(h_ref, d):
    rows = DIL_TILE // d
    return jnp.concatenate(
        [jnp.concatenate([h_ref[c, pl.ds(r, rows, stride=d), :] for r in range(d)], axis=0)
         for c in range(h_ref.shape[0])], axis=1)


def _proj_kernel(x_ref, g_ref, wd0, wd1, wd2, wqs, wks, wvs, wga, wgb, bga, bgb, head_sel,
                 d0_ref, d1_ref, d2_ref, qs_ref, ks_ref, vs_ref, ga_ref, gb_ref, kn_ref, *h_scr):
    tiles = range(PROJ_TILES)
    rows = [slice(t * DIL_TILE, (t + 1) * DIL_TILE) for t in tiles]
    h = []
    for t in tiles:
        h32 = _rms(x_ref[rows[t], :], g_ref[...])
        for c in range(h_scr[t].shape[0]):
            h_scr[t][c] = h32[:, c * LANES:(c + 1) * LANES]
        h.append(h32.astype(BF16))
    for (_, dilation), w_ref, o_ref in zip(DIL_GROUPS, (wd0, wd1, wd2), (d0_ref, d1_ref, d2_ref)):
        for t in tiles:
            hd = h[t] if dilation == 1 else _residue_major(h_scr[t], dilation).astype(BF16)
            o_ref[rows[t], :] = _dot(hd, w_ref[...]).astype(BF16)
    for t in tiles:
        qs_ref[rows[t], :] = (_dot(h[t], wqs[...]) * (LOG2E / math.sqrt(HEAD_DIM))).astype(BF16)
    k_norm2 = None
    for t in tiles:
        k = _dot(h[t], wks[...]).astype(BF16)
        ks_ref[rows[t], :] = k
        n2 = jnp.max(_dot(jnp.square(k.astype(F32)).astype(BF16), head_sel[...]), axis=0,
                     keepdims=True)
        k_norm2 = n2 if k_norm2 is None else jnp.maximum(k_norm2, n2)
    kn_ref[...] = jnp.broadcast_to(k_norm2, kn_ref.shape)
    for t in tiles:
        vs_ref[rows[t], :] = _dot(h[t], wvs[...]).astype(BF16)
    for t in tiles:
        ga_ref[rows[t], :] = _sigmoid(_dot(h[t], wga[...]) + bga[...])
    for t in tiles:
        gb_ref[rows[t], :] = _sigmoid(_dot(h[t], wgb[...]) + bgb[...])


def _proj(x, g, wd, wqs, wks, wvs, wga, wgb, bga, bgb):
    b, s, d = x.shape
    tm = PROJ_TILES * DIL_TILE
    tok = lambda w: pl.BlockSpec((None, tm, w), lambda bi, i: (bi, i, 0))
    full = lambda a: pl.BlockSpec(a.shape, lambda bi, i: (0,) * a.ndim,
                                  pipeline_mode=pl.Buffered(1))
    dil_width = 3 * DIL_OUT_WIDTH
    out_shape = (
        (jax.ShapeDtypeStruct((b, s, dil_width), BF16),) * len(DIL_GROUPS)
        + (jax.ShapeDtypeStruct((b, s, SB_WIDTH), BF16),) * 3
        + (jax.ShapeDtypeStruct((b, s, D_MODEL), F32),) * 2
        + (jax.ShapeDtypeStruct((b, s // tm, 8, LANES), F32),))
    out_specs = ((tok(dil_width),) * len(DIL_GROUPS) + (tok(SB_WIDTH),) * 3
                 + (tok(D_MODEL),) * 2
                 + (pl.BlockSpec((None, None, 8, LANES), lambda bi, i: (bi, i, 0, 0)),))
    head_sel = (jnp.arange(SB_WIDTH)[:, None] // HEAD_DIM == jnp.arange(LANES)[None, :]).astype(BF16)
    weights = (g, *wd, wqs, wks, wvs, wga, wgb, bga, bgb, head_sel)
    return pl.pallas_call(
        _proj_kernel,
        grid=(b, s // tm),
        in_specs=[tok(d)] + [full(a) for a in weights],
        out_specs=out_specs,
        out_shape=out_shape,
        scratch_shapes=[pltpu.VMEM((d // LANES, DIL_TILE, LANES), F32)] * PROJ_TILES,
        compiler_params=pltpu.CompilerParams(
            dimension_semantics=("parallel", "parallel"), vmem_limit_bytes=VMEM_LIMIT),
        name="proj",
    )(x, *weights)


DIL_MAX_QB = 8


def _alibi_slope(head):
    return 2.0 ** (-8.0 * (head + 1) / N_DIL_HEADS)


def _dil_kernel(q_ref, kp_ref, kc_ref, vp_ref, vc_ref, o_ref, lse_ref, *, group, dilation, n_steps,
                qb):
    n = pl.program_id(2)
    width = DIL_OUT_WIDTH
    qrows = qb * BLOCK
    n_res = q_ref.shape[1] if len(q_ref.shape) == 4 else 1
    res = range(n_res)
    take = lambda ref, r, n_rows: (ref[:, r] if len(ref.shape) == 4 else ref[...]).reshape(n_rows, width)
    q = [take(q_ref, r, qrows) for r in res]
    k = [jnp.concatenate([take(kp_ref, r, BLOCK), take(kc_ref, r, qrows)], axis=0) for r in res]
    v = [jnp.concatenate([take(vp_ref, r, BLOCK), take(vc_ref, r, qrows)], axis=0) for r in res]
    qi = lax.broadcasted_iota(jnp.int32, (BLOCK, 2 * BLOCK), 0)
    kj = lax.broadcasted_iota(jnp.int32, (BLOCK, 2 * BLOCK), 1)
    steps = qi + BLOCK - kj
    valid = (steps >= 0) & (steps <= n_steps)
    has_prev = (kj >= BLOCK) | (n > 0)
    dist = (steps * dilation).astype(F32)
    low = lax.broadcasted_iota(jnp.int32, (1, LANES), 1) < HEAD_DIM
    sel = (low, jnp.logical_not(low))
    ones = [jnp.broadcast_to(jnp.where(sel[e], 1.0, 0.0).astype(BF16), (2 * BLOCK, LANES))
            for e in range(2)]
    work = [(r, p, j, e) for r in res for p in range(2) for j in range(qb) for e in range(2)]
    q_rows = lambda j: slice(j * BLOCK, (j + 1) * BLOCK)
    k_rows = lambda j: slice(j * BLOCK, (j + 2) * BLOCK)
    pair = lambda t, p: t[:, p * LANES:(p + 1) * LANES]
    qm = {(r, p, e): jnp.where(sel[e], pair(q[r], p), jnp.zeros((qrows, LANES), BF16))
          for r in res for p in range(2) for e in range(2)}
    vm = {(r, p, e): jnp.where(sel[e], pair(v[r], p), jnp.zeros((BLOCK + qrows, LANES), BF16))
          for r in res for p in range(2) for e in range(2)}
    bias = {(p, e): jnp.where(valid, -_alibi_slope(group * DIL_HEADS_PER_GROUP + 2 * p + e) * dist,
                              NEG_INF) for p in range(2) for e in range(2)}
    logits = {}
    for r, p, j, e in work:
        lg = _dot_nt(qm[r, p, e][q_rows(j)], pair(k[r], p)[k_rows(j)]) + bias[p, e]
        logits[r, p, j, e] = jnp.where(has_prev, lg, NEG_INF) if j == 0 else lg
    ms = {w: jnp.max(logits[w], axis=-1, keepdims=True) for w in work}
    pr = {w: jnp.exp(logits[w] - ms[w]).astype(BF16) for w in work}
    t = {(r, p, j, e): _dot(pr[r, p, j, e],
                            jnp.concatenate([vm[r, p, e][k_rows(j)], ones[e]], axis=1))
         for r, p, j, e in work}
    for r in res:
        for j in range(qb):
            o_pairs, lse_pairs = [], []
            for p in range(2):
                acc = t[r, p, j, 0] + t[r, p, j, 1]
                num, den = acc[:, :LANES], acc[:, LANES:]
                o_pairs.append(num / den)
                lse_pairs.append(jnp.where(low, ms[r, p, j, 0], ms[r, p, j, 1]) + jnp.log(den))
            for ref, pairs in ((o_ref, o_pairs), (lse_ref, lse_pairs)):
                val = jnp.concatenate(pairs, axis=1)
                if len(ref.shape) == 4:
                    rows = ref.shape[2]
                    tiles = slice(j * BLOCK // rows, (j + 1) * BLOCK // rows)
                    ref[tiles, r] = val.reshape(BLOCK // rows, rows, width)
                else:
                    ref[j * BLOCK:(j + 1) * BLOCK, :] = val


def _dilated_group(qkv, group):
    window, dilation = DIL_GROUPS[group]
    b, s, _ = qkv.shape
    width = DIL_OUT_WIDTH
    qb = min(DIL_MAX_QB, s // dilation // BLOCK)
    n_res = min(dilation, DIL_MAX_QB // qb)
    qrows = qb * BLOCK
    n_grid = s // dilation // qrows
    first_prev = lambda n: jnp.maximum(n * qb - 1, 0)
    if dilation == 1:
        view = qkv
        cur = lambda c: pl.BlockSpec((None, qrows, width), lambda bi, r, n: (bi, n, c))
        prev = lambda c: pl.BlockSpec((None, BLOCK, width), lambda bi, r, n: (bi, first_prev(n), c))
        shp = jax.ShapeDtypeStruct((b, s, width), F32)
    else:
        rows = DIL_TILE // dilation
        view = qkv.reshape(b, s // DIL_TILE, dilation, rows, 3 * width)
        cur = lambda c: pl.BlockSpec((None, qrows // rows, n_res, rows, width),
                                     lambda bi, r, n: (bi, n, r, 0, c))
        prev = lambda c: pl.BlockSpec((None, BLOCK // rows, n_res, rows, width),
                                      lambda bi, r, n: (bi, first_prev(n), r, 0, c))
        shp = jax.ShapeDtypeStruct((b, s // DIL_TILE, dilation, rows, width), F32)
    return pl.pallas_call(
        functools.partial(_dil_kernel, group=group, dilation=dilation,
                          n_steps=window // dilation, qb=qb),
        grid=(b, dilation // n_res, n_grid),
        in_specs=[cur(0), prev(1), cur(1), prev(2), cur(2)],
        out_specs=(cur(0), cur(0)),
        out_shape=(shp, shp),
        compiler_params=pltpu.CompilerParams(
            dimension_semantics=("parallel", "parallel", "parallel"),
            vmem_limit_bytes=VMEM_LIMIT),
        name=f"dilated_d{dilation}",
    )(view, view, view, view, view)


SB_TQ = 64
SB_TK = 256
SB_WINDOW = 256
SB_DEAD = 160.0
SB_BLOCKS_PER_ITER = 10
HEADS_PER_STEP = LANES // HEAD_DIM


def _softplus2(z):
    return jnp.maximum(z, 0.0) + jnp.log2(1.0 + jnp.exp2(-jnp.abs(z)))


def _split_hi_lo(sp):
    parts = []
    for c in range(sp.shape[1] // LANES):
        sp_c = sp[:, c * LANES:(c + 1) * LANES]
        hi = sp_c.astype(BF16)
        parts += [hi, (sp_c - hi.astype(F32)).astype(BF16)]
    return jnp.concatenate(parts, axis=1)


def _sb_suffix_sums(hl, tri):
    return [_dot(hl[:, 2 * c * LANES:2 * (c + 1) * LANES], tri)
            for c in range(hl.shape[1] // (2 * LANES))]


def _sb_weights(z, sums, carry):
    las = [None] * len(sums)
    for c in reversed(range(len(sums))):
        las[c] = z[:, c * LANES:(c + 1) * LANES] - sums[c][:, :LANES] - carry
        carry = carry + sums[c][:, LANES:]
    return jnp.exp2(jnp.concatenate(las, axis=1)).astype(BF16), carry


def _sb_kernel(q_ref, k_ref, v_ref, tri_ref, kn_ref, o_ref, acc_ref, carry_ref):
    s = q_ref.shape[0]
    n_q = s // SB_TQ
    heads = range(HEADS_PER_STEP)
    tri = tri_ref[...]
    lane = lax.broadcasted_iota(jnp.int32, (1, LANES), 1)
    own = [(lane >= h * HEAD_DIM) & (lane < (h + 1) * HEAD_DIM) for h in heads]

    kn = jnp.max(kn_ref[...], axis=(0, 1), keepdims=True).reshape(1, LANES)
    k_max = [jnp.sqrt(jnp.max(jnp.where(lane == HEADS_PER_STEP * pl.program_id(1) + h, kn, 0.0)))
             * (1.0 + 2.0 ** -8) for h in heads]

    col_minus_row = (lax.broadcasted_iota(jnp.int32, (SB_TQ, SB_WINDOW), 1)
                     - lax.broadcasted_iota(jnp.int32, (SB_TQ, SB_WINDOW), 0))
    lead = SB_WINDOW - SB_TQ

    def query_blocks(blocks):
        items = [(slot, h) for slot in range(len(blocks)) for h in heads]
        rows, start, causal, q, z_max = {}, {}, {}, {}, {}
        for slot, qi in enumerate(blocks):
            first = qi * SB_TQ
            if isinstance(qi, int):
                start[slot] = max(first - lead, 0)
                rows[slot] = pl.ds(first, SB_TQ)
            else:
                start[slot] = pl.multiple_of(jnp.maximum(first - lead, 0), SB_TQ)
                rows[slot] = pl.ds(pl.multiple_of(first, SB_TQ), SB_TQ)
            causal[slot] = col_minus_row < (first - start[slot] if isinstance(qi, int) else lead)
            q_pair = q_ref[rows[slot], :]
            for h in heads:
                q[slot, h] = jnp.where(own[h], q_pair, jnp.zeros_like(q_pair))
                z_max[slot, h] = (jnp.sqrt(jnp.sum(jnp.square(q[slot, h].astype(F32)), axis=1,
                                                   keepdims=True)) * (k_max[h] * 1.001) + 1.0)
        z = {(slot, h): jnp.where(causal[slot],
                                  _dot_nt(q[slot, h], k_ref[pl.ds(start[slot], SB_WINDOW), :]),
                                  NEG_INF) for slot, h in items}
        hl = {w: _split_hi_lo(_softplus2(z[w])) for w in items}
        sums = {w: _sb_suffix_sums(hl[w], tri) for w in items}
        a, carry = {}, {}
        for w in items:
            a[w], carry[w] = _sb_weights(z[w], sums[w], jnp.zeros((SB_TQ, LANES), F32))
        for slot, h in items:
            acc_ref[slot, h] = _dot(a[slot, h], v_ref[pl.ds(start[slot], SB_WINDOW), :])
            carry_ref[slot, h] = carry[slot, h]

        def slack(slot, carries):
            per_head = [carries[h] - z_max[slot, h] for h in heads]
            return jnp.min(jnp.minimum(per_head[0], per_head[1]), keepdims=True)

        slacks = [slack(slot, [carry[slot, h] for h in heads]) for slot in range(len(blocks))]

        @pl.when(functools.reduce(jnp.minimum, slacks)[0, 0] < SB_DEAD)
        def _():
            for slot in range(len(blocks)):
                def cond(state):
                    ke, live = state
                    return (ke > 0) & live

                def body(state):
                    ke, _ = state
                    ks = pl.multiple_of(jnp.maximum(ke - SB_TK, 0), SB_TQ)
                    fresh = (lax.broadcasted_iota(jnp.int32, (SB_TQ, SB_TK), 1) + ks) < ke
                    carries = []
                    for h in heads:
                        zt = jnp.where(fresh, _dot_nt(q[slot, h], k_ref[pl.ds(ks, SB_TK), :]),
                                       NEG_INF)
                        at, ct = _sb_weights(
                            zt, _sb_suffix_sums(_split_hi_lo(_softplus2(zt)), tri),
                            carry_ref[slot, h])
                        acc_ref[slot, h] += _dot(at, v_ref[pl.ds(ks, SB_TK), :])
                        carry_ref[slot, h] = ct
                        carries.append(ct)
                    return ks, slack(slot, carries)[0, 0] < SB_DEAD

                lax.while_loop(cond, body, (jnp.asarray(start[slot], jnp.int32),
                                            slacks[slot][0, 0] < SB_DEAD))

        for slot in range(len(blocks)):
            o_ref[rows[slot], :] = jnp.where(own[0], acc_ref[slot, 0],
                                             acc_ref[slot, 1]).astype(o_ref.dtype)

    n_lead = lead // SB_TQ
    query_blocks(list(range(n_lead)))
    n_groups = (n_q - n_lead) // SB_BLOCKS_PER_ITER

    def group(i, _):
        query_blocks([n_lead + i * SB_BLOCKS_PER_ITER + j for j in range(SB_BLOCKS_PER_ITER)])
        return 0

    lax.fori_loop(0, n_groups, group, 0)
    rest = list(range(n_lead + n_groups * SB_BLOCKS_PER_ITER, n_q))
    if rest:
        query_blocks(rest)


def _sb_attention(qs, ks, vs, k_norm2):
    b, s, _ = qs.shape
    assert s % SB_TQ == 0 and s >= SB_WINDOW and SB_WINDOW % LANES == 0 and HEADS_PER_STEP == 2
    t = (jnp.arange(LANES)[:, None] >= jnp.arange(LANES)[None, :]).astype(BF16)
    half = jnp.concatenate([t, jnp.ones((LANES, LANES), BF16)], axis=1)
    tri = jnp.concatenate([half, half], axis=0)
    nh = HEADS_PER_STEP
    seq = pl.BlockSpec((None, s, LANES), lambda bi, p: (bi, 0, p))
    return pl.pallas_call(
        _sb_kernel,
        grid=(b, SB_WIDTH // LANES),
        in_specs=[seq, seq, seq, pl.BlockSpec(tri.shape, lambda bi, p: (0, 0)),
                  pl.BlockSpec((None,) + k_norm2.shape[1:], lambda bi, p: (bi, 0, 0, 0))],
        out_specs=seq,
        out_shape=jax.ShapeDtypeStruct((b, s, SB_WIDTH), BF16),
        scratch_shapes=[pltpu.VMEM((SB_BLOCKS_PER_ITER, nh, SB_TQ, LANES), F32)] * 2,
        compiler_params=pltpu.CompilerParams(
            dimension_semantics=("parallel", "parallel"), vmem_limit_bytes=VMEM_LIMIT),
        name="stick_breaking",
    )(qs, ks, vs, tri, k_norm2)


TAIL_TILES = 2
MLP_FF_CHUNK = 1024


def _token_major(ref, scr):
    d, rows, _ = ref.shape
    for c in range(scr.shape[0]):
        for r in range(d):
            scr[c, pl.ds(r, rows, stride=d), :] = ref[r, :, c * LANES:(c + 1) * LANES]
    return jnp.concatenate([scr[c] for c in range(scr.shape[0])], axis=1)


def _tail_kernel(x_ref, o0, o1, o2, l0, l1, l2, ob_ref, ga_ref, gb_ref,
                 wud, wus, wout, g_mlp, w1, w2, g_out, out_ref, *scratch, final_norm):
    tiles = range(TAIL_TILES)
    rows = [slice(t * DIL_TILE, (t + 1) * DIL_TILE) for t in tiles]
    o_a = []
    for t in tiles:
        scr = scratch[4 * t:4 * t + 4]
        outs = [o0[rows[t], :], _token_major(o1.at[t], scr[0]), _token_major(o2.at[t], scr[1])]
        lse = [l0[rows[t], :], _token_major(l1.at[t], scr[2]), _token_major(l2.at[t], scr[3])]
        m = jnp.maximum(jnp.maximum(lse[0], lse[1]), lse[2])
        e = [jnp.exp(l - m) for l in lse]
        o_a.append((e[0] * outs[0] + e[1] * outs[1] + e[2] * outs[2]) / (e[0] + e[1] + e[2]))
    up_a = [_dot(o_a[t].astype(BF16), wud[...]) for t in tiles]
    up_b = [_dot(ob_ref[rows[t], :], wus[...]) for t in tiles]
    merged = [ga_ref[rows[t], :] * up_a[t] + gb_ref[rows[t], :] * up_b[t] for t in tiles]
    y = [x_ref[rows[t], :] + _dot(merged[t].astype(BF16), wout[...]) for t in tiles]
    h2 = [_rms(y[t], g_mlp[...]).astype(BF16) for t in tiles]
    for c in range(D_FF // MLP_FF_CHUNK):
        cs = slice(c * MLP_FF_CHUNK, (c + 1) * MLP_FF_CHUNK)
        u = [jnp.maximum(_dot(h2[t], w1[:, cs]), 0.0) for t in tiles]
        y = [y[t] + _dot((u[t] * u[t]).astype(BF16), w2[cs, :]) for t in tiles]
    for t in tiles:
        out_ref[rows[t], :] = _rms(y[t], g_out[...]) if final_norm else y[t]


def _tail(x, oas, lses, o_b, g_a, g_b, wud, wus, wout, g_mlp, w1, w2, g_out, final_norm):
    b, s, d = x.shape
    tm = TAIL_TILES * DIL_TILE
    tok = lambda w: pl.BlockSpec((None, tm, w), lambda bi, i: (bi, i, 0))
    full = lambda a: pl.BlockSpec(a.shape, lambda bi, i: (0,) * a.ndim,
                                  pipeline_mode=pl.Buffered(1))

    def dil(a):
        if a.ndim == 3:
            return tok(DIL_OUT_WIDTH)
        return pl.BlockSpec((None, TAIL_TILES) + a.shape[2:], lambda bi, i: (bi, i, 0, 0, 0))

    weights = (wud, wus, wout, g_mlp, w1, w2, g_out)
    return pl.pallas_call(
        functools.partial(_tail_kernel, final_norm=final_norm),
        grid=(b, s // tm),
        in_specs=[tok(d)] + [dil(a) for a in (*oas, *lses)] + [tok(SB_WIDTH), tok(d), tok(d)]
                 + [full(a) for a in weights],
        out_specs=tok(d),
        out_shape=jax.ShapeDtypeStruct((b, s, d), F32),
        scratch_shapes=[pltpu.VMEM((DIL_OUT_WIDTH // LANES, DIL_TILE, LANES), F32)]
                       * (4 * TAIL_TILES),
        compiler_params=pltpu.CompilerParams(
            dimension_semantics=("parallel", "parallel"), vmem_limit_bytes=VMEM_LIMIT),
        name="tail",
    )(x, *oas, *lses, o_b, g_a, g_b, *weights)


def kernel(x, norm_mix_g, w_in, b_gate, w_up_dil, w_up_sb, w_out, norm_mlp_g, w_mlp_in,
           w_mlp_out, norm_final_g):
    depth = w_in.shape[0]
    row = lambda v: v.reshape(1, -1)
    for layer in range(depth):
        w = w_in[layer]
        c0 = 3 * DIL_WIDTH
        c1 = c0 + 3 * SB_WIDTH
        dil_cols = lambda i, grp: w[:, i * DIL_WIDTH + grp * DIL_OUT_WIDTH:
                                    i * DIL_WIDTH + (grp + 1) * DIL_OUT_WIDTH]
        wd = [jnp.concatenate([dil_cols(0, grp) * (1.0 / math.sqrt(HEAD_DIM)), dil_cols(1, grp),
                               dil_cols(2, grp)], axis=1).astype(BF16)
              for grp in range(len(DIL_GROUPS))]
        w = w.astype(BF16)
        wqs, wks, wvs = (w[:, c0 + i * SB_WIDTH:c0 + (i + 1) * SB_WIDTH] for i in range(3))
        wga, wgb = w[:, c1:c1 + D_MODEL], w[:, c1 + D_MODEL:]
        bga, bgb = row(b_gate[layer][:D_MODEL]), row(b_gate[layer][D_MODEL:])
        *qkv, qs, ks, vs, g_a, g_b, k_norm2 = _proj(
            x, row(norm_mix_g[layer]), wd, wqs, wks, wvs, wga, wgb, bga, bgb)
        dil = [_dilated_group(qkv[grp], grp) for grp in range(len(DIL_GROUPS))]
        o_b = _sb_attention(qs, ks, vs, k_norm2)
        x = _tail(x, [o for o, _ in dil], [l for _, l in dil], o_b, g_a, g_b,
                  w_up_dil[layer].astype(BF16), w_up_sb[layer].astype(BF16),
                  w_out[layer].astype(BF16), row(norm_mlp_g[layer]),
                  w_mlp_in[layer].astype(BF16), w_mlp_out[layer].astype(BF16),
                  row(norm_final_g), final_norm=layer == depth - 1)
    return x
```

```python
import functools
import math

import jax
import jax.numpy as jnp
from jax import lax
from jax.experimental import pallas as pl
from jax.experimental.pallas import tpu as pltpu

D_MODEL = 1024
HEAD_DIM = 64
DIL_GROUPS = ((128, 1), (512, 4), (2048, 16))
DIL_HEADS_PER_GROUP = 4
N_DIL_HEADS = DIL_HEADS_PER_GROUP * len(DIL_GROUPS)
N_SB_HEADS = 8
DIL_WIDTH = N_DIL_HEADS * HEAD_DIM
DIL_OUT_WIDTH = DIL_HEADS_PER_GROUP * HEAD_DIM
SB_WIDTH = N_SB_HEADS * HEAD_DIM
D_FF = 4 * D_MODEL
BLOCK = 128
RMS_EPS = 1e-6
NEG_INF = -1e30
LOG2E = 1.4426950408889634
LANES = 128

VMEM_LIMIT = 56 * 1024 * 1024

BF16 = jnp.bfloat16
F32 = jnp.float32


def _dot(a, b):
    return jnp.dot(a, b, preferred_element_type=F32)


def _dot_nt(a, b):
    return lax.dot_general(a, b, (((1,), (1,)), ((), ())), preferred_element_type=F32)


def _rms(x, g):
    return x * lax.rsqrt(jnp.mean(x * x, axis=-1, keepdims=True) + RMS_EPS) * g


def _sigmoid(x):
    return 1.0 / (1.0 + jnp.exp(-x))


DIL_TILE = 256
PROJ_TILES = 2


def _residue_major(h_ref, d):
    rows = DIL_TILE // d
    return jnp.concatenate(
        [jnp.concatenate([h_ref[c, pl.ds(r, rows, stride=d), :] for r in range(d)], axis=0)
         for c in range(h_ref.shape[0])], axis=1)


def _proj_kernel(x_ref, g_ref, wd0, wd1, wd2, wqs, wks, wvs, wga, wgb, bga, bgb, head_sel,
                 d0_ref, d1_ref, d2_ref, qs_ref, ks_ref, vs_ref, ga_ref, gb_ref, kn_ref, *h_scr):
    tiles = range(PROJ_TILES)
    rows = [slice(t * DIL_TILE, (t + 1) * DIL_TILE) for t in tiles]
    h = []
    for t in tiles:
        h32 = _rms(x_ref[rows[t], :], g_ref[...])
        for c in range(h_scr[t].shape[0]):
            h_scr[t][c] = h32[:, c * LANES:(c + 1) * LANES]
        h.append(h32.astype(BF16))
    for (_, dilation), w_ref, o_ref in zip(DIL_GROUPS, (wd0, wd1, wd2), (d0_ref, d1_ref, d2_ref)):
        for t in tiles:
            hd = h[t] if dilation == 1 else _residue_major(h_scr[t], dilation).astype(BF16)
            o_ref[rows[t], :] = _dot(hd, w_ref[...]).astype(BF16)
    for t in tiles:
        qs_ref[rows[t], :] = (_dot(h[t], wqs[...]) * (LOG2E / math.sqrt(HEAD_DIM))).astype(BF16)
    k_norm2 = None
    for t in tiles:
        k = _dot(h[t], wks[...]).astype(BF16)
        ks_ref[rows[t], :] = k
        n2 = jnp.max(_dot(jnp.square(k.astype(F32)).astype(BF16), head_sel[...]), axis=0,
                     keepdims=True)
        k_norm2 = n2 if k_norm2 is None else jnp.maximum(k_norm2, n2)
    kn_ref[...] = jnp.broadcast_to(k_norm2, kn_ref.shape)
    for t in tiles:
        vs_ref[rows[t], :] = _dot(h[t], wvs[...]).astype(BF16)
    for t in tiles:
        ga_ref[rows[t], :] = _sigmoid(_dot(h[t], wga[...]) + bga[...])
    for t in tiles:
        gb_ref[rows[t], :] = _sigmoid(_dot(h[t], wgb[...]) + bgb[...])


def _proj(x, g, wd, wqs, wks, wvs, wga, wgb, bga, bgb):
    b, s, d = x.shape
    tm = PROJ_TILES * DIL_TILE
    tok = lambda w: pl.BlockSpec((None, tm, w), lambda bi, i: (bi, i, 0))
    full = lambda a: pl.BlockSpec(a.shape, lambda bi, i: (0,) * a.ndim,
                                  pipeline_mode=pl.Buffered(1))
    dil_width = 3 * DIL_OUT_WIDTH
    out_shape = (
        (jax.ShapeDtypeStruct((b, s, dil_width), BF16),) * len(DIL_GROUPS)
        + (jax.ShapeDtypeStruct((b, s, SB_WIDTH), BF16),) * 3
        + (jax.ShapeDtypeStruct((b, s, D_MODEL), F32),) * 2
        + (jax.ShapeDtypeStruct((b, s // tm, 8, LANES), F32),))
    out_specs = ((tok(dil_width),) * len(DIL_GROUPS) + (tok(SB_WIDTH),) * 3
                 + (tok(D_MODEL),) * 2
                 + (pl.BlockSpec((None, None, 8, LANES), lambda bi, i: (bi, i, 0, 0)),))
    head_sel = (jnp.arange(SB_WIDTH)[:, None] // HEAD_DIM == jnp.arange(LANES)[None, :]).astype(BF16)
    weights = (g, *wd, wqs, wks, wvs, wga, wgb, bga, bgb, head_sel)
    return pl.pallas_call(
        _proj_kernel,
        grid=(b, s // tm),
        in_specs=[tok(d)] + [full(a) for a in weights],
        out_specs=out_specs,
        out_shape=out_shape,
        scratch_shapes=[pltpu.VMEM((d // LANES, DIL_TILE, LANES), F32)] * PROJ_TILES,
        compiler_params=pltpu.CompilerParams(
            dimension_semantics=("parallel", "parallel"), vmem_limit_bytes=VMEM_LIMIT),
        name="proj",
    )(x, *weights)


DIL_MAX_QB = 8


def _alibi_slope(head):
    return 2.0 ** (-8.0 * (head + 1) / N_DIL_HEADS)


def _dil_kernel(q_ref, kp_ref, kc_ref, vp_ref, vc_ref, o_ref, lse_ref, *, group, dilation, n_steps,
                qb):
    n = pl.program_id(2)
    width = DIL_OUT_WIDTH
    qrows = qb * BLOCK
    n_res = q_ref.shape[1] if len(q_ref.shape) == 4 else 1
    res = range(n_res)
    take = lambda ref, r, n_rows: (ref[:, r] if len(ref.shape) == 4 else ref[...]).reshape(n_rows, width)
    q = [take(q_ref, r, qrows) for r in res]
    k = [jnp.concatenate([take(kp_ref, r, BLOCK), take(kc_ref, r, qrows)], axis=0) for r in res]
    v = [jnp.concatenate([take(vp_ref, r, BLOCK), take(vc_ref, r, qrows)], axis=0) for r in res]
    qi = lax.broadcasted_iota(jnp.int32, (BLOCK, 2 * BLOCK), 0)
    kj = lax.broadcasted_iota(jnp.int32, (BLOCK, 2 * BLOCK), 1)
    steps = qi + BLOCK - kj
    valid = (steps >= 0) & (steps <= n_steps)
    has_prev = (kj >= BLOCK) | (n > 0)
    dist = (steps * dilation).astype(F32)
    low = lax.broadcasted_iota(jnp.int32, (1, LANES), 1) < HEAD_DIM
    sel = (low, jnp.logical_not(low))
    ones = [jnp.broadcast_to(jnp.where(sel[e], 1.0, 0.0).astype(BF16), (2 * BLOCK, LANES))
            for e in range(2)]
    work = [(r, p, j, e) for r in res for p in range(2) for j in range(qb) for e in range(2)]
    q_rows = lambda j: slice(j * BLOCK, (j + 1) * BLOCK)
    k_rows = lambda j: slice(j * BLOCK, (j + 2) * BLOCK)
    pair = lambda t, p: t[:, p * LANES:(p + 1) * LANES]
    qm = {(r, p, e): jnp.where(sel[e], pair(q[r], p), jnp.zeros((qrows, LANES), BF16))
          for r in res for p in range(2) for e in range(2)}
    vm = {(r, p, e): jnp.where(sel[e], pair(v[r], p), jnp.zeros((BLOCK + qrows, LANES), BF16))
          for r in res for p in range(2) for e in range(2)}
    bias = {(p, e): jnp.where(valid, -_alibi_slope(group * DIL_HEADS_PER_GROUP + 2 * p + e) * dist,
                              NEG_INF) for p in range(2) for e in range(2)}
    logits = {}
    for r, p, j, e in work:
        lg = _dot_nt(qm[r, p, e][q_rows(j)], pair(k[r], p)[k_rows(j)]) + bias[p, e]
        logits[r, p, j, e] = jnp.where(has_prev, lg, NEG_INF) if j == 0 else lg
    ms = {w: jnp.max(logits[w], axis=-1, keepdims=True) for w in work}
    pr = {w: jnp.exp(logits[w] - ms[w]).astype(BF16) for w in work}
    t = {(r, p, j, e): _dot(pr[r, p, j, e],
                            jnp.concatenate([vm[r, p, e][k_rows(j)], ones[e]], axis=1))
         for r, p, j, e in work}
    for r in res:
        for j in range(qb):
            o_pairs, lse_pairs = [], []
            for p in range(2):
                acc = t[r, p, j, 0] + t[r, p, j, 1]
                num, den = acc[:, :LANES], acc[:, LANES:]
                o_pairs.append(num / den)
                lse_pairs.append(jnp.where(low, ms[r, p, j, 0], ms[r, p, j, 1]) + jnp.log(den))
            for ref, pairs in ((o_ref, o_pairs), (lse_ref, lse_pairs)):
                val = jnp.concatenate(pairs, axis=1)
                if len(ref.shape) == 4:
                    rows = ref.shape[2]
                    tiles = slice(j * BLOCK // rows, (j + 1) * BLOCK // rows)
                    ref[tiles, r] = val.reshape(BLOCK // rows, rows, width)
                else:
                    ref[j * BLOCK:(j + 1) * BLOCK, :] = val


def _dilated_group(qkv, group):
    window, dilation = DIL_GROUPS[group]
    b, s, _ = qkv.shape
    width = DIL_OUT_WIDTH
    qb = min(DIL_MAX_QB, s // dilation // BLOCK)
    n_res = min(dilation, DIL_MAX_QB // qb)
    qrows = qb * BLOCK
    n_grid = s // dilation // qrows
    first_prev = lambda n: jnp.maximum(n * qb - 1, 0)
    if dilation == 1:
        view = qkv
        cur = lambda c: pl.BlockSpec((None, qrows, width), lambda bi, r, n: (bi, n, c))
        prev = lambda c: pl.BlockSpec((None, BLOCK, width), lambda bi, r, n: (bi, first_prev(n), c))
        shp = jax.ShapeDtypeStruct((b, s, width), F32)
    else:
        rows = DIL_TILE // dilation
        view = qkv.reshape(b, s // DIL_TILE, dilation, rows, 3 * width)
        cur = lambda c: pl.BlockSpec((None, qrows // rows, n_res, rows, width),
                                     lambda bi, r, n: (bi, n, r, 0, c))
        prev = lambda c: pl.BlockSpec((None, BLOCK // rows, n_res, rows, width),
                                      lambda bi, r, n: (bi, first_prev(n), r, 0, c))
        shp = jax.ShapeDtypeStruct((b, s // DIL_TILE, dilation, rows, width), F32)
    return pl.pallas_call(
        functools.partial(_dil_kernel, group=group, dilation=dilation,
                          n_steps=window // dilation, qb=qb),
        grid=(b, dilation // n_res, n_grid),
        in_specs=[cur(0), prev(1), cur(1), prev(2), cur(2)],
        out_specs=(cur(0), cur(0)),
        out_shape=(shp, shp),
        compiler_params=pltpu.CompilerParams(
            dimension_semantics=("parallel", "parallel", "parallel"),
            vmem_limit_bytes=VMEM_LIMIT),
        name=f"dilated_d{dilation}",
    )(view, view, view, view, view)


SB_TQ = 64
SB_TK = 256
SB_WINDOW = 256
SB_DEAD = 136.0
SB_BLOCKS_PER_ITER = 10
HEADS_PER_STEP = LANES // HEAD_DIM


def _softplus2(z):
    return jnp.maximum(z, 0.0) + jnp.log2(1.0 + jnp.exp2(-jnp.abs(z)))


def _split_hi_lo(sp):
    parts = []
    for c in range(sp.shape[1] // LANES):
        sp_c = sp[:, c * LANES:(c + 1) * LANES]
        hi = sp_c.astype(BF16)
        parts += [hi, (sp_c - hi.astype(F32)).astype(BF16)]
    return jnp.concatenate(parts, axis=1)


def _sb_suffix_sums(hl, tri):
    return [_dot(hl[:, 2 * c * LANES:2 * (c + 1) * LANES], tri)
            for c in range(hl.shape[1] // (2 * LANES))]


def _sb_weights(z, sums, carry):
    las = [None] * len(sums)
    for c in reversed(range(len(sums))):
        las[c] = z[:, c * LANES:(c + 1) * LANES] - sums[c][:, :LANES] - carry
        carry = carry + sums[c][:, LANES:]
    return jnp.exp2(jnp.concatenate(las, axis=1)).astype(BF16), carry


def _sb_kernel(q_ref, k_ref, v_ref, tri_ref, kn_ref, o_ref, acc_ref, carry_ref):
    s = q_ref.shape[0]
    n_q = s // SB_TQ
    heads = range(HEADS_PER_STEP)
    tri = tri_ref[...]
    lane = lax.broadcasted_iota(jnp.int32, (1, LANES), 1)
    own = [(lane >= h * HEAD_DIM) & (lane < (h + 1) * HEAD_DIM) for h in heads]

    kn = jnp.max(kn_ref[...], axis=(0, 1), keepdims=True).reshape(1, LANES)
    k_max = [jnp.sqrt(jnp.max(jnp.where(lane == HEADS_PER_STEP * pl.program_id(1) + h, kn, 0.0)))
             * (1.0 + 2.0 ** -8) for h in heads]

    col_minus_row = (lax.broadcasted_iota(jnp.int32, (SB_TQ, SB_WINDOW), 1)
                     - lax.broadcasted_iota(jnp.int32, (SB_TQ, SB_WINDOW), 0))
    lead = SB_WINDOW - SB_TQ

    def query_blocks(blocks):
        items = [(slot, h) for slot in range(len(blocks)) for h in heads]
        rows, start, causal, q, z_max = {}, {}, {}, {}, {}
        for slot, qi in enumerate(blocks):
            first = qi * SB_TQ
            if isinstance(qi, int):
                start[slot] = max(first - lead, 0)
                rows[slot] = pl.ds(first, SB_TQ)
            else:
                start[slot] = pl.multiple_of(jnp.maximum(first - lead, 0), SB_TQ)
                rows[slot] = pl.ds(pl.multiple_of(first, SB_TQ), SB_TQ)
            causal[slot] = col_minus_row < (first - start[slot] if isinstance(qi, int) else lead)
            q_pair = q_ref[rows[slot], :]
            for h in heads:
                q[slot, h] = jnp.where(own[h], q_pair, jnp.zeros_like(q_pair))
                z_max[slot, h] = (jnp.sqrt(jnp.sum(jnp.square(q[slot, h].astype(F32)), axis=1,
                                                   keepdims=True)) * (k_max[h] * 1.001) + 1.0)
        z = {(slot, h): jnp.where(causal[slot],
                                  _dot_nt(q[slot, h], k_ref[pl.ds(start[slot], SB_WINDOW), :]),
                                  NEG_INF) for slot, h in items}
        hl = {w: _split_hi_lo(_softplus2(z[w])) for w in items}
        sums = {w: _sb_suffix_sums(hl[w], tri) for w in items}
        a, carry = {}, {}
        for w in items:
            a[w], carry[w] = _sb_weights(z[w], sums[w], jnp.zeros((SB_TQ, LANES), F32))
        for slot, h in items:
            acc_ref[slot, h] = _dot(a[slot, h], v_ref[pl.ds(start[slot], SB_WINDOW), :])
            carry_ref[slot, h] = carry[slot, h]

        def slack(slot, carries):
            per_head = [carries[h] - z_max[slot, h] for h in heads]
            return jnp.min(jnp.minimum(per_head[0], per_head[1]), keepdims=True)

        slacks = [slack(slot, [carry[slot, h] for h in heads]) for slot in range(len(blocks))]

        @pl.when(functools.reduce(jnp.minimum, slacks)[0, 0] < SB_DEAD)
        def _():
            for slot in range(len(blocks)):
                def cond(state):
                    ke, live = state
                    return (ke > 0) & live

                def body(state):
                    ke, _ = state
                    ks = pl.multiple_of(jnp.maximum(ke - SB_TK, 0), SB_TQ)
                    fresh = (lax.broadcasted_iota(jnp.int32, (SB_TQ, SB_TK), 1) + ks) < ke
                    carries = []
                    for h in heads:
                        zt = jnp.where(fresh, _dot_nt(q[slot, h], k_ref[pl.ds(ks, SB_TK), :]),
                                       NEG_INF)
                        at, ct = _sb_weights(
                            zt, _sb_suffix_sums(_split_hi_lo(_softplus2(zt)), tri),
                            carry_ref[slot, h])
                        acc_ref[slot, h] += _dot(at, v_ref[pl.ds(ks, SB_TK), :])
                        carry_ref[slot, h] = ct
                        carries.append(ct)
                    return ks, slack(slot, carries)[0, 0] < SB_DEAD

                lax.while_loop(cond, body, (jnp.asarray(start[slot], jnp.int32),
                                            slacks[slot][0, 0] < SB_DEAD))

        for slot in range(len(blocks)):
            o_ref[rows[slot], :] = jnp.where(own[0], acc_ref[slot, 0],
                                             acc_ref[slot, 1]).astype(o_ref.dtype)

    n_lead = lead // SB_TQ
    query_blocks(list(range(n_lead)))
    n_groups = (n_q - n_lead) // SB_BLOCKS_PER_ITER

    def group(i, _):
        query_blocks([n_lead + i * SB_BLOCKS_PER_ITER + j for j in range(SB_BLOCKS_PER_ITER)])
        return 0

    lax.fori_loop(0, n_groups, group, 0)
    rest = list(range(n_lead + n_groups * SB_BLOCKS_PER_ITER, n_q))
    if rest:
        query_blocks(rest)


def _sb_attention(qs, ks, vs, k_norm2):
    b, s, _ = qs.shape
    assert s % SB_TQ == 0 and s >= SB_WINDOW and SB_WINDOW % LANES == 0 and HEADS_PER_STEP == 2
    t = (jnp.arange(LANES)[:, None] >= jnp.arange(LANES)[None, :]).astype(BF16)
    half = jnp.concatenate([t, jnp.ones((LANES, LANES), BF16)], axis=1)
    tri = jnp.concatenate([half, half], axis=0)
    nh = HEADS_PER_STEP
    seq = pl.BlockSpec((None, s, LANES), lambda bi, p: (bi, 0, p))
    return pl.pallas_call(
        _sb_kernel,
        grid=(b, SB_WIDTH // LANES),
        in_specs=[seq, seq, seq, pl.BlockSpec(tri.shape, lambda bi, p: (0, 0)),
                  pl.BlockSpec((None,) + k_norm2.shape[1:], lambda bi, p: (bi, 0, 0, 0))],
        out_specs=seq,
        out_shape=jax.ShapeDtypeStruct((b, s, SB_WIDTH), BF16),
        scratch_shapes=[pltpu.VMEM((SB_BLOCKS_PER_ITER, nh, SB_TQ, LANES), F32)] * 2,
        compiler_params=pltpu.CompilerParams(
            dimension_semantics=("parallel", "parallel"), vmem_limit_bytes=VMEM_LIMIT),
        name="stick_breaking",
    )(qs, ks, vs, tri, k_norm2)


TAIL_TILES = 2
MLP_FF_CHUNK = 1024


def _token_major(ref, scr):
    d, rows, _ = ref.shape
    for c in range(scr.shape[0]):
        for r in range(d):
            scr[c, pl.ds(r, rows, stride=d), :] = ref[r, :, c * LANES:(c + 1) * LANES]
    return jnp.concatenate([scr[c] for c in range(scr.shape[0])], axis=1)


def _tail_kernel(x_ref, o0, o1, o2, l0, l1, l2, ob_ref, ga_ref, gb_ref,
                 wud, wus, wout, g_mlp, w1, w2, g_out, out_ref, *scratch, final_norm):
    tiles = range(TAIL_TILES)
    rows = [slice(t * DIL_TILE, (t + 1) * DIL_TILE) for t in tiles]
    o_a = []
    for t in tiles:
        scr = scratch[4 * t:4 * t + 4]
        outs = [o0[rows[t], :], _token_major(o1.at[t], scr[0]), _token_major(o2.at[t], scr[1])]
        lse = [l0[rows[t], :], _token_major(l1.at[t], scr[2]), _token_major(l2.at[t], scr[3])]
        m = jnp.maximum(jnp.maximum(lse[0], lse[1]), lse[2])
        e = [jnp.exp(l - m) for l in lse]
        o_a.append((e[0] * outs[0] + e[1] * outs[1] + e[2] * outs[2]) / (e[0] + e[1] + e[2]))
    up_a = [_dot(o_a[t].astype(BF16), wud[...]) for t in tiles]
    up_b = [_dot(ob_ref[rows[t], :], wus[...]) for t in tiles]
    merged = [ga_ref[rows[t], :] * up_a[t] + gb_ref[rows[t], :] * up_b[t] for t in tiles]
    y = [x_ref[rows[t], :] + _dot(merged[t].astype(BF16), wout[...]) for t in tiles]
    h2 = [_rms(y[t], g_mlp[...]).astype(BF16) for t in tiles]
    for c in range(D_FF // MLP_FF_CHUNK):
        cs = slice(c * MLP_FF_CHUNK, (c + 1) * MLP_FF_CHUNK)
        u = [jnp.maximum(_dot(h2[t], w1[:, cs]), 0.0) for t in tiles]
        y = [y[t] + _dot((u[t] * u[t]).astype(BF16), w2[cs, :]) for t in tiles]
    for t in tiles:
        out_ref[rows[t], :] = _rms(y[t], g_out[...]) if final_norm else y[t]


def _tail(x, oas, lses, o_b, g_a, g_b, wud, wus, wout, g_mlp, w1, w2, g_out, final_norm):
    b, s, d = x.shape
    tm = TAIL_TILES * DIL_TILE
    tok = lambda w: pl.BlockSpec((None, tm, w), lambda bi, i: (bi, i, 0))
    full = lambda a: pl.BlockSpec(a.shape, lambda bi, i: (0,) * a.ndim,
                                  pipeline_mode=pl.Buffered(1))

    def dil(a):
        if a.ndim == 3:
            return tok(DIL_OUT_WIDTH)
        return pl.BlockSpec((None, TAIL_TILES) + a.shape[2:], lambda bi, i: (bi, i, 0, 0, 0))

    weights = (wud, wus, wout, g_mlp, w1, w2, g_out)
    return pl.pallas_call(
        functools.partial(_tail_kernel, final_norm=final_norm),
        grid=(b, s // tm),
        in_specs=[tok(d)] + [dil(a) for a in (*oas, *lses)] + [tok(SB_WIDTH), tok(d), tok(d)]
                 + [full(a) for a in weights],
        out_specs=tok(d),
        out_shape=jax.ShapeDtypeStruct((b, s, d), F32),
        scratch_shapes=[pltpu.VMEM((DIL_OUT_WIDTH // LANES, DIL_TILE, LANES), F32)]
                       * (4 * TAIL_TILES),
        compiler_params=pltpu.CompilerParams(
            dimension_semantics=("parallel", "parallel"), vmem_limit_bytes=VMEM_LIMIT),
        name="tail",
    )(x, *oas, *lses, o_b, g_a, g_b, *weights)


def kernel(x, norm_mix_g, w_in, b_gate, w_up_dil, w_up_sb, w_out, norm_mlp_g, w_mlp_in,
           w_mlp_out, norm_final_g):
    depth = w_in.shape[0]
    row = lambda v: v.reshape(1, -1)
    for layer in range(depth):
        w = w_in[layer]
        c0 = 3 * DIL_WIDTH
        c1 = c0 + 3 * SB_WIDTH
        dil_cols = lambda i, grp: w[:, i * DIL_WIDTH + grp * DIL_OUT_WIDTH:
                                    i * DIL_WIDTH + (grp + 1) * DIL_OUT_WIDTH]
        wd = [jnp.concatenate([dil_cols(0, grp) * (1.0 / math.sqrt(HEAD_DIM)), dil_cols(1, grp),
                               dil_cols(2, grp)], axis=1).astype(BF16)
              for grp in range(len(DIL_GROUPS))]
        w = w.astype(BF16)
        wqs, wks, wvs = (w[:, c0 + i * SB_WIDTH:c0 + (i + 1) * SB_WIDTH] for i in range(3))
        wga, wgb = w[:, c1:c1 + D_MODEL], w[:, c1 + D_MODEL:]
        bga, bgb = row(b_gate[layer][:D_MODEL]), row(b_gate[layer][D_MODEL:])
        *qkv, qs, ks, vs, g_a, g_b, k_norm2 = _proj(
            x, row(norm_mix_g[layer]), wd, wqs, wks, wvs, wga, wgb, bga, bgb)
        dil = [_dilated_group(qkv[grp], grp) for grp in range(len(DIL_GROUPS))]
        o_b = _sb_attention(qs, ks, vs, k_norm2)
        x = _tail(x, [o for o, _ in dil], [l for _, l in dil], o_b, g_a, g_b,
                  w_up_dil[layer].astype(BF16), w_up_sb[layer].astype(BF16),
                  w_out[layer].astype(BF16), row(norm_mlp_g[layer]),
                  w_mlp_in[layer].astype(BF16), w_mlp_out[layer].astype(BF16),
                  row(norm_final_g), final_norm=layer == depth - 1)
    return x
```

```python
import functools
import math

import jax
import jax.numpy as jnp
import numpy as np
from jax import lax
from jax.experimental import pallas as pl
from jax.experimental.pallas import tpu as pltpu

D_MODEL = 1024
HEAD_DIM = 64
DIL_GROUPS = ((128, 1), (512, 4), (2048, 16))
DIL_HEADS_PER_GROUP = 4
N_DIL_HEADS = DIL_HEADS_PER_GROUP * len(DIL_GROUPS)
N_SB_HEADS = 8
DIL_WIDTH = N_DIL_HEADS * HEAD_DIM
DIL_OUT_WIDTH = DIL_HEADS_PER_GROUP * HEAD_DIM
SB_WIDTH = N_SB_HEADS * HEAD_DIM
D_FF = 4 * D_MODEL
BLOCK = 128
RMS_EPS = 1e-6
NEG_INF = -1e30
LOG2E = 1.4426950408889634
LANES = 128
SUBLANES = 8

VMEM_LIMIT = 56 * 1024 * 1024

BF16 = jnp.bfloat16
F32 = jnp.float32


def _dot(a, b):
    return jnp.dot(a, b, preferred_element_type=F32)


def _dot_nt(a, b):
    return lax.dot_general(a, b, (((1,), (1,)), ((), ())), preferred_element_type=F32)


def _rms(x, g):
    return x * lax.rsqrt(jnp.mean(x * x, axis=-1, keepdims=True) + RMS_EPS) * g


def _sigmoid(x):
    return 1.0 / (1.0 + jnp.exp(-x))


DIL_TILE = 256
PROJ_TILES = 2


def _residue_major(h_ref, d):
    rows = DIL_TILE // d
    return jnp.concatenate(
        [jnp.concatenate([h_ref[c, pl.ds(r, rows, stride=d), :] for r in range(d)], axis=0)
         for c in range(h_ref.shape[0])], axis=1)


def _proj_kernel(x_ref, g_ref, wd0, wd1, wd2, wqs, wks, wvs, wga, wgb, bga, bgb, head_sel,
                 d0_ref, d1_ref, d2_ref, qs_ref, ks_ref, vs_ref, ga_ref, gb_ref, kn_ref, *h_scr):
    tiles = range(PROJ_TILES)
    rows = [slice(t * DIL_TILE, (t + 1) * DIL_TILE) for t in tiles]
    h = []
    for t in tiles:
        h32 = _rms(x_ref[rows[t], :], g_ref[...])
        for c in range(h_scr[t].shape[0]):
            h_scr[t][c] = h32[:, c * LANES:(c + 1) * LANES]
        h.append(h32.astype(BF16))
    for (_, dilation), w_ref, o_ref in zip(DIL_GROUPS, (wd0, wd1, wd2), (d0_ref, d1_ref, d2_ref)):
        for t in tiles:
            hd = h[t] if dilation == 1 else _residue_major(h_scr[t], dilation).astype(BF16)
            o_ref[rows[t], :] = _dot(hd, w_ref[...]).astype(BF16)
    for t in tiles:
        qs_ref[rows[t], :] = (_dot(h[t], wqs[...]) * (LOG2E / math.sqrt(HEAD_DIM))).astype(BF16)
    k_norm2 = None
    for t in tiles:
        k = _dot(h[t], wks[...]).astype(BF16)
        ks_ref[rows[t], :] = k
        n2 = jnp.max(_dot(jnp.square(k.astype(F32)).astype(BF16), head_sel[...]), axis=0,
                     keepdims=True)
        k_norm2 = n2 if k_norm2 is None else jnp.maximum(k_norm2, n2)
    kn_ref[...] = jnp.broadcast_to(k_norm2, kn_ref.shape)
    for t in tiles:
        vs_ref[rows[t], :] = _dot(h[t], wvs[...]).astype(BF16)
    for t in tiles:
        ga_ref[rows[t], :] = _sigmoid(_dot(h[t], wga[...]) + bga[...])
    for t in tiles:
        gb_ref[rows[t], :] = _sigmoid(_dot(h[t], wgb[...]) + bgb[...])


def _proj(x, g, w_dil, w_sb, w_gate, b_gate):
    b, s, d = x.shape
    tm = PROJ_TILES * DIL_TILE
    tok = lambda w: pl.BlockSpec((None, tm, w), lambda bi, i: (bi, i, 0))
    const = lambda shape, col: pl.BlockSpec(shape, lambda bi, i: (0, col),
                                            pipeline_mode=pl.Buffered(1))
    cols = lambda a, n: [const((a.shape[0], a.shape[1] // n), c) for c in range(n)]
    dil_width = 3 * DIL_OUT_WIDTH
    out_shape = (
        (jax.ShapeDtypeStruct((b, s, dil_width), BF16),) * len(DIL_GROUPS)
        + (jax.ShapeDtypeStruct((b, s, SB_WIDTH), BF16),) * 3
        + (jax.ShapeDtypeStruct((b, s, D_MODEL), F32),) * 2
        + (jax.ShapeDtypeStruct((b, s // tm, SUBLANES, LANES), F32),))
    out_specs = ((tok(dil_width),) * len(DIL_GROUPS) + (tok(SB_WIDTH),) * 3
                 + (tok(D_MODEL),) * 2
                 + (pl.BlockSpec((None, None, SUBLANES, LANES), lambda bi, i: (bi, i, 0, 0)),))
    head_sel = jnp.asarray(np.arange(SB_WIDTH)[:, None] // HEAD_DIM == np.arange(LANES)[None, :],
                           BF16)
    return pl.pallas_call(
        _proj_kernel,
        grid=(b, s // tm),
        in_specs=([tok(d), const(g.shape, 0)] + cols(w_dil, len(DIL_GROUPS)) + cols(w_sb, 3)
                  + cols(w_gate, 2) + cols(b_gate, 2) + [const(head_sel.shape, 0)]),
        out_specs=out_specs,
        out_shape=out_shape,
        scratch_shapes=[pltpu.VMEM((d // LANES, DIL_TILE, LANES), F32)] * PROJ_TILES,
        compiler_params=pltpu.CompilerParams(
            dimension_semantics=("parallel", "parallel"), vmem_limit_bytes=VMEM_LIMIT),
        name="proj",
    )(x, g, *([w_dil] * len(DIL_GROUPS)), *([w_sb] * 3), *([w_gate] * 2), *([b_gate] * 2), head_sel)


DIL_MAX_QB = 8


def _alibi_slope(head):
    return 2.0 ** (-8.0 * (head + 1) / N_DIL_HEADS)


def _dil_kernel(q_ref, kp_ref, kc_ref, vp_ref, vc_ref, o_ref, lse_ref, *, group, dilation, n_steps,
                qb):
    n = pl.program_id(2)
    width = DIL_OUT_WIDTH
    qrows = qb * BLOCK
    n_res = q_ref.shape[1] if len(q_ref.shape) == 4 else 1
    res = range(n_res)
    take = lambda ref, r, n_rows: (ref[:, r] if len(ref.shape) == 4 else ref[...]).reshape(n_rows, width)
    q = [take(q_ref, r, qrows) for r in res]
    k = [jnp.concatenate([take(kp_ref, r, BLOCK), take(kc_ref, r, qrows)], axis=0) for r in res]
    v = [jnp.concatenate([take(vp_ref, r, BLOCK), take(vc_ref, r, qrows)], axis=0) for r in res]
    qi = lax.broadcasted_iota(jnp.int32, (BLOCK, 2 * BLOCK), 0)
    kj = lax.broadcasted_iota(jnp.int32, (BLOCK, 2 * BLOCK), 1)
    steps = qi + BLOCK - kj
    valid = (steps >= 0) & (steps <= n_steps)
    has_prev = (kj >= BLOCK) | (n > 0)
    dist = (steps * dilation).astype(F32)
    low = lax.broadcasted_iota(jnp.int32, (1, LANES), 1) < HEAD_DIM
    sel = (low, jnp.logical_not(low))
    ones = [jnp.broadcast_to(jnp.where(sel[e], 1.0, 0.0).astype(BF16), (2 * BLOCK, LANES))
            for e in range(2)]
    work = [(r, p, j, e) for r in res for p in range(2) for j in range(qb) for e in range(2)]
    q_rows = lambda j: slice(j * BLOCK, (j + 1) * BLOCK)
    k_rows = lambda j: slice(j * BLOCK, (j + 2) * BLOCK)
    pair = lambda t, p: t[:, p * LANES:(p + 1) * LANES]
    qm = {(r, p, e): jnp.where(sel[e], pair(q[r], p), jnp.zeros((qrows, LANES), BF16))
          for r in res for p in range(2) for e in range(2)}
    vm = {(r, p, e): jnp.where(sel[e], pair(v[r], p), jnp.zeros((BLOCK + qrows, LANES), BF16))
          for r in res for p in range(2) for e in range(2)}
    bias = {(p, e): jnp.where(valid, -_alibi_slope(group * DIL_HEADS_PER_GROUP + 2 * p + e) * dist,
                              NEG_INF) for p in range(2) for e in range(2)}
    logits = {}
    for r, p, j, e in work:
        lg = _dot_nt(qm[r, p, e][q_rows(j)], pair(k[r], p)[k_rows(j)]) + bias[p, e]
        logits[r, p, j, e] = jnp.where(has_prev, lg, NEG_INF) if j == 0 else lg
    ms = {w: jnp.max(logits[w], axis=-1, keepdims=True) for w in work}
    pr = {w: jnp.exp(logits[w] - ms[w]).astype(BF16) for w in work}
    t = {(r, p, j, e): _dot(pr[r, p, j, e],
                            jnp.concatenate([vm[r, p, e][k_rows(j)], ones[e]], axis=1))
         for r, p, j, e in work}
    for r in res:
        for j in range(qb):
            o_pairs, lse_pairs = [], []
            for p in range(2):
                acc = t[r, p, j, 0] + t[r, p, j, 1]
                num, den = acc[:, :LANES], acc[:, LANES:]
                o_pairs.append(num / den)
                lse_pairs.append(jnp.where(low, ms[r, p, j, 0], ms[r, p, j, 1]) + jnp.log(den))
            for ref, pairs in ((o_ref, o_pairs), (lse_ref, lse_pairs)):
                val = jnp.concatenate(pairs, axis=1)
                if len(ref.shape) == 4:
                    rows = ref.shape[2]
                    tiles = slice(j * BLOCK // rows, (j + 1) * BLOCK // rows)
                    ref[tiles, r] = val.reshape(BLOCK // rows, rows, width)
                else:
                    ref[j * BLOCK:(j + 1) * BLOCK, :] = val


def _dilated_group(qkv, group):
    window, dilation = DIL_GROUPS[group]
    b, s, _ = qkv.shape
    width = DIL_OUT_WIDTH
    qb = min(DIL_MAX_QB, s // dilation // BLOCK)
    n_res = min(dilation, DIL_MAX_QB // qb)
    qrows = qb * BLOCK
    n_grid = s // dilation // qrows
    first_prev = lambda n: jnp.maximum(n * qb - 1, 0)
    if dilation == 1:
        view = qkv
        cur = lambda c: pl.BlockSpec((None, qrows, width), lambda bi, r, n: (bi, n, c))
        prev = lambda c: pl.BlockSpec((None, BLOCK, width), lambda bi, r, n: (bi, first_prev(n), c))
        shp = jax.ShapeDtypeStruct((b, s, width), F32)
    else:
        rows = DIL_TILE // dilation
        view = qkv.reshape(b, s // DIL_TILE, dilation, rows, 3 * width)
        cur = lambda c: pl.BlockSpec((None, qrows // rows, n_res, rows, width),
                                     lambda bi, r, n: (bi, n, r, 0, c))
        prev = lambda c: pl.BlockSpec((None, BLOCK // rows, n_res, rows, width),
                                      lambda bi, r, n: (bi, first_prev(n), r, 0, c))
        shp = jax.ShapeDtypeStruct((b, s // DIL_TILE, dilation, rows, width), F32)
    return pl.pallas_call(
        functools.partial(_dil_kernel, group=group, dilation=dilation,
                          n_steps=window // dilation, qb=qb),
        grid=(b, dilation // n_res, n_grid),
        in_specs=[cur(0), prev(1), cur(1), prev(2), cur(2)],
        out_specs=(cur(0), cur(0)),
        out_shape=(shp, shp),
        compiler_params=pltpu.CompilerParams(
            dimension_semantics=("parallel", "parallel", "parallel"),
            vmem_limit_bytes=VMEM_LIMIT),
        name=f"dilated_d{dilation}",
    )(view, view, view, view, view)


SB_TQ = 64
SB_TK = 256
SB_WINDOW = 256
SB_DEAD = 136.0
SB_NORM_SLACK = 1.0 + 2.0 ** -8
SB_DOT_SLACK = 1.001
SB_BLOCKS_PER_ITER = 10
HEADS_PER_STEP = LANES // HEAD_DIM


def _softplus2(z):
    return jnp.maximum(z, 0.0) + jnp.log2(1.0 + jnp.exp2(-jnp.abs(z)))


def _split_hi_lo(sp):
    parts = []
    for c in range(sp.shape[1] // LANES):
        sp_c = sp[:, c * LANES:(c + 1) * LANES]
        hi = sp_c.astype(BF16)
        parts += [hi, (sp_c - hi.astype(F32)).astype(BF16)]
    return jnp.concatenate(parts, axis=1)


def _sb_suffix_sums(hl, tri):
    return [_dot(hl[:, 2 * c * LANES:2 * (c + 1) * LANES], tri)
            for c in range(hl.shape[1] // (2 * LANES))]


def _sb_weights(z, sums, carry):
    las = [None] * len(sums)
    for c in reversed(range(len(sums))):
        las[c] = z[:, c * LANES:(c + 1) * LANES] - sums[c][:, :LANES] - carry
        carry = carry + sums[c][:, LANES:]
    return jnp.exp2(jnp.concatenate(las, axis=1)).astype(BF16), carry


def _sb_kernel(q_ref, k_ref, v_ref, tri_ref, kn_ref, o_ref, acc_ref, carry_ref):
    s = q_ref.shape[0]
    n_q = s // SB_TQ
    heads = range(HEADS_PER_STEP)
    tri = tri_ref[...]
    lane = lax.broadcasted_iota(jnp.int32, (1, LANES), 1)
    own = [(lane >= h * HEAD_DIM) & (lane < (h + 1) * HEAD_DIM) for h in heads]

    kn = jnp.max(kn_ref[...], axis=(0, 1), keepdims=True).reshape(1, LANES)
    k_max = [jnp.sqrt(jnp.max(jnp.where(lane == HEADS_PER_STEP * pl.program_id(1) + h, kn, 0.0)))
             * SB_NORM_SLACK for h in heads]

    col_minus_row = (lax.broadcasted_iota(jnp.int32, (SB_TQ, SB_WINDOW), 1)
                     - lax.broadcasted_iota(jnp.int32, (SB_TQ, SB_WINDOW), 0))
    lead = SB_WINDOW - SB_TQ

    def query_blocks(blocks):
        items = [(slot, h) for slot in range(len(blocks)) for h in heads]
        rows, start, causal, q, z_max = {}, {}, {}, {}, {}
        for slot, qi in enumerate(blocks):
            first = qi * SB_TQ
            if isinstance(qi, int):
                start[slot] = max(first - lead, 0)
                rows[slot] = pl.ds(first, SB_TQ)
            else:
                start[slot] = pl.multiple_of(jnp.maximum(first - lead, 0), SB_TQ)
                rows[slot] = pl.ds(pl.multiple_of(first, SB_TQ), SB_TQ)
            causal[slot] = col_minus_row < (first - start[slot] if isinstance(qi, int) else lead)
            q_pair = q_ref[rows[slot], :]
            for h in heads:
                q[slot, h] = jnp.where(own[h], q_pair, jnp.zeros_like(q_pair))
                z_max[slot, h] = (jnp.sqrt(jnp.sum(jnp.square(q[slot, h].astype(F32)), axis=1,
                                                   keepdims=True)) * (k_max[h] * SB_DOT_SLACK) + 1.0)
        z = {(slot, h): jnp.where(causal[slot],
                                  _dot_nt(q[slot, h], k_ref[pl.ds(start[slot], SB_WINDOW), :]),
                                  NEG_INF) for slot, h in items}
        hl = {w: _split_hi_lo(_softplus2(z[w])) for w in items}
        sums = {w: _sb_suffix_sums(hl[w], tri) for w in items}
        a, carry = {}, {}
        for w in items:
            a[w], carry[w] = _sb_weights(z[w], sums[w], jnp.zeros((SB_TQ, LANES), F32))
        for slot, h in items:
            acc_ref[slot, h] = _dot(a[slot, h], v_ref[pl.ds(start[slot], SB_WINDOW), :])
            carry_ref[slot, h] = carry[slot, h]

        def slack(slot, carries):
            per_head = [carries[h] - z_max[slot, h] for h in heads]
            return jnp.min(jnp.minimum(per_head[0], per_head[1]), keepdims=True)

        slacks = [slack(slot, [carry[slot, h] for h in heads]) for slot in range(len(blocks))]

        @pl.when(functools.reduce(jnp.minimum, slacks)[0, 0] < SB_DEAD)
        def _():
            for slot in range(len(blocks)):
                def cond(state):
                    ke, live = state
                    return (ke > 0) & live

                def body(state):
                    ke, _ = state
                    ks = pl.multiple_of(jnp.maximum(ke - SB_TK, 0), SB_TQ)
                    fresh = (lax.broadcasted_iota(jnp.int32, (SB_TQ, SB_TK), 1) + ks) < ke
                    carries = []
                    for h in heads:
                        zt = jnp.where(fresh, _dot_nt(q[slot, h], k_ref[pl.ds(ks, SB_TK), :]),
                                       NEG_INF)
                        at, ct = _sb_weights(
                            zt, _sb_suffix_sums(_split_hi_lo(_softplus2(zt)), tri),
                            carry_ref[slot, h])
                        acc_ref[slot, h] += _dot(at, v_ref[pl.ds(ks, SB_TK), :])
                        carry_ref[slot, h] = ct
                        carries.append(ct)
                    return ks, slack(slot, carries)[0, 0] < SB_DEAD

                lax.while_loop(cond, body, (jnp.asarray(start[slot], jnp.int32),
                                            slacks[slot][0, 0] < SB_DEAD))

        for slot in range(len(blocks)):
            o_ref[rows[slot], :] = jnp.where(own[0], acc_ref[slot, 0],
                                             acc_ref[slot, 1]).astype(o_ref.dtype)

    n_lead = lead // SB_TQ
    query_blocks(list(range(n_lead)))
    n_groups = (n_q - n_lead) // SB_BLOCKS_PER_ITER

    def group(i, _):
        query_blocks([n_lead + i * SB_BLOCKS_PER_ITER + j for j in range(SB_BLOCKS_PER_ITER)])
        return 0

    lax.fori_loop(0, n_groups, group, 0)
    rest = list(range(n_lead + n_groups * SB_BLOCKS_PER_ITER, n_q))
    if rest:
        query_blocks(rest)


def _sb_attention(qs, ks, vs, k_norm2):
    b, s, _ = qs.shape
    assert s % SB_TQ == 0 and s >= SB_WINDOW and SB_WINDOW % LANES == 0 and HEADS_PER_STEP == 2
    half = np.concatenate([np.arange(LANES)[:, None] >= np.arange(LANES)[None, :],
                           np.ones((LANES, LANES), bool)], axis=1)
    tri = jnp.asarray(np.concatenate([half, half], axis=0), BF16)
    nh = HEADS_PER_STEP
    seq = pl.BlockSpec((None, s, LANES), lambda bi, p: (bi, 0, p))
    return pl.pallas_call(
        _sb_kernel,
        grid=(b, SB_WIDTH // LANES),
        in_specs=[seq, seq, seq, pl.BlockSpec(tri.shape, lambda bi, p: (0, 0)),
                  pl.BlockSpec((None,) + k_norm2.shape[1:], lambda bi, p: (bi, 0, 0, 0))],
        out_specs=seq,
        out_shape=jax.ShapeDtypeStruct((b, s, SB_WIDTH), BF16),
        scratch_shapes=[pltpu.VMEM((SB_BLOCKS_PER_ITER, nh, SB_TQ, LANES), F32)] * 2,
        compiler_params=pltpu.CompilerParams(
            dimension_semantics=("parallel", "parallel"), vmem_limit_bytes=VMEM_LIMIT),
        name="stick_breaking",
    )(qs, ks, vs, tri, k_norm2)


TAIL_TILES = 2
MLP_FF_CHUNK = 1024


def _token_major(ref, scr):
    d, rows, _ = ref.shape
    for c in range(scr.shape[0]):
        for r in range(d):
            scr[c, pl.ds(r, rows, stride=d), :] = ref[r, :, c * LANES:(c + 1) * LANES]
    return jnp.concatenate([scr[c] for c in range(scr.shape[0])], axis=1)


def _tail_kernel(x_ref, o0, o1, o2, l0, l1, l2, ob_ref, ga_ref, gb_ref,
                 wud, wus, wout, g_mlp, w1, w2, g_out, out_ref, *scratch, final_norm):
    tiles = range(TAIL_TILES)
    rows = [slice(t * DIL_TILE, (t + 1) * DIL_TILE) for t in tiles]
    o_a = []
    for t in tiles:
        scr = scratch[4 * t:4 * t + 4]
        outs = [o0[rows[t], :], _token_major(o1.at[t], scr[0]), _token_major(o2.at[t], scr[1])]
        lse = [l0[rows[t], :], _token_major(l1.at[t], scr[2]), _token_major(l2.at[t], scr[3])]
        m = jnp.maximum(jnp.maximum(lse[0], lse[1]), lse[2])
        e = [jnp.exp(l - m) for l in lse]
        o_a.append((e[0] * outs[0] + e[1] * outs[1] + e[2] * outs[2]) / (e[0] + e[1] + e[2]))
    up_a = [_dot(o_a[t].astype(BF16), wud[...]) for t in tiles]
    up_b = [_dot(ob_ref[rows[t], :], wus[...]) for t in tiles]
    merged = [ga_ref[rows[t], :] * up_a[t] + gb_ref[rows[t], :] * up_b[t] for t in tiles]
    y = [x_ref[rows[t], :] + _dot(merged[t].astype(BF16), wout[...]) for t in tiles]
    h2 = [_rms(y[t], g_mlp[...]).astype(BF16) for t in tiles]
    for c in range(D_FF // MLP_FF_CHUNK):
        cs = slice(c * MLP_FF_CHUNK, (c + 1) * MLP_FF_CHUNK)
        u = [jnp.maximum(_dot(h2[t], w1[:, cs]), 0.0) for t in tiles]
        y = [y[t] + _dot((u[t] * u[t]).astype(BF16), w2[cs, :]) for t in tiles]
    for t in tiles:
        out_ref[rows[t], :] = _rms(y[t], g_out[...]) if final_norm else y[t]


def _tail(x, oas, lses, o_b, g_a, g_b, wud, wus, wout, g_mlp, w1, w2, g_out, final_norm):
    b, s, d = x.shape
    tm = TAIL_TILES * DIL_TILE
    tok = lambda w: pl.BlockSpec((None, tm, w), lambda bi, i: (bi, i, 0))
    full = lambda a: pl.BlockSpec(a.shape, lambda bi, i: (0,) * a.ndim,
                                  pipeline_mode=pl.Buffered(1))

    def dil(a):
        if a.ndim == 3:
            return tok(DIL_OUT_WIDTH)
        return pl.BlockSpec((None, TAIL_TILES) + a.shape[2:], lambda bi, i: (bi, i, 0, 0, 0))

    weights = (wud, wus, wout, g_mlp, w1, w2, g_out)
    return pl.pallas_call(
        functools.partial(_tail_kernel, final_norm=final_norm),
        grid=(b, s // tm),
        in_specs=[tok(d)] + [dil(a) for a in (*oas, *lses)] + [tok(SB_WIDTH), tok(d), tok(d)]
                 + [full(a) for a in weights],
        out_specs=tok(d),
        out_shape=jax.ShapeDtypeStruct((b, s, d), F32),
        scratch_shapes=[pltpu.VMEM((DIL_OUT_WIDTH // LANES, DIL_TILE, LANES), F32)]
                       * (4 * TAIL_TILES),
        compiler_params=pltpu.CompilerParams(
            dimension_semantics=("parallel", "parallel"), vmem_limit_bytes=VMEM_LIMIT),
        name="tail",
    )(x, *oas, *lses, o_b, g_a, g_b, *weights)


def kernel(x, norm_mix_g, w_in, b_gate, w_up_dil, w_up_sb, w_out, norm_mlp_g, w_mlp_in,
           w_mlp_out, norm_final_g):
    depth = w_in.shape[0]
    row = lambda v: v.reshape(1, -1)
    for layer in range(depth):
        w = w_in[layer]
        c0 = 3 * DIL_WIDTH
        c1 = c0 + 3 * SB_WIDTH
        dil_cols = lambda i, grp: w[:, i * DIL_WIDTH + grp * DIL_OUT_WIDTH:
                                    i * DIL_WIDTH + (grp + 1) * DIL_OUT_WIDTH]
        w_dil = jnp.concatenate(
            [part for grp in range(len(DIL_GROUPS))
             for part in (dil_cols(0, grp) * (1.0 / math.sqrt(HEAD_DIM)), dil_cols(1, grp),
                          dil_cols(2, grp))], axis=1).astype(BF16)
        *qkv, qs, ks, vs, g_a, g_b, k_norm2 = _proj(
            x, row(norm_mix_g[layer]), w_dil, w[:, c0:c1].astype(BF16), w[:, c1:].astype(BF16),
            row(b_gate[layer]))
        dil = [_dilated_group(qkv[grp], grp) for grp in range(len(DIL_GROUPS))]
        o_b = _sb_attention(qs, ks, vs, k_norm2)
        x = _tail(x, [o for o, _ in dil], [l for _, l in dil], o_b, g_a, g_b,
                  w_up_dil[layer].astype(BF16), w_up_sb[layer].astype(BF16),
                  w_out[layer].astype(BF16), row(norm_mlp_g[layer]),
                  w_mlp_in[layer].astype(BF16), w_mlp_out[layer].astype(BF16),
                  row(norm_final_g), final_norm=layer == depth - 1)
    return x
```

```python
import functools
import math

import jax
import jax.numpy as jnp
import numpy as np
from jax import lax
from jax.experimental import pallas as pl
from jax.experimental.pallas import tpu as pltpu

D_MODEL = 1024
HEAD_DIM = 64
DIL_GROUPS = ((128, 1), (512, 4), (2048, 16))
DIL_HEADS_PER_GROUP = 4
N_DIL_HEADS = DIL_HEADS_PER_GROUP * len(DIL_GROUPS)
N_SB_HEADS = 8
DIL_WIDTH = N_DIL_HEADS * HEAD_DIM
DIL_OUT_WIDTH = DIL_HEADS_PER_GROUP * HEAD_DIM
SB_WIDTH = N_SB_HEADS * HEAD_DIM
D_FF = 4 * D_MODEL
BLOCK = 128
RMS_EPS = 1e-6
NEG_INF = -1e30
LOG2E = 1.4426950408889634
LANES = 128
SUBLANES = 8

VMEM_LIMIT = 56 * 1024 * 1024

BF16 = jnp.bfloat16
F32 = jnp.float32


def _dot(a, b):
    return jnp.dot(a, b, preferred_element_type=F32)


def _dot_nt(a, b):
    return lax.dot_general(a, b, (((1,), (1,)), ((), ())), preferred_element_type=F32)


def _rms(x, g):
    return x * lax.rsqrt(jnp.mean(x * x, axis=-1, keepdims=True) + RMS_EPS) * g


def _sigmoid(x):
    return 1.0 / (1.0 + jnp.exp(-x))


DIL_TILE = 256
PROJ_TILES = 2


def _residue_major(h_ref, d):
    rows = DIL_TILE // d
    return jnp.concatenate(
        [jnp.concatenate([h_ref[c, pl.ds(r, rows, stride=d), :] for r in range(d)], axis=0)
         for c in range(h_ref.shape[0])], axis=1)


def _proj_kernel(x_ref, g_ref, wd0, wd1, wd2, wqs, wks, wvs, wga, wgb, bga, bgb, head_sel,
                 d0_ref, d1_ref, d2_ref, qs_ref, ks_ref, vs_ref, ga_ref, gb_ref, kn_ref, *h_scr):
    tiles = range(PROJ_TILES)
    rows = [slice(t * DIL_TILE, (t + 1) * DIL_TILE) for t in tiles]
    h = []
    for t in tiles:
        h32 = _rms(x_ref[rows[t], :], g_ref[...])
        for c in range(h_scr[t].shape[0]):
            h_scr[t][c] = h32[:, c * LANES:(c + 1) * LANES]
        h.append(h32.astype(BF16))
    for (_, dilation), w_ref, o_ref in zip(DIL_GROUPS, (wd0, wd1, wd2), (d0_ref, d1_ref, d2_ref)):
        for t in tiles:
            hd = h[t] if dilation == 1 else _residue_major(h_scr[t], dilation).astype(BF16)
            o_ref[rows[t], :] = _dot(hd, w_ref[...]).astype(BF16)
    for t in tiles:
        qs_ref[rows[t], :] = (_dot(h[t], wqs[...]) * (LOG2E / math.sqrt(HEAD_DIM))).astype(BF16)
    k_norm2 = None
    for t in tiles:
        k = _dot(h[t], wks[...]).astype(BF16)
        ks_ref[rows[t], :] = k
        n2 = jnp.max(_dot(jnp.square(k.astype(F32)).astype(BF16), head_sel[...]), axis=0,
                     keepdims=True)
        k_norm2 = n2 if k_norm2 is None else jnp.maximum(k_norm2, n2)
    kn_ref[...] = jnp.broadcast_to(k_norm2, kn_ref.shape)
    for t in tiles:
        vs_ref[rows[t], :] = _dot(h[t], wvs[...]).astype(BF16)
    for t in tiles:
        ga_ref[rows[t], :] = _sigmoid(_dot(h[t], wga[...]) + bga[...])
    for t in tiles:
        gb_ref[rows[t], :] = _sigmoid(_dot(h[t], wgb[...]) + bgb[...])


def _proj(x, g, w_dil, w_sb, w_gate, b_gate):
    b, s, d = x.shape
    tm = PROJ_TILES * DIL_TILE
    tok = lambda w: pl.BlockSpec((None, tm, w), lambda bi, i: (bi, i, 0))
    const = lambda shape, col: pl.BlockSpec(shape, lambda bi, i: (0, col),
                                            pipeline_mode=pl.Buffered(1))
    cols = lambda a, n: [const((a.shape[0], a.shape[1] // n), c) for c in range(n)]
    dil_width = 3 * DIL_OUT_WIDTH
    out_shape = (
        (jax.ShapeDtypeStruct((b, s, dil_width), BF16),) * len(DIL_GROUPS)
        + (jax.ShapeDtypeStruct((b, s, SB_WIDTH), BF16),) * 3
        + (jax.ShapeDtypeStruct((b, s, D_MODEL), F32),) * 2
        + (jax.ShapeDtypeStruct((b, s // tm, SUBLANES, LANES), F32),))
    out_specs = ((tok(dil_width),) * len(DIL_GROUPS) + (tok(SB_WIDTH),) * 3
                 + (tok(D_MODEL),) * 2
                 + (pl.BlockSpec((None, None, SUBLANES, LANES), lambda bi, i: (bi, i, 0, 0)),))
    head_sel = jnp.asarray(np.arange(SB_WIDTH)[:, None] // HEAD_DIM == np.arange(LANES)[None, :],
                           BF16)
    return pl.pallas_call(
        _proj_kernel,
        grid=(b, s // tm),
        in_specs=([tok(d), const(g.shape, 0)] + cols(w_dil, len(DIL_GROUPS)) + cols(w_sb, 3)
                  + cols(w_gate, 2) + cols(b_gate, 2) + [const(head_sel.shape, 0)]),
        out_specs=out_specs,
        out_shape=out_shape,
        scratch_shapes=[pltpu.VMEM((d // LANES, DIL_TILE, LANES), F32)] * PROJ_TILES,
        compiler_params=pltpu.CompilerParams(
            dimension_semantics=("parallel", "parallel"), vmem_limit_bytes=VMEM_LIMIT),
        name="proj",
    )(x, g, *([w_dil] * len(DIL_GROUPS)), *([w_sb] * 3), *([w_gate] * 2), *([b_gate] * 2), head_sel)


DIL_MAX_QB = 16


def _alibi_slope(head):
    return 2.0 ** (-8.0 * (head + 1) / N_DIL_HEADS)


def _dil_kernel(q_ref, kp_ref, kc_ref, vp_ref, vc_ref, o_ref, lse_ref, *, group, dilation, n_steps,
                qb):
    n = pl.program_id(2)
    width = DIL_OUT_WIDTH
    qrows = qb * BLOCK
    n_res = q_ref.shape[1] if len(q_ref.shape) == 4 else 1
    res = range(n_res)
    take = lambda ref, r, n_rows: (ref[:, r] if len(ref.shape) == 4 else ref[...]).reshape(n_rows, width)
    q = [take(q_ref, r, qrows) for r in res]
    k = [jnp.concatenate([take(kp_ref, r, BLOCK), take(kc_ref, r, qrows)], axis=0) for r in res]
    v = [jnp.concatenate([take(vp_ref, r, BLOCK), take(vc_ref, r, qrows)], axis=0) for r in res]
    qi = lax.broadcasted_iota(jnp.int32, (BLOCK, 2 * BLOCK), 0)
    kj = lax.broadcasted_iota(jnp.int32, (BLOCK, 2 * BLOCK), 1)
    steps = qi + BLOCK - kj
    valid = (steps >= 0) & (steps <= n_steps)
    has_prev = (kj >= BLOCK) | (n > 0)
    dist = (steps * dilation).astype(F32)
    low = lax.broadcasted_iota(jnp.int32, (1, LANES), 1) < HEAD_DIM
    sel = (low, jnp.logical_not(low))
    ones = [jnp.broadcast_to(jnp.where(sel[e], 1.0, 0.0).astype(BF16), (2 * BLOCK, LANES))
            for e in range(2)]
    work = [(r, p, j, e) for r in res for p in range(2) for j in range(qb) for e in range(2)]
    q_rows = lambda j: slice(j * BLOCK, (j + 1) * BLOCK)
    k_rows = lambda j: slice(j * BLOCK, (j + 2) * BLOCK)
    pair = lambda t, p: t[:, p * LANES:(p + 1) * LANES]
    qm = {(r, p, e): jnp.where(sel[e], pair(q[r], p), jnp.zeros((qrows, LANES), BF16))
          for r in res for p in range(2) for e in range(2)}
    vm = {(r, p, e): jnp.where(sel[e], pair(v[r], p), jnp.zeros((BLOCK + qrows, LANES), BF16))
          for r in res for p in range(2) for e in range(2)}
    bias = {(p, e): jnp.where(valid, -_alibi_slope(group * DIL_HEADS_PER_GROUP + 2 * p + e) * dist,
                              NEG_INF) for p in range(2) for e in range(2)}
    logits = {}
    for r, p, j, e in work:
        lg = _dot_nt(qm[r, p, e][q_rows(j)], pair(k[r], p)[k_rows(j)]) + bias[p, e]
        logits[r, p, j, e] = jnp.where(has_prev, lg, NEG_INF) if j == 0 else lg
    ms = {w: jnp.max(logits[w], axis=-1, keepdims=True) for w in work}
    pr = {w: jnp.exp(logits[w] - ms[w]).astype(BF16) for w in work}
    t = {(r, p, j, e): _dot(pr[r, p, j, e],
                            jnp.concatenate([vm[r, p, e][k_rows(j)], ones[e]], axis=1))
         for r, p, j, e in work}
    for r in res:
        for j in range(qb):
            o_pairs, lse_pairs = [], []
            for p in range(2):
                acc = t[r, p, j, 0] + t[r, p, j, 1]
                num, den = acc[:, :LANES], acc[:, LANES:]
                o_pairs.append(num / den)
                lse_pairs.append(jnp.where(low, ms[r, p, j, 0], ms[r, p, j, 1]) + jnp.log(den))
            for ref, pairs in ((o_ref, o_pairs), (lse_ref, lse_pairs)):
                val = jnp.concatenate(pairs, axis=1)
                if len(ref.shape) == 4:
                    rows = ref.shape[2]
                    tiles = slice(j * BLOCK // rows, (j + 1) * BLOCK // rows)
                    ref[tiles, r] = val.reshape(BLOCK // rows, rows, width)
                else:
                    ref[j * BLOCK:(j + 1) * BLOCK, :] = val


def _dilated_group(qkv, group):
    window, dilation = DIL_GROUPS[group]
    b, s, _ = qkv.shape
    width = DIL_OUT_WIDTH
    qb = min(DIL_MAX_QB, s // dilation // BLOCK)
    n_res = min(dilation, DIL_MAX_QB // qb)
    qrows = qb * BLOCK
    n_grid = s // dilation // qrows
    first_prev = lambda n: jnp.maximum(n * qb - 1, 0)
    if dilation == 1:
        view = qkv
        cur = lambda c: pl.BlockSpec((None, qrows, width), lambda bi, r, n: (bi, n, c))
        prev = lambda c: pl.BlockSpec((None, BLOCK, width), lambda bi, r, n: (bi, first_prev(n), c))
        shp = jax.ShapeDtypeStruct((b, s, width), F32)
    else:
        rows = DIL_TILE // dilation
        view = qkv.reshape(b, s // DIL_TILE, dilation, rows, 3 * width)
        cur = lambda c: pl.BlockSpec((None, qrows // rows, n_res, rows, width),
                                     lambda bi, r, n: (bi, n, r, 0, c))
        prev = lambda c: pl.BlockSpec((None, BLOCK // rows, n_res, rows, width),
                                      lambda bi, r, n: (bi, first_prev(n), r, 0, c))
        shp = jax.ShapeDtypeStruct((b, s // DIL_TILE, dilation, rows, width), F32)
    return pl.pallas_call(
        functools.partial(_dil_kernel, group=group, dilation=dilation,
                          n_steps=window // dilation, qb=qb),
        grid=(b, dilation // n_res, n_grid),
        in_specs=[cur(0), prev(1), cur(1), prev(2), cur(2)],
        out_specs=(cur(0), cur(0)),
        out_shape=(shp, shp),
        compiler_params=pltpu.CompilerParams(
            dimension_semantics=("parallel", "parallel", "parallel"),
            vmem_limit_bytes=VMEM_LIMIT),
        name=f"dilated_d{dilation}",
    )(view, view, view, view, view)


SB_TQ = 64
SB_TK = 256
SB_WINDOW = 256
SB_DEAD = 136.0
SB_NORM_SLACK = 1.0 + 2.0 ** -8
SB_DOT_SLACK = 1.001
SB_BLOCKS_PER_ITER = 10
HEADS_PER_STEP = LANES // HEAD_DIM


def _softplus2(z):
    return jnp.maximum(z, 0.0) + jnp.log2(1.0 + jnp.exp2(-jnp.abs(z)))


def _split_hi_lo(sp):
    parts = []
    for c in range(sp.shape[1] // LANES):
        sp_c = sp[:, c * LANES:(c + 1) * LANES]
        hi = sp_c.astype(BF16)
        parts += [hi, (sp_c - hi.astype(F32)).astype(BF16)]
    return jnp.concatenate(parts, axis=1)


def _sb_suffix_sums(hl, tri):
    return [_dot(hl[:, 2 * c * LANES:2 * (c + 1) * LANES], tri)
            for c in range(hl.shape[1] // (2 * LANES))]


def _sb_weights(z, sums, carry):
    las = [None] * len(sums)
    for c in reversed(range(len(sums))):
        las[c] = z[:, c * LANES:(c + 1) * LANES] - sums[c][:, :LANES] - carry
        carry = carry + sums[c][:, LANES:]
    return jnp.exp2(jnp.concatenate(las, axis=1)).astype(BF16), carry


def _sb_kernel(q_ref, k_ref, v_ref, tri_ref, kn_ref, o_ref, acc_ref, carry_ref):
    s = q_ref.shape[0]
    n_q = s // SB_TQ
    heads = range(HEADS_PER_STEP)
    tri = tri_ref[...]
    lane = lax.broadcasted_iota(jnp.int32, (1, LANES), 1)
    own = [(lane >= h * HEAD_DIM) & (lane < (h + 1) * HEAD_DIM) for h in heads]

    kn = jnp.max(kn_ref[...], axis=(0, 1), keepdims=True).reshape(1, LANES)
    k_max = [jnp.sqrt(jnp.max(jnp.where(lane == HEADS_PER_STEP * pl.program_id(1) + h, kn, 0.0)))
             * SB_NORM_SLACK for h in heads]

    col_minus_row = (lax.broadcasted_iota(jnp.int32, (SB_TQ, SB_WINDOW), 1)
                     - lax.broadcasted_iota(jnp.int32, (SB_TQ, SB_WINDOW), 0))
    lead = SB_WINDOW - SB_TQ

    def query_blocks(blocks):
        items = [(slot, h) for slot in range(len(blocks)) for h in heads]
        rows, start, causal, q, z_max = {}, {}, {}, {}, {}
        for slot, qi in enumerate(blocks):
            first = qi * SB_TQ
            if isinstance(qi, int):
                start[slot] = max(first - lead, 0)
                rows[slot] = pl.ds(first, SB_TQ)
            else:
                start[slot] = pl.multiple_of(jnp.maximum(first - lead, 0), SB_TQ)
                rows[slot] = pl.ds(pl.multiple_of(first, SB_TQ), SB_TQ)
            causal[slot] = col_minus_row < (first - start[slot] if isinstance(qi, int) else lead)
            q_pair = q_ref[rows[slot], :]
            for h in heads:
                q[slot, h] = jnp.where(own[h], q_pair, jnp.zeros_like(q_pair))
                z_max[slot, h] = (jnp.sqrt(jnp.sum(jnp.square(q[slot, h].astype(F32)), axis=1,
                                                   keepdims=True)) * (k_max[h] * SB_DOT_SLACK) + 1.0)
        z = {(slot, h): jnp.where(causal[slot],
                                  _dot_nt(q[slot, h], k_ref[pl.ds(start[slot], SB_WINDOW), :]),
                                  NEG_INF) for slot, h in items}
        hl = {w: _split_hi_lo(_softplus2(z[w])) for w in items}
        sums = {w: _sb_suffix_sums(hl[w], tri) for w in items}
        a, carry = {}, {}
        for w in items:
            a[w], carry[w] = _sb_weights(z[w], sums[w], jnp.zeros((SB_TQ, LANES), F32))
        for slot, h in items:
            acc_ref[slot, h] = _dot(a[slot, h], v_ref[pl.ds(start[slot], SB_WINDOW), :])
            carry_ref[slot, h] = carry[slot, h]

        def slack(slot, carries):
            per_head = [carries[h] - z_max[slot, h] for h in heads]
            return jnp.min(jnp.minimum(per_head[0], per_head[1]), keepdims=True)

        slacks = [slack(slot, [carry[slot, h] for h in heads]) for slot in range(len(blocks))]

        @pl.when(functools.reduce(jnp.minimum, slacks)[0, 0] < SB_DEAD)
        def _():
            for slot in range(len(blocks)):
                def cond(state):
                    ke, live = state
                    return (ke > 0) & live

                def body(state):
                    ke, _ = state
                    ks = pl.multiple_of(jnp.maximum(ke - SB_TK, 0), SB_TQ)
                    fresh = (lax.broadcasted_iota(jnp.int32, (SB_TQ, SB_TK), 1) + ks) < ke
                    carries = []
                    for h in heads:
                        zt = jnp.where(fresh, _dot_nt(q[slot, h], k_ref[pl.ds(ks, SB_TK), :]),
                                       NEG_INF)
                        at, ct = _sb_weights(
                            zt, _sb_suffix_sums(_split_hi_lo(_softplus2(zt)), tri),
                            carry_ref[slot, h])
                        acc_ref[slot, h] += _dot(at, v_ref[pl.ds(ks, SB_TK), :])
                        carry_ref[slot, h] = ct
                        carries.append(ct)
                    return ks, slack(slot, carries)[0, 0] < SB_DEAD

                lax.while_loop(cond, body, (jnp.asarray(start[slot], jnp.int32),
                                            slacks[slot][0, 0] < SB_DEAD))

        for slot in range(len(blocks)):
            o_ref[rows[slot], :] = jnp.where(own[0], acc_ref[slot, 0],
                                             acc_ref[slot, 1]).astype(o_ref.dtype)

    n_lead = lead // SB_TQ
    query_blocks(list(range(n_lead)))
    n_groups = (n_q - n_lead) // SB_BLOCKS_PER_ITER

    def group(i, _):
        query_blocks([n_lead + i * SB_BLOCKS_PER_ITER + j for j in range(SB_BLOCKS_PER_ITER)])
        return 0

    lax.fori_loop(0, n_groups, group, 0)
    rest = list(range(n_lead + n_groups * SB_BLOCKS_PER_ITER, n_q))
    if rest:
        query_blocks(rest)


def _sb_attention(qs, ks, vs, k_norm2):
    b, s, _ = qs.shape
    assert s % SB_TQ == 0 and s >= SB_WINDOW and SB_WINDOW % LANES == 0 and HEADS_PER_STEP == 2
    half = np.concatenate([np.arange(LANES)[:, None] >= np.arange(LANES)[None, :],
                           np.ones((LANES, LANES), bool)], axis=1)
    tri = jnp.asarray(np.concatenate([half, half], axis=0), BF16)
    nh = HEADS_PER_STEP
    seq = pl.BlockSpec((None, s, LANES), lambda bi, p: (bi, 0, p))
    return pl.pallas_call(
        _sb_kernel,
        grid=(b, SB_WIDTH // LANES),
        in_specs=[seq, seq, seq, pl.BlockSpec(tri.shape, lambda bi, p: (0, 0)),
                  pl.BlockSpec((None,) + k_norm2.shape[1:], lambda bi, p: (bi, 0, 0, 0))],
        out_specs=seq,
        out_shape=jax.ShapeDtypeStruct((b, s, SB_WIDTH), BF16),
        scratch_shapes=[pltpu.VMEM((SB_BLOCKS_PER_ITER, nh, SB_TQ, LANES), F32)] * 2,
        compiler_params=pltpu.CompilerParams(
            dimension_semantics=("parallel", "parallel"), vmem_limit_bytes=VMEM_LIMIT),
        name="stick_breaking",
    )(qs, ks, vs, tri, k_norm2)


TAIL_TILES = 2
MLP_FF_CHUNK = 1024


def _token_major(ref, scr):
    d, rows, _ = ref.shape
    for c in range(scr.shape[0]):
        for r in range(d):
            scr[c, pl.ds(r, rows, stride=d), :] = ref[r, :, c * LANES:(c + 1) * LANES]
    return jnp.concatenate([scr[c] for c in range(scr.shape[0])], axis=1)


def _tail_kernel(x_ref, o0, o1, o2, l0, l1, l2, ob_ref, ga_ref, gb_ref,
                 wud, wus, wout, g_mlp, w1, w2, g_out, out_ref, *scratch, final_norm):
    tiles = range(TAIL_TILES)
    rows = [slice(t * DIL_TILE, (t + 1) * DIL_TILE) for t in tiles]
    o_a = []
    for t in tiles:
        scr = scratch[4 * t:4 * t + 4]
        outs = [o0[rows[t], :], _token_major(o1.at[t], scr[0]), _token_major(o2.at[t], scr[1])]
        lse = [l0[rows[t], :], _token_major(l1.at[t], scr[2]), _token_major(l2.at[t], scr[3])]
        m = jnp.maximum(jnp.maximum(lse[0], lse[1]), lse[2])
        e = [jnp.exp(l - m) for l in lse]
        o_a.append((e[0] * outs[0] + e[1] * outs[1] + e[2] * outs[2]) / (e[0] + e[1] + e[2]))
    up_a = [_dot(o_a[t].astype(BF16), wud[...]) for t in tiles]
    up_b = [_dot(ob_ref[rows[t], :], wus[...]) for t in tiles]
    merged = [ga_ref[rows[t], :] * up_a[t] + gb_ref[rows[t], :] * up_b[t] for t in tiles]
    y = [x_ref[rows[t], :] + _dot(merged[t].astype(BF16), wout[...]) for t in tiles]
    h2 = [_rms(y[t], g_mlp[...]).astype(BF16) for t in tiles]
    for c in range(D_FF // MLP_FF_CHUNK):
        cs = slice(c * MLP_FF_CHUNK, (c + 1) * MLP_FF_CHUNK)
        u = [jnp.maximum(_dot(h2[t], w1[:, cs]), 0.0) for t in tiles]
        y = [y[t] + _dot((u[t] * u[t]).astype(BF16), w2[cs, :]) for t in tiles]
    for t in tiles:
        out_ref[rows[t], :] = _rms(y[t], g_out[...]) if final_norm else y[t]


def _tail(x, oas, lses, o_b, g_a, g_b, wud, wus, wout, g_mlp, w1, w2, g_out, final_norm):
    b, s, d = x.shape
    tm = TAIL_TILES * DIL_TILE
    tok = lambda w: pl.BlockSpec((None, tm, w), lambda bi, i: (bi, i, 0))
    full = lambda a: pl.BlockSpec(a.shape, lambda bi, i: (0,) * a.ndim,
                                  pipeline_mode=pl.Buffered(1))

    def dil(a):
        if a.ndim == 3:
            return tok(DIL_OUT_WIDTH)
        return pl.BlockSpec((None, TAIL_TILES) + a.shape[2:], lambda bi, i: (bi, i, 0, 0, 0))

    weights = (wud, wus, wout, g_mlp, w1, w2, g_out)
    return pl.pallas_call(
        functools.partial(_tail_kernel, final_norm=final_norm),
        grid=(b, s // tm),
        in_specs=[tok(d)] + [dil(a) for a in (*oas, *lses)] + [tok(SB_WIDTH), tok(d), tok(d)]
                 + [full(a) for a in weights],
        out_specs=tok(d),
        out_shape=jax.ShapeDtypeStruct((b, s, d), F32),
        scratch_shapes=[pltpu.VMEM((DIL_OUT_WIDTH // LANES, DIL_TILE, LANES), F32)]
                       * (4 * TAIL_TILES),
        compiler_params=pltpu.CompilerParams(
            dimension_semantics=("parallel", "parallel"), vmem_limit_bytes=VMEM_LIMIT),
        name="tail",
    )(x, *oas, *lses, o_b, g_a, g_b, *weights)


def kernel(x, norm_mix_g, w_in, b_gate, w_up_dil, w_up_sb, w_out, norm_mlp_g, w_mlp_in,
           w_mlp_out, norm_final_g):
    depth = w_in.shape[0]
    row = lambda v: v.reshape(1, -1)
    for layer in range(depth):
        w = w_in[layer]
        c0 = 3 * DIL_WIDTH
        c1 = c0 + 3 * SB_WIDTH
        dil_cols = lambda i, grp: w[:, i * DIL_WIDTH + grp * DIL_OUT_WIDTH:
                                    i * DIL_WIDTH + (grp + 1) * DIL_OUT_WIDTH]
        w_dil = jnp.concatenate(
            [part for grp in range(len(DIL_GROUPS))
             for part in (dil_cols(0, grp) * (1.0 / math.sqrt(HEAD_DIM)), dil_cols(1, grp),
                          dil_cols(2, grp))], axis=1).astype(BF16)
        *qkv, qs, ks, vs, g_a, g_b, k_norm2 = _proj(
            x, row(norm_mix_g[layer]), w_dil, w[:, c0:c1].astype(BF16), w[:, c1:].astype(BF16),
            row(b_gate[layer]))
        dil = [_dilated_group(qkv[grp], grp) for grp in range(len(DIL_GROUPS))]
        o_b = _sb_attention(qs, ks, vs, k_norm2)
        x = _tail(x, [o for o, _ in dil], [l for _, l in dil], o_b, g_a, g_b,
                  w_up_dil[layer].astype(BF16), w_up_sb[layer].astype(BF16),
                  w_out[layer].astype(BF16), row(norm_mlp_g[layer]),
                  w_mlp_in[layer].astype(BF16), w_mlp_out[layer].astype(BF16),
                  row(norm_final_g), final_norm=layer == depth - 1)
    return x
```

```python
import functools
import math

import jax
import jax.numpy as jnp
import numpy as np
from jax import lax
from jax.experimental import pallas as pl
from jax.experimental.pallas import tpu as pltpu

D_MODEL = 1024
HEAD_DIM = 64
DIL_GROUPS = ((128, 1), (512, 4), (2048, 16))
DIL_HEADS_PER_GROUP = 4
N_DIL_HEADS = DIL_HEADS_PER_GROUP * len(DIL_GROUPS)
N_SB_HEADS = 8
DIL_WIDTH = N_DIL_HEADS * HEAD_DIM
DIL_OUT_WIDTH = DIL_HEADS_PER_GROUP * HEAD_DIM
SB_WIDTH = N_SB_HEADS * HEAD_DIM
D_FF = 4 * D_MODEL
BLOCK = 128
RMS_EPS = 1e-6
NEG_INF = -1e30
LOG2E = 1.4426950408889634
LANES = 128
SUBLANES = 8

VMEM_LIMIT = 56 * 1024 * 1024

BF16 = jnp.bfloat16
F32 = jnp.float32


def _dot(a, b):
    return jnp.dot(a, b, preferred_element_type=F32)


def _dot_nt(a, b):
    return lax.dot_general(a, b, (((1,), (1,)), ((), ())), preferred_element_type=F32)


def _rms(x, g):
    return x * lax.rsqrt(jnp.mean(x * x, axis=-1, keepdims=True) + RMS_EPS) * g


def _sigmoid(x):
    return 1.0 / (1.0 + jnp.exp(-x))


DIL_TILE = 256
PROJ_TILES = 2


def _residue_major(h_ref, d):
    rows = DIL_TILE // d
    return jnp.concatenate(
        [jnp.concatenate([h_ref[c, pl.ds(r, rows, stride=d), :] for r in range(d)], axis=0)
         for c in range(h_ref.shape[0])], axis=1)


def _proj_kernel(x_ref, g_ref, wd0, wd1, wd2, wqs, wks, wvs, wga, wgb, bga, bgb, head_sel,
                 d0_ref, d1_ref, d2_ref, qs_ref, ks_ref, vs_ref, ga_ref, gb_ref, kn_ref, *h_scr):
    tiles = range(PROJ_TILES)
    rows = [slice(t * DIL_TILE, (t + 1) * DIL_TILE) for t in tiles]
    h = []
    for t in tiles:
        h32 = _rms(x_ref[rows[t], :], g_ref[...])
        for c in range(h_scr[t].shape[0]):
            h_scr[t][c] = h32[:, c * LANES:(c + 1) * LANES]
        h.append(h32.astype(BF16))
    for (_, dilation), w_ref, o_ref in zip(DIL_GROUPS, (wd0, wd1, wd2), (d0_ref, d1_ref, d2_ref)):
        for t in tiles:
            hd = h[t] if dilation == 1 else _residue_major(h_scr[t], dilation).astype(BF16)
            o_ref[rows[t], :] = _dot(hd, w_ref[...]).astype(BF16)
    for t in tiles:
        qs_ref[rows[t], :] = (_dot(h[t], wqs[...]) * (LOG2E / math.sqrt(HEAD_DIM))).astype(BF16)
    k_norm2 = None
    for t in tiles:
        k = _dot(h[t], wks[...]).astype(BF16)
        ks_ref[rows[t], :] = k
        n2 = jnp.max(_dot(jnp.square(k.astype(F32)).astype(BF16), head_sel[...]), axis=0,
                     keepdims=True)
        k_norm2 = n2 if k_norm2 is None else jnp.maximum(k_norm2, n2)
    kn_ref[...] = jnp.broadcast_to(k_norm2, kn_ref.shape)
    for t in tiles:
        vs_ref[rows[t], :] = _dot(h[t], wvs[...]).astype(BF16)
    for t in tiles:
        ga_ref[rows[t], :] = _sigmoid(_dot(h[t], wga[...]) + bga[...])
    for t in tiles:
        gb_ref[rows[t], :] = _sigmoid(_dot(h[t], wgb[...]) + bgb[...])


def _proj(x, g, w_dil, w_sb, w_gate, b_gate):
    b, s, d = x.shape
    tm = PROJ_TILES * DIL_TILE
    tok = lambda w: pl.BlockSpec((None, tm, w), lambda bi, i: (bi, i, 0))
    const = lambda shape, col: pl.BlockSpec(shape, lambda bi, i: (0, col),
                                            pipeline_mode=pl.Buffered(1))
    cols = lambda a, n: [const((a.shape[0], a.shape[1] // n), c) for c in range(n)]
    dil_width = 3 * DIL_OUT_WIDTH
    out_shape = (
        (jax.ShapeDtypeStruct((b, s, dil_width), BF16),) * len(DIL_GROUPS)
        + (jax.ShapeDtypeStruct((b, s, SB_WIDTH), BF16),) * 3
        + (jax.ShapeDtypeStruct((b, s, D_MODEL), F32),) * 2
        + (jax.ShapeDtypeStruct((b, s // tm, SUBLANES, LANES), F32),))
    out_specs = ((tok(dil_width),) * len(DIL_GROUPS) + (tok(SB_WIDTH),) * 3
                 + (tok(D_MODEL),) * 2
                 + (pl.BlockSpec((None, None, SUBLANES, LANES), lambda bi, i: (bi, i, 0, 0)),))
    head_sel = jnp.asarray(np.arange(SB_WIDTH)[:, None] // HEAD_DIM == np.arange(LANES)[None, :],
                           BF16)
    return pl.pallas_call(
        _proj_kernel,
        grid=(b, s // tm),
        in_specs=([tok(d), const(g.shape, 0)] + cols(w_dil, len(DIL_GROUPS)) + cols(w_sb, 3)
                  + cols(w_gate, 2) + cols(b_gate, 2) + [const(head_sel.shape, 0)]),
        out_specs=out_specs,
        out_shape=out_shape,
        scratch_shapes=[pltpu.VMEM((d // LANES, DIL_TILE, LANES), F32)] * PROJ_TILES,
        compiler_params=pltpu.CompilerParams(
            dimension_semantics=("parallel", "parallel"), vmem_limit_bytes=VMEM_LIMIT),
        name="proj",
    )(x, g, *([w_dil] * len(DIL_GROUPS)), *([w_sb] * 3), *([w_gate] * 2), *([b_gate] * 2), head_sel)


DIL_MAX_QB = 16


def _alibi_slope(head):
    return 2.0 ** (-8.0 * (head + 1) / N_DIL_HEADS)


def _dil_kernel(q_ref, kp_ref, kc_ref, vp_ref, vc_ref, o_ref, lse_ref, *, group, dilation, n_steps,
                qb):
    n = pl.program_id(2)
    width = DIL_OUT_WIDTH
    qrows = qb * BLOCK
    n_res = q_ref.shape[1] if len(q_ref.shape) == 4 else 1
    res = range(n_res)
    take = lambda ref, r, n_rows: (ref[:, r] if len(ref.shape) == 4 else ref[...]).reshape(n_rows, width)
    q = [take(q_ref, r, qrows) for r in res]
    k = [jnp.concatenate([take(kp_ref, r, BLOCK), take(kc_ref, r, qrows)], axis=0) for r in res]
    v = [jnp.concatenate([take(vp_ref, r, BLOCK), take(vc_ref, r, qrows)], axis=0) for r in res]
    qi = lax.broadcasted_iota(jnp.int32, (BLOCK, 2 * BLOCK), 0)
    kj = lax.broadcasted_iota(jnp.int32, (BLOCK, 2 * BLOCK), 1)
    steps = qi + BLOCK - kj
    valid = (steps >= 0) & (steps <= n_steps)
    has_prev = (kj >= BLOCK) | (n > 0)
    dist = (steps * dilation).astype(F32)
    low = lax.broadcasted_iota(jnp.int32, (1, LANES), 1) < HEAD_DIM
    sel = (low, jnp.logical_not(low))
    work = [(r, p, j, e) for r in res for p in range(2) for j in range(qb) for e in range(2)]
    q_rows = lambda j: slice(j * BLOCK, (j + 1) * BLOCK)
    k_rows = lambda j: slice(j * BLOCK, (j + 2) * BLOCK)
    pair = lambda t, p: t[:, p * LANES:(p + 1) * LANES]
    qm = {(r, p, e): jnp.where(sel[e], pair(q[r], p), jnp.zeros((qrows, LANES), BF16))
          for r in res for p in range(2) for e in range(2)}
    bias = {(p, e): jnp.where(valid, -_alibi_slope(group * DIL_HEADS_PER_GROUP + 2 * p + e) * dist,
                              NEG_INF) for p in range(2) for e in range(2)}
    logits = {}
    for r in res:
        for p in range(2):
            for j in range(qb):
                both = _dot_nt(jnp.concatenate([qm[r, p, e][q_rows(j)] for e in range(2)], axis=0),
                               pair(k[r], p)[k_rows(j)])
                for e in range(2):
                    lg = both[e * BLOCK:(e + 1) * BLOCK] + bias[p, e]
                    logits[r, p, j, e] = jnp.where(has_prev, lg, NEG_INF) if j == 0 else lg
    ms = {w: jnp.max(logits[w], axis=-1, keepdims=True) for w in work}
    pr = {w: jnp.exp(logits[w] - ms[w]).astype(BF16) for w in work}
    ones = jnp.ones((2 * BLOCK, LANES), BF16)
    t = {(r, p, j): _dot(jnp.concatenate([pr[r, p, j, e] for e in range(2)], axis=0),
                         jnp.concatenate([pair(v[r], p)[k_rows(j)], ones], axis=1))
         for r in res for p in range(2) for j in range(qb)}
    for r in res:
        for j in range(qb):
            o_pairs, lse_pairs = [], []
            for p in range(2):
                head = [t[r, p, j][e * BLOCK:(e + 1) * BLOCK] for e in range(2)]
                num = jnp.where(low, head[0][:, :LANES], head[1][:, :LANES])
                den = jnp.where(low, head[0][:, LANES:], head[1][:, LANES:])
                o_pairs.append(num / den)
                lse_pairs.append(jnp.where(low, ms[r, p, j, 0], ms[r, p, j, 1]) + jnp.log(den))
            for ref, pairs in ((o_ref, o_pairs), (lse_ref, lse_pairs)):
                val = jnp.concatenate(pairs, axis=1)
                if len(ref.shape) == 4:
                    rows = ref.shape[2]
                    tiles = slice(j * BLOCK // rows, (j + 1) * BLOCK // rows)
                    ref[tiles, r] = val.reshape(BLOCK // rows, rows, width)
                else:
                    ref[j * BLOCK:(j + 1) * BLOCK, :] = val


def _dilated_group(qkv, group):
    window, dilation = DIL_GROUPS[group]
    b, s, _ = qkv.shape
    width = DIL_OUT_WIDTH
    qb = min(DIL_MAX_QB, s // dilation // BLOCK)
    n_res = min(dilation, DIL_MAX_QB // qb)
    qrows = qb * BLOCK
    n_grid = s // dilation // qrows
    first_prev = lambda n: jnp.maximum(n * qb - 1, 0)
    if dilation == 1:
        view = qkv
        cur = lambda c: pl.BlockSpec((None, qrows, width), lambda bi, r, n: (bi, n, c))
        prev = lambda c: pl.BlockSpec((None, BLOCK, width), lambda bi, r, n: (bi, first_prev(n), c))
        shp = jax.ShapeDtypeStruct((b, s, width), F32)
    else:
        rows = DIL_TILE // dilation
        view = qkv.reshape(b, s // DIL_TILE, dilation, rows, 3 * width)
        cur = lambda c: pl.BlockSpec((None, qrows // rows, n_res, rows, width),
                                     lambda bi, r, n: (bi, n, r, 0, c))
        prev = lambda c: pl.BlockSpec((None, BLOCK // rows, n_res, rows, width),
                                      lambda bi, r, n: (bi, first_prev(n), r, 0, c))
        shp = jax.ShapeDtypeStruct((b, s // DIL_TILE, dilation, rows, width), F32)
    return pl.pallas_call(
        functools.partial(_dil_kernel, group=group, dilation=dilation,
                          n_steps=window // dilation, qb=qb),
        grid=(b, dilation // n_res, n_grid),
        in_specs=[cur(0), prev(1), cur(1), prev(2), cur(2)],
        out_specs=(cur(0), cur(0)),
        out_shape=(shp, shp),
        compiler_params=pltpu.CompilerParams(
            dimension_semantics=("parallel", "parallel", "parallel"),
            vmem_limit_bytes=VMEM_LIMIT),
        name=f"dilated_d{dilation}",
    )(view, view, view, view, view)


SB_TQ = 64
SB_TK = 256
SB_WINDOW = 256
SB_DEAD = 136.0
SB_NORM_SLACK = 1.0 + 2.0 ** -8
SB_DOT_SLACK = 1.001
SB_BLOCKS_PER_ITER = 10
HEADS_PER_STEP = LANES // HEAD_DIM


def _softplus2(z):
    return jnp.maximum(z, 0.0) + jnp.log2(1.0 + jnp.exp2(-jnp.abs(z)))


def _split_hi_lo(sp):
    parts = []
    for c in range(sp.shape[1] // LANES):
        sp_c = sp[:, c * LANES:(c + 1) * LANES]
        hi = sp_c.astype(BF16)
        parts += [hi, (sp_c - hi.astype(F32)).astype(BF16)]
    return jnp.concatenate(parts, axis=1)


def _sb_suffix_sums(hl, tri):
    return [_dot(hl[:, 2 * c * LANES:2 * (c + 1) * LANES], tri)
            for c in range(hl.shape[1] // (2 * LANES))]


def _sb_weights(z, sums, carry):
    las = [None] * len(sums)
    for c in reversed(range(len(sums))):
        las[c] = z[:, c * LANES:(c + 1) * LANES] - sums[c][:, :LANES] - carry
        carry = carry + sums[c][:, LANES:]
    return jnp.exp2(jnp.concatenate(las, axis=1)).astype(BF16), carry


def _sb_kernel(q_ref, k_ref, v_ref, tri_ref, kn_ref, o_ref, acc_ref, carry_ref):
    s = q_ref.shape[0]
    n_q = s // SB_TQ
    heads = range(HEADS_PER_STEP)
    tri = tri_ref[...]
    lane = lax.broadcasted_iota(jnp.int32, (1, LANES), 1)
    own = [(lane >= h * HEAD_DIM) & (lane < (h + 1) * HEAD_DIM) for h in heads]

    kn = jnp.max(kn_ref[...], axis=(0, 1), keepdims=True).reshape(1, LANES)
    k_max = [jnp.sqrt(jnp.max(jnp.where(lane == HEADS_PER_STEP * pl.program_id(1) + h, kn, 0.0)))
             * SB_NORM_SLACK for h in heads]

    col_minus_row = (lax.broadcasted_iota(jnp.int32, (SB_TQ, SB_WINDOW), 1)
                     - lax.broadcasted_iota(jnp.int32, (SB_TQ, SB_WINDOW), 0))
    lead = SB_WINDOW - SB_TQ

    def query_blocks(blocks):
        items = [(slot, h) for slot in range(len(blocks)) for h in heads]
        rows, start, causal, q, z_max = {}, {}, {}, {}, {}
        for slot, qi in enumerate(blocks):
            first = qi * SB_TQ
            if isinstance(qi, int):
                start[slot] = max(first - lead, 0)
                rows[slot] = pl.ds(first, SB_TQ)
            else:
                start[slot] = pl.multiple_of(jnp.maximum(first - lead, 0), SB_TQ)
                rows[slot] = pl.ds(pl.multiple_of(first, SB_TQ), SB_TQ)
            causal[slot] = col_minus_row < (first - start[slot] if isinstance(qi, int) else lead)
            q_pair = q_ref[rows[slot], :]
            for h in heads:
                q[slot, h] = jnp.where(own[h], q_pair, jnp.zeros_like(q_pair))
                z_max[slot, h] = (jnp.sqrt(jnp.sum(jnp.square(q[slot, h].astype(F32)), axis=1,
                                                   keepdims=True)) * (k_max[h] * SB_DOT_SLACK) + 1.0)
        z = {(slot, h): jnp.where(causal[slot],
                                  _dot_nt(q[slot, h], k_ref[pl.ds(start[slot], SB_WINDOW), :]),
                                  NEG_INF) for slot, h in items}
        hl = {w: _split_hi_lo(_softplus2(z[w])) for w in items}
        n_chunks = SB_WINDOW // LANES
        stacked = _dot(jnp.concatenate(
            [hl[w][:, 2 * c * LANES:2 * (c + 1) * LANES] for w in items for c in range(n_chunks)],
            axis=0), tri)
        sums = {w: [stacked[(i * n_chunks + c) * SB_TQ:(i * n_chunks + c + 1) * SB_TQ]
                    for c in range(n_chunks)] for i, w in enumerate(items)}
        a, carry = {}, {}
        for w in items:
            a[w], carry[w] = _sb_weights(z[w], sums[w], jnp.zeros((SB_TQ, LANES), F32))
        for slot, h in items:
            acc_ref[slot, h] = _dot(a[slot, h], v_ref[pl.ds(start[slot], SB_WINDOW), :])
            carry_ref[slot, h] = carry[slot, h]

        def slack(slot, carries):
            per_head = [carries[h] - z_max[slot, h] for h in heads]
            return jnp.min(jnp.minimum(per_head[0], per_head[1]), keepdims=True)

        slacks = [slack(slot, [carry[slot, h] for h in heads]) for slot in range(len(blocks))]

        @pl.when(functools.reduce(jnp.minimum, slacks)[0, 0] < SB_DEAD)
        def _():
            for slot in range(len(blocks)):
                def cond(state):
                    ke, live = state
                    return (ke > 0) & live

                def body(state):
                    ke, _ = state
                    ks = pl.multiple_of(jnp.maximum(ke - SB_TK, 0), SB_TQ)
                    fresh = (lax.broadcasted_iota(jnp.int32, (SB_TQ, SB_TK), 1) + ks) < ke
                    carries = []
                    for h in heads:
                        zt = jnp.where(fresh, _dot_nt(q[slot, h], k_ref[pl.ds(ks, SB_TK), :]),
                                       NEG_INF)
                        at, ct = _sb_weights(
                            zt, _sb_suffix_sums(_split_hi_lo(_softplus2(zt)), tri),
                            carry_ref[slot, h])
                        acc_ref[slot, h] += _dot(at, v_ref[pl.ds(ks, SB_TK), :])
                        carry_ref[slot, h] = ct
                        carries.append(ct)
                    return ks, slack(slot, carries)[0, 0] < SB_DEAD

                lax.while_loop(cond, body, (jnp.asarray(start[slot], jnp.int32),
                                            slacks[slot][0, 0] < SB_DEAD))

        for slot in range(len(blocks)):
            o_ref[rows[slot], :] = jnp.where(own[0], acc_ref[slot, 0],
                                             acc_ref[slot, 1]).astype(o_ref.dtype)

    n_lead = lead // SB_TQ
    query_blocks(list(range(n_lead)))
    n_groups = (n_q - n_lead) // SB_BLOCKS_PER_ITER

    def group(i, _):
        query_blocks([n_lead + i * SB_BLOCKS_PER_ITER + j for j in range(SB_BLOCKS_PER_ITER)])
        return 0

    lax.fori_loop(0, n_groups, group, 0)
    rest = list(range(n_lead + n_groups * SB_BLOCKS_PER_ITER, n_q))
    if rest:
        query_blocks(rest)


def _sb_attention(qs, ks, vs, k_norm2):
    b, s, _ = qs.shape
    assert s % SB_TQ == 0 and s >= SB_WINDOW and SB_WINDOW % LANES == 0 and HEADS_PER_STEP == 2
    half = np.concatenate([np.arange(LANES)[:, None] >= np.arange(LANES)[None, :],
                           np.ones((LANES, LANES), bool)], axis=1)
    tri = jnp.asarray(np.concatenate([half, half], axis=0), BF16)
    nh = HEADS_PER_STEP
    seq = pl.BlockSpec((None, s, LANES), lambda bi, p: (bi, 0, p))
    return pl.pallas_call(
        _sb_kernel,
        grid=(b, SB_WIDTH // LANES),
        in_specs=[seq, seq, seq, pl.BlockSpec(tri.shape, lambda bi, p: (0, 0)),
                  pl.BlockSpec((None,) + k_norm2.shape[1:], lambda bi, p: (bi, 0, 0, 0))],
        out_specs=seq,
        out_shape=jax.ShapeDtypeStruct((b, s, SB_WIDTH), BF16),
        scratch_shapes=[pltpu.VMEM((SB_BLOCKS_PER_ITER, nh, SB_TQ, LANES), F32)] * 2,
        compiler_params=pltpu.CompilerParams(
            dimension_semantics=("parallel", "parallel"), vmem_limit_bytes=VMEM_LIMIT),
        name="stick_breaking",
    )(qs, ks, vs, tri, k_norm2)


TAIL_TILES = 2
MLP_FF_CHUNK = 1024


def _token_major(ref, scr):
    d, rows, _ = ref.shape
    for c in range(scr.shape[0]):
        for r in range(d):
            scr[c, pl.ds(r, rows, stride=d), :] = ref[r, :, c * LANES:(c + 1) * LANES]
    return jnp.concatenate([scr[c] for c in range(scr.shape[0])], axis=1)


def _tail_kernel(x_ref, o0, o1, o2, l0, l1, l2, ob_ref, ga_ref, gb_ref,
                 wud, wus, wout, g_mlp, w1, w2, g_out, out_ref, *scratch, final_norm):
    tiles = range(TAIL_TILES)
    rows = [slice(t * DIL_TILE, (t + 1) * DIL_TILE) for t in tiles]
    o_a = []
    for t in tiles:
        scr = scratch[4 * t:4 * t + 4]
        outs = [o0[rows[t], :], _token_major(o1.at[t], scr[0]), _token_major(o2.at[t], scr[1])]
        lse = [l0[rows[t], :], _token_major(l1.at[t], scr[2]), _token_major(l2.at[t], scr[3])]
        m = jnp.maximum(jnp.maximum(lse[0], lse[1]), lse[2])
        e = [jnp.exp(l - m) for l in lse]
        o_a.append((e[0] * outs[0] + e[1] * outs[1] + e[2] * outs[2]) / (e[0] + e[1] + e[2]))
    up_a = [_dot(o_a[t].astype(BF16), wud[...]) for t in tiles]
    up_b = [_dot(ob_ref[rows[t], :], wus[...]) for t in tiles]
    merged = [ga_ref[rows[t], :] * up_a[t] + gb_ref[rows[t], :] * up_b[t] for t in tiles]
    y = [x_ref[rows[t], :] + _dot(merged[t].astype(BF16), wout[...]) for t in tiles]
    h2 = [_rms(y[t], g_mlp[...]).astype(BF16) for t in tiles]
    for c in range(D_FF // MLP_FF_CHUNK):
        cs = slice(c * MLP_FF_CHUNK, (c + 1) * MLP_FF_CHUNK)
        u = [jnp.maximum(_dot(h2[t], w1[:, cs]), 0.0) for t in tiles]
        y = [y[t] + _dot((u[t] * u[t]).astype(BF16), w2[cs, :]) for t in tiles]
    for t in tiles:
        out_ref[rows[t], :] = _rms(y[t], g_out[...]) if final_norm else y[t]


def _tail(x, oas, lses, o_b, g_a, g_b, wud, wus, wout, g_mlp, w1, w2, g_out, final_norm):
    b, s, d = x.shape
    tm = TAIL_TILES * DIL_TILE
    tok = lambda w: pl.BlockSpec((None, tm, w), lambda bi, i: (bi, i, 0))
    full = lambda a: pl.BlockSpec(a.shape, lambda bi, i: (0,) * a.ndim,
                                  pipeline_mode=pl.Buffered(1))

    def dil(a):
        if a.ndim == 3:
            return tok(DIL_OUT_WIDTH)
        return pl.BlockSpec((None, TAIL_TILES) + a.shape[2:], lambda bi, i: (bi, i, 0, 0, 0))

    weights = (wud, wus, wout, g_mlp, w1, w2, g_out)
    return pl.pallas_call(
        functools.partial(_tail_kernel, final_norm=final_norm),
        grid=(b, s // tm),
        in_specs=[tok(d)] + [dil(a) for a in (*oas, *lses)] + [tok(SB_WIDTH), tok(d), tok(d)]
                 + [full(a) for a in weights],
        out_specs=tok(d),
        out_shape=jax.ShapeDtypeStruct((b, s, d), F32),
        scratch_shapes=[pltpu.VMEM((DIL_OUT_WIDTH // LANES, DIL_TILE, LANES), F32)]
                       * (4 * TAIL_TILES),
        compiler_params=pltpu.CompilerParams(
            dimension_semantics=("parallel", "parallel"), vmem_limit_bytes=VMEM_LIMIT),
        name="tail",
    )(x, *oas, *lses, o_b, g_a, g_b, *weights)


def kernel(x, norm_mix_g, w_in, b_gate, w_up_dil, w_up_sb, w_out, norm_mlp_g, w_mlp_in,
           w_mlp_out, norm_final_g):
    depth = w_in.shape[0]
    row = lambda v: v.reshape(1, -1)
    for layer in range(depth):
        w = w_in[layer]
        c0 = 3 * DIL_WIDTH
        c1 = c0 + 3 * SB_WIDTH
        dil_cols = lambda i, grp: w[:, i * DIL_WIDTH + grp * DIL_OUT_WIDTH:
                                    i * DIL_WIDTH + (grp + 1) * DIL_OUT_WIDTH]
        w_dil = jnp.concatenate(
            [part for grp in range(len(DIL_GROUPS))
             for part in (dil_cols(0, grp) * (1.0 / math.sqrt(HEAD_DIM)), dil_cols(1, grp),
                          dil_cols(2, grp))], axis=1).astype(BF16)
        *qkv, qs, ks, vs, g_a, g_b, k_norm2 = _proj(
            x, row(norm_mix_g[layer]), w_dil, w[:, c0:c1].astype(BF16), w[:, c1:].astype(BF16),
            row(b_gate[layer]))
        dil = [_dilated_group(qkv[grp], grp) for grp in range(len(DIL_GROUPS))]
        o_b = _sb_attention(qs, ks, vs, k_norm2)
        x = _tail(x, [o for o, _ in dil], [l for _, l in dil], o_b, g_a, g_b,
                  w_up_dil[layer].astype(BF16), w_up_sb[layer].astype(BF16),
                  w_out[layer].astype(BF16), row(norm_mlp_g[layer]),
                  w_mlp_in[layer].astype(BF16), w_mlp_out[layer].astype(BF16),
                  row(norm_final_g), final_norm=layer == depth - 1)
    return x
```

```python
import functools
import math

import jax
import jax.numpy as jnp
import numpy as np
from jax import lax
from jax.experimental import pallas as pl
from jax.experimental.pallas import tpu as pltpu

D_MODEL = 1024
HEAD_DIM = 64
DIL_GROUPS = ((128, 1), (512, 4), (2048, 16))
DIL_HEADS_PER_GROUP = 4
N_DIL_HEADS = DIL_HEADS_PER_GROUP * len(DIL_GROUPS)
N_SB_HEADS = 8
DIL_WIDTH = N_DIL_HEADS * HEAD_DIM
DIL_OUT_WIDTH = DIL_HEADS_PER_GROUP * HEAD_DIM
SB_WIDTH = N_SB_HEADS * HEAD_DIM
D_FF = 4 * D_MODEL
BLOCK = 128
RMS_EPS = 1e-6
NEG_INF = -1e30
LOG2E = 1.4426950408889634
LANES = 128
SUBLANES = 8

VMEM_LIMIT = 56 * 1024 * 1024

BF16 = jnp.bfloat16
F32 = jnp.float32


def _dot(a, b):
    return jnp.dot(a, b, preferred_element_type=F32)


def _dot_nt(a, b):
    return lax.dot_general(a, b, (((1,), (1,)), ((), ())), preferred_element_type=F32)


def _rms(x, g):
    return x * lax.rsqrt(jnp.mean(x * x, axis=-1, keepdims=True) + RMS_EPS) * g


def _sigmoid(x):
    return 1.0 / (1.0 + jnp.exp(-x))


DIL_TILE = 256
PROJ_TILES = 2


def _residue_major(h_ref, d):
    rows = DIL_TILE // d
    return jnp.concatenate(
        [jnp.concatenate([h_ref[c, pl.ds(r, rows, stride=d), :] for r in range(d)], axis=0)
         for c in range(h_ref.shape[0])], axis=1)


def _proj_kernel(x_ref, g_ref, wd0, wd1, wd2, wqs, wks, wvs, wga, wgb, bga, bgb, head_sel,
                 d0_ref, d1_ref, d2_ref, qs_ref, ks_ref, vs_ref, ga_ref, gb_ref, kn_ref, *h_scr):
    tiles = range(PROJ_TILES)
    rows = [slice(t * DIL_TILE, (t + 1) * DIL_TILE) for t in tiles]
    h = []
    for t in tiles:
        h32 = _rms(x_ref[rows[t], :], g_ref[...])
        for c in range(h_scr[t].shape[0]):
            h_scr[t][c] = h32[:, c * LANES:(c + 1) * LANES]
        h.append(h32.astype(BF16))
    for (_, dilation), w_ref, o_ref in zip(DIL_GROUPS, (wd0, wd1, wd2), (d0_ref, d1_ref, d2_ref)):
        for t in tiles:
            hd = h[t] if dilation == 1 else _residue_major(h_scr[t], dilation).astype(BF16)
            o_ref[rows[t], :] = _dot(hd, w_ref[...]).astype(BF16)
    for t in tiles:
        qs_ref[rows[t], :] = (_dot(h[t], wqs[...]) * (LOG2E / math.sqrt(HEAD_DIM))).astype(BF16)
    k_norm2 = None
    for t in tiles:
        k = _dot(h[t], wks[...]).astype(BF16)
        ks_ref[rows[t], :] = k
        n2 = jnp.max(_dot(jnp.square(k.astype(F32)).astype(BF16), head_sel[...]), axis=0,
                     keepdims=True)
        k_norm2 = n2 if k_norm2 is None else jnp.maximum(k_norm2, n2)
    kn_ref[...] = jnp.broadcast_to(k_norm2, kn_ref.shape)
    for t in tiles:
        vs_ref[rows[t], :] = _dot(h[t], wvs[...]).astype(BF16)
    for t in tiles:
        ga_ref[rows[t], :] = _sigmoid(_dot(h[t], wga[...]) + bga[...])
    for t in tiles:
        gb_ref[rows[t], :] = _sigmoid(_dot(h[t], wgb[...]) + bgb[...])


def _proj(x, g, w_dil, w_sb, w_gate, b_gate):
    b, s, d = x.shape
    tm = PROJ_TILES * DIL_TILE
    tok = lambda w: pl.BlockSpec((None, tm, w), lambda bi, i: (bi, i, 0))
    const = lambda shape, col: pl.BlockSpec(shape, lambda bi, i: (0, col),
                                            pipeline_mode=pl.Buffered(1))
    cols = lambda a, n: [const((a.shape[0], a.shape[1] // n), c) for c in range(n)]
    dil_width = 3 * DIL_OUT_WIDTH
    out_shape = (
        (jax.ShapeDtypeStruct((b, s, dil_width), BF16),) * len(DIL_GROUPS)
        + (jax.ShapeDtypeStruct((b, s, SB_WIDTH), BF16),) * 3
        + (jax.ShapeDtypeStruct((b, s, D_MODEL), F32),) * 2
        + (jax.ShapeDtypeStruct((b, s // tm, SUBLANES, LANES), F32),))
    out_specs = ((tok(dil_width),) * len(DIL_GROUPS) + (tok(SB_WIDTH),) * 3
                 + (tok(D_MODEL),) * 2
                 + (pl.BlockSpec((None, None, SUBLANES, LANES), lambda bi, i: (bi, i, 0, 0)),))
    head_sel = jnp.asarray(np.arange(SB_WIDTH)[:, None] // HEAD_DIM == np.arange(LANES)[None, :],
                           BF16)
    return pl.pallas_call(
        _proj_kernel,
        grid=(b, s // tm),
        in_specs=([tok(d), const(g.shape, 0)] + cols(w_dil, len(DIL_GROUPS)) + cols(w_sb, 3)
                  + cols(w_gate, 2) + cols(b_gate, 2) + [const(head_sel.shape, 0)]),
        out_specs=out_specs,
        out_shape=out_shape,
        scratch_shapes=[pltpu.VMEM((d // LANES, DIL_TILE, LANES), F32)] * PROJ_TILES,
        compiler_params=pltpu.CompilerParams(
            dimension_semantics=("parallel", "parallel"), vmem_limit_bytes=VMEM_LIMIT),
        name="proj",
    )(x, g, *([w_dil] * len(DIL_GROUPS)), *([w_sb] * 3), *([w_gate] * 2), *([b_gate] * 2), head_sel)


DIL_MAX_QB = 16


def _alibi_slope(head):
    return 2.0 ** (-8.0 * (head + 1) / N_DIL_HEADS)


def _dil_kernel(q_ref, kp_ref, kc_ref, vp_ref, vc_ref, o_ref, lse_ref, *, group, dilation, n_steps,
                qb):
    n = pl.program_id(2)
    width = DIL_OUT_WIDTH
    qrows = qb * BLOCK
    n_res = q_ref.shape[1] if len(q_ref.shape) == 4 else 1
    res = range(n_res)
    take = lambda ref, r, n_rows: (ref[:, r] if len(ref.shape) == 4 else ref[...]).reshape(n_rows, width)
    q = [take(q_ref, r, qrows) for r in res]
    k = [jnp.concatenate([take(kp_ref, r, BLOCK), take(kc_ref, r, qrows)], axis=0) for r in res]
    v = [jnp.concatenate([take(vp_ref, r, BLOCK), take(vc_ref, r, qrows)], axis=0) for r in res]
    qi = lax.broadcasted_iota(jnp.int32, (BLOCK, 2 * BLOCK), 0)
    kj = lax.broadcasted_iota(jnp.int32, (BLOCK, 2 * BLOCK), 1)
    steps = qi + BLOCK - kj
    valid = (steps >= 0) & (steps <= n_steps)
    has_prev = (kj >= BLOCK) | (n > 0)
    dist = (steps * dilation).astype(F32)
    low = lax.broadcasted_iota(jnp.int32, (1, LANES), 1) < HEAD_DIM
    sel = (low, jnp.logical_not(low))
    work = [(r, p, j, e) for r in res for p in range(2) for j in range(qb) for e in range(2)]
    q_rows = lambda j: slice(j * BLOCK, (j + 1) * BLOCK)
    k_rows = lambda j: slice(j * BLOCK, (j + 2) * BLOCK)
    pair = lambda t, p: t[:, p * LANES:(p + 1) * LANES]
    qm = {(r, p, e): jnp.where(sel[e], pair(q[r], p), jnp.zeros((qrows, LANES), BF16))
          for r in res for p in range(2) for e in range(2)}
    bias = {(p, e): jnp.where(valid, -_alibi_slope(group * DIL_HEADS_PER_GROUP + 2 * p + e) * dist,
                              NEG_INF) for p in range(2) for e in range(2)}
    logits = {}
    for r in res:
        for p in range(2):
            for j in range(qb):
                both = _dot_nt(jnp.concatenate([qm[r, p, e][q_rows(j)] for e in range(2)], axis=0),
                               pair(k[r], p)[k_rows(j)])
                for e in range(2):
                    lg = both[e * BLOCK:(e + 1) * BLOCK] + bias[p, e]
                    logits[r, p, j, e] = jnp.where(has_prev, lg, NEG_INF) if j == 0 else lg
    ms = {w: jnp.max(logits[w], axis=-1, keepdims=True) for w in work}
    pr = {w: jnp.exp(logits[w] - ms[w]).astype(BF16) for w in work}
    ones = jnp.ones((2 * BLOCK, LANES), BF16)
    t = {(r, p, j): _dot(jnp.concatenate([pr[r, p, j, e] for e in range(2)], axis=0),
                         jnp.concatenate([pair(v[r], p)[k_rows(j)], ones], axis=1))
         for r in res for p in range(2) for j in range(qb)}
    for r in res:
        for j in range(qb):
            o_pairs, lse_pairs = [], []
            for p in range(2):
                head = [t[r, p, j][e * BLOCK:(e + 1) * BLOCK] for e in range(2)]
                num = jnp.where(low, head[0][:, :LANES], head[1][:, :LANES])
                den = jnp.where(low, head[0][:, LANES:], head[1][:, LANES:])
                o_pairs.append(num / den)
                lse_pairs.append(jnp.where(low, ms[r, p, j, 0], ms[r, p, j, 1]) + jnp.log(den))
            for ref, pairs in ((o_ref, o_pairs), (lse_ref, lse_pairs)):
                val = jnp.concatenate(pairs, axis=1)
                if len(ref.shape) == 4:
                    rows = ref.shape[2]
                    tiles = slice(j * BLOCK // rows, (j + 1) * BLOCK // rows)
                    ref[tiles, r] = val.reshape(BLOCK // rows, rows, width)
                else:
                    ref[j * BLOCK:(j + 1) * BLOCK, :] = val


def _dilated_group(qkv, group):
    window, dilation = DIL_GROUPS[group]
    b, s, _ = qkv.shape
    width = DIL_OUT_WIDTH
    qb = min(DIL_MAX_QB, s // dilation // BLOCK)
    n_res = min(dilation, DIL_MAX_QB // qb)
    qrows = qb * BLOCK
    n_grid = s // dilation // qrows
    first_prev = lambda n: jnp.maximum(n * qb - 1, 0)
    if dilation == 1:
        view = qkv
        cur = lambda c: pl.BlockSpec((None, qrows, width), lambda bi, r, n: (bi, n, c))
        prev = lambda c: pl.BlockSpec((None, BLOCK, width), lambda bi, r, n: (bi, first_prev(n), c))
        shp = jax.ShapeDtypeStruct((b, s, width), F32)
    else:
        rows = DIL_TILE // dilation
        view = qkv.reshape(b, s // DIL_TILE, dilation, rows, 3 * width)
        cur = lambda c: pl.BlockSpec((None, qrows // rows, n_res, rows, width),
                                     lambda bi, r, n: (bi, n, r, 0, c))
        prev = lambda c: pl.BlockSpec((None, BLOCK // rows, n_res, rows, width),
                                      lambda bi, r, n: (bi, first_prev(n), r, 0, c))
        shp = jax.ShapeDtypeStruct((b, s // DIL_TILE, dilation, rows, width), F32)
    return pl.pallas_call(
        functools.partial(_dil_kernel, group=group, dilation=dilation,
                          n_steps=window // dilation, qb=qb),
        grid=(b, dilation // n_res, n_grid),
        in_specs=[cur(0), prev(1), cur(1), prev(2), cur(2)],
        out_specs=(cur(0), cur(0)),
        out_shape=(shp, shp),
        compiler_params=pltpu.CompilerParams(
            dimension_semantics=("parallel", "parallel", "parallel"),
            vmem_limit_bytes=VMEM_LIMIT),
        name=f"dilated_d{dilation}",
    )(view, view, view, view, view)


SB_TQ = 64
SB_TK = 256
SB_WINDOW = 256
SB_DEAD = 136.0
SB_NORM_SLACK = 1.0 + 2.0 ** -8
SB_DOT_SLACK = 1.001
SB_BLOCKS_PER_ITER = 10
HEADS_PER_STEP = LANES // HEAD_DIM


def _softplus2(z):
    return jnp.maximum(z, 0.0) + jnp.log2(1.0 + jnp.exp2(-jnp.abs(z)))


def _split_hi_lo(sp):
    parts = []
    for c in range(sp.shape[1] // LANES):
        sp_c = sp[:, c * LANES:(c + 1) * LANES]
        hi = sp_c.astype(BF16)
        parts += [hi, (sp_c - hi.astype(F32)).astype(BF16)]
    return jnp.concatenate(parts, axis=1)


def _sb_suffix_sums(hl, tri):
    return [_dot(hl[:, 2 * c * LANES:2 * (c + 1) * LANES], tri)
            for c in range(hl.shape[1] // (2 * LANES))]


def _sb_weights(z, sums, carry):
    las = [None] * len(sums)
    for c in reversed(range(len(sums))):
        las[c] = z[:, c * LANES:(c + 1) * LANES] - sums[c][:, :LANES] - carry
        carry = carry + sums[c][:, LANES:]
    return jnp.exp2(jnp.concatenate(las, axis=1)).astype(BF16), carry


def _sb_kernel(q_ref, k_ref, v_ref, tri_ref, kn_ref, o_ref, acc_ref, carry_ref):
    s = q_ref.shape[0]
    n_q = s // SB_TQ
    heads = range(HEADS_PER_STEP)
    tri = tri_ref[...]
    lane = lax.broadcasted_iota(jnp.int32, (1, LANES), 1)
    own = [(lane >= h * HEAD_DIM) & (lane < (h + 1) * HEAD_DIM) for h in heads]

    kn = jnp.max(kn_ref[...], axis=(0, 1), keepdims=True).reshape(1, LANES)
    k_max = [jnp.sqrt(jnp.max(jnp.where(lane == HEADS_PER_STEP * pl.program_id(1) + h, kn, 0.0)))
             * SB_NORM_SLACK for h in heads]

    col_minus_row = (lax.broadcasted_iota(jnp.int32, (SB_TQ, SB_WINDOW), 1)
                     - lax.broadcasted_iota(jnp.int32, (SB_TQ, SB_WINDOW), 0))
    lead = SB_WINDOW - SB_TQ

    def query_blocks(blocks):
        items = [(slot, h) for slot in range(len(blocks)) for h in heads]
        rows, start, causal, q, z_max = {}, {}, {}, {}, {}
        for slot, qi in enumerate(blocks):
            first = qi * SB_TQ
            if isinstance(qi, int):
                start[slot] = max(first - lead, 0)
                rows[slot] = pl.ds(first, SB_TQ)
            else:
                start[slot] = pl.multiple_of(jnp.maximum(first - lead, 0), SB_TQ)
                rows[slot] = pl.ds(pl.multiple_of(first, SB_TQ), SB_TQ)
            causal[slot] = col_minus_row < (first - start[slot] if isinstance(qi, int) else lead)
            q_pair = q_ref[rows[slot], :]
            for h in heads:
                q[slot, h] = jnp.where(own[h], q_pair, jnp.zeros_like(q_pair))
                z_max[slot, h] = (jnp.sqrt(jnp.sum(jnp.square(q[slot, h].astype(F32)), axis=1,
                                                   keepdims=True)) * (k_max[h] * SB_DOT_SLACK) + 1.0)
        z = {}
        for slot in range(len(blocks)):
            both = _dot_nt(jnp.concatenate([q[slot, h] for h in heads], axis=0),
                           k_ref[pl.ds(start[slot], SB_WINDOW), :])
            for h in heads:
                z[slot, h] = jnp.where(causal[slot], both[h * SB_TQ:(h + 1) * SB_TQ], NEG_INF)
        hl = {w: _split_hi_lo(_softplus2(z[w])) for w in items}
        n_chunks = SB_WINDOW // LANES
        stacked = _dot(jnp.concatenate(
            [hl[w][:, 2 * c * LANES:2 * (c + 1) * LANES] for w in items for c in range(n_chunks)],
            axis=0), tri)
        sums = {w: [stacked[(i * n_chunks + c) * SB_TQ:(i * n_chunks + c + 1) * SB_TQ]
                    for c in range(n_chunks)] for i, w in enumerate(items)}
        a, carry = {}, {}
        for w in items:
            a[w], carry[w] = _sb_weights(z[w], sums[w], jnp.zeros((SB_TQ, LANES), F32))
        for slot in range(len(blocks)):
            both = _dot(jnp.concatenate([a[slot, h] for h in heads], axis=0),
                        v_ref[pl.ds(start[slot], SB_WINDOW), :])
            for h in heads:
                acc_ref[slot, h] = both[h * SB_TQ:(h + 1) * SB_TQ]
                carry_ref[slot, h] = carry[slot, h]

        def slack(slot, carries):
            per_head = [carries[h] - z_max[slot, h] for h in heads]
            return jnp.min(jnp.minimum(per_head[0], per_head[1]), keepdims=True)

        slacks = [slack(slot, [carry[slot, h] for h in heads]) for slot in range(len(blocks))]

        @pl.when(functools.reduce(jnp.minimum, slacks)[0, 0] < SB_DEAD)
        def _():
            for slot in range(len(blocks)):
                def cond(state):
                    ke, live = state
                    return (ke > 0) & live

                def body(state):
                    ke, _ = state
                    ks = pl.multiple_of(jnp.maximum(ke - SB_TK, 0), SB_TQ)
                    fresh = (lax.broadcasted_iota(jnp.int32, (SB_TQ, SB_TK), 1) + ks) < ke
                    carries = []
                    for h in heads:
                        zt = jnp.where(fresh, _dot_nt(q[slot, h], k_ref[pl.ds(ks, SB_TK), :]),
                                       NEG_INF)
                        at, ct = _sb_weights(
                            zt, _sb_suffix_sums(_split_hi_lo(_softplus2(zt)), tri),
                            carry_ref[slot, h])
                        acc_ref[slot, h] += _dot(at, v_ref[pl.ds(ks, SB_TK), :])
                        carry_ref[slot, h] = ct
                        carries.append(ct)
                    return ks, slack(slot, carries)[0, 0] < SB_DEAD

                lax.while_loop(cond, body, (jnp.asarray(start[slot], jnp.int32),
                                            slacks[slot][0, 0] < SB_DEAD))

        for slot in range(len(blocks)):
            o_ref[rows[slot], :] = jnp.where(own[0], acc_ref[slot, 0],
                                             acc_ref[slot, 1]).astype(o_ref.dtype)

    n_lead = lead // SB_TQ
    query_blocks(list(range(n_lead)))
    n_groups = (n_q - n_lead) // SB_BLOCKS_PER_ITER

    def group(i, _):
        query_blocks([n_lead + i * SB_BLOCKS_PER_ITER + j for j in range(SB_BLOCKS_PER_ITER)])
        return 0

    lax.fori_loop(0, n_groups, group, 0)
    rest = list(range(n_lead + n_groups * SB_BLOCKS_PER_ITER, n_q))
    if rest:
        query_blocks(rest)


def _sb_attention(qs, ks, vs, k_norm2):
    b, s, _ = qs.shape
    assert s % SB_TQ == 0 and s >= SB_WINDOW and SB_WINDOW % LANES == 0 and HEADS_PER_STEP == 2
    half = np.concatenate([np.arange(LANES)[:, None] >= np.arange(LANES)[None, :],
                           np.ones((LANES, LANES), bool)], axis=1)
    tri = jnp.asarray(np.concatenate([half, half], axis=0), BF16)
    nh = HEADS_PER_STEP
    seq = pl.BlockSpec((None, s, LANES), lambda bi, p: (bi, 0, p))
    return pl.pallas_call(
        _sb_kernel,
        grid=(b, SB_WIDTH // LANES),
        in_specs=[seq, seq, seq, pl.BlockSpec(tri.shape, lambda bi, p: (0, 0)),
                  pl.BlockSpec((None,) + k_norm2.shape[1:], lambda bi, p: (bi, 0, 0, 0))],
        out_specs=seq,
        out_shape=jax.ShapeDtypeStruct((b, s, SB_WIDTH), BF16),
        scratch_shapes=[pltpu.VMEM((SB_BLOCKS_PER_ITER, nh, SB_TQ, LANES), F32)] * 2,
        compiler_params=pltpu.CompilerParams(
            dimension_semantics=("parallel", "parallel"), vmem_limit_bytes=VMEM_LIMIT),
        name="stick_breaking",
    )(qs, ks, vs, tri, k_norm2)


TAIL_TILES = 2
MLP_FF_CHUNK = 1024


def _token_major(ref, scr):
    d, rows, _ = ref.shape
    for c in range(scr.shape[0]):
        for r in range(d):
            scr[c, pl.ds(r, rows, stride=d), :] = ref[r, :, c * LANES:(c + 1) * LANES]
    return jnp.concatenate([scr[c] for c in range(scr.shape[0])], axis=1)


def _tail_kernel(x_ref, o0, o1, o2, l0, l1, l2, ob_ref, ga_ref, gb_ref,
                 wud, wus, wout, g_mlp, w1, w2, g_out, out_ref, *scratch, final_norm):
    tiles = range(TAIL_TILES)
    rows = [slice(t * DIL_TILE, (t + 1) * DIL_TILE) for t in tiles]
    o_a = []
    for t in tiles:
        scr = scratch[4 * t:4 * t + 4]
        outs = [o0[rows[t], :], _token_major(o1.at[t], scr[0]), _token_major(o2.at[t], scr[1])]
        lse = [l0[rows[t], :], _token_major(l1.at[t], scr[2]), _token_major(l2.at[t], scr[3])]
        m = jnp.maximum(jnp.maximum(lse[0], lse[1]), lse[2])
        e = [jnp.exp(l - m) for l in lse]
        o_a.append((e[0] * outs[0] + e[1] * outs[1] + e[2] * outs[2]) / (e[0] + e[1] + e[2]))
    up_a = [_dot(o_a[t].astype(BF16), wud[...]) for t in tiles]
    up_b = [_dot(ob_ref[rows[t], :], wus[...]) for t in tiles]
    merged = [ga_ref[rows[t], :] * up_a[t] + gb_ref[rows[t], :] * up_b[t] for t in tiles]
    y = [x_ref[rows[t], :] + _dot(merged[t].astype(BF16), wout[...]) for t in tiles]
    h2 = [_rms(y[t], g_mlp[...]).astype(BF16) for t in tiles]
    for c in range(D_FF // MLP_FF_CHUNK):
        cs = slice(c * MLP_FF_CHUNK, (c + 1) * MLP_FF_CHUNK)
        u = [jnp.maximum(_dot(h2[t], w1[:, cs]), 0.0) for t in tiles]
        y = [y[t] + _dot((u[t] * u[t]).astype(BF16), w2[cs, :]) for t in tiles]
    for t in tiles:
        out_ref[rows[t], :] = _rms(y[t], g_out[...]) if final_norm else y[t]


def _tail(x, oas, lses, o_b, g_a, g_b, wud, wus, wout, g_mlp, w1, w2, g_out, final_norm):
    b, s, d = x.shape
    tm = TAIL_TILES * DIL_TILE
    tok = lambda w: pl.BlockSpec((None, tm, w), lambda bi, i: (bi, i, 0))
    full = lambda a: pl.BlockSpec(a.shape, lambda bi, i: (0,) * a.ndim,
                                  pipeline_mode=pl.Buffered(1))

    def dil(a):
        if a.ndim == 3:
            return tok(DIL_OUT_WIDTH)
        return pl.BlockSpec((None, TAIL_TILES) + a.shape[2:], lambda bi, i: (bi, i, 0, 0, 0))

    weights = (wud, wus, wout, g_mlp, w1, w2, g_out)
    return pl.pallas_call(
        functools.partial(_tail_kernel, final_norm=final_norm),
        grid=(b, s // tm),
        in_specs=[tok(d)] + [dil(a) for a in (*oas, *lses)] + [tok(SB_WIDTH), tok(d), tok(d)]
                 + [full(a) for a in weights],
        out_specs=tok(d),
        out_shape=jax.ShapeDtypeStruct((b, s, d), F32),
        scratch_shapes=[pltpu.VMEM((DIL_OUT_WIDTH // LANES, DIL_TILE, LANES), F32)]
                       * (4 * TAIL_TILES),
        compiler_params=pltpu.CompilerParams(
            dimension_semantics=("parallel", "parallel"), vmem_limit_bytes=VMEM_LIMIT),
        name="tail",
    )(x, *oas, *lses, o_b, g_a, g_b, *weights)


def kernel(x, norm_mix_g, w_in, b_gate, w_up_dil, w_up_sb, w_out, norm_mlp_g, w_mlp_in,
           w_mlp_out, norm_final_g):
    depth = w_in.shape[0]
    row = lambda v: v.reshape(1, -1)
    for layer in range(depth):
        w = w_in[layer]
        c0 = 3 * DIL_WIDTH
        c1 = c0 + 3 * SB_WIDTH
        dil_cols = lambda i, grp: w[:, i * DIL_WIDTH + grp * DIL_OUT_WIDTH:
                                    i * DIL_WIDTH + (grp + 1) * DIL_OUT_WIDTH]
        w_dil = jnp.concatenate(
            [part for grp in range(len(DIL_GROUPS))
             for part in (dil_cols(0, grp) * (1.0 / math.sqrt(HEAD_DIM)), dil_cols(1, grp),
                          dil_cols(2, grp))], axis=1).astype(BF16)
        *qkv, qs, ks, vs, g_a, g_b, k_norm2 = _proj(
            x, row(norm_mix_g[layer]), w_dil, w[:, c0:c1].astype(BF16), w[:, c1:].astype(BF16),
            row(b_gate[layer]))
        dil = [_dilated_group(qkv[grp], grp) for grp in range(len(DIL_GROUPS))]
        o_b = _sb_attention(qs, ks, vs, k_norm2)
        x = _tail(x, [o for o, _ in dil], [l for _, l in dil], o_b, g_a, g_b,
                  w_up_dil[layer].astype(BF16), w_up_sb[layer].astype(BF16),
                  w_out[layer].astype(BF16), row(norm_mlp_g[layer]),
                  w_mlp_in[layer].astype(BF16), w_mlp_out[layer].astype(BF16),
                  row(norm_final_g), final_norm=layer == depth - 1)
    return x
```

```python
import functools
import math

import jax
import jax.numpy as jnp
import numpy as np
from jax import lax
from jax.experimental import pallas as pl
from jax.experimental.pallas import tpu as pltpu

D_MODEL = 1024
HEAD_DIM = 64
DIL_GROUPS = ((128, 1), (512, 4), (2048, 16))
DIL_HEADS_PER_GROUP = 4
N_DIL_HEADS = DIL_HEADS_PER_GROUP * len(DIL_GROUPS)
N_SB_HEADS = 8
DIL_WIDTH = N_DIL_HEADS * HEAD_DIM
DIL_OUT_WIDTH = DIL_HEADS_PER_GROUP * HEAD_DIM
SB_WIDTH = N_SB_HEADS * HEAD_DIM
D_FF = 4 * D_MODEL
BLOCK = 128
RMS_EPS = 1e-6
NEG_INF = -1e30
LOG2E = 1.4426950408889634
LANES = 128
SUBLANES = 8

VMEM_LIMIT = 56 * 1024 * 1024

BF16 = jnp.bfloat16
F32 = jnp.float32


def _dot(a, b):
    return jnp.dot(a, b, preferred_element_type=F32)


def _dot_nt(a, b):
    return lax.dot_general(a, b, (((1,), (1,)), ((), ())), preferred_element_type=F32)


def _rms(x, g):
    return x * lax.rsqrt(jnp.mean(x * x, axis=-1, keepdims=True) + RMS_EPS) * g


def _sigmoid(x):
    return 1.0 / (1.0 + jnp.exp(-x))


DIL_TILE = 256
PROJ_TILES = 2


def _residue_major(h_ref, d):
    rows = DIL_TILE // d
    return jnp.concatenate(
        [jnp.concatenate([h_ref[c, pl.ds(r, rows, stride=d), :] for r in range(d)], axis=0)
         for c in range(h_ref.shape[0])], axis=1)


def _proj_kernel(x_ref, g_ref, wd0, wd1, wd2, wqs, wks, wvs, wga, wgb, bga, bgb, head_sel,
                 d0_ref, d1_ref, d2_ref, qs_ref, ks_ref, vs_ref, ga_ref, gb_ref, kn_ref, *h_scr):
    tiles = range(PROJ_TILES)
    rows = [slice(t * DIL_TILE, (t + 1) * DIL_TILE) for t in tiles]
    h = []
    for t in tiles:
        h32 = _rms(x_ref[rows[t], :], g_ref[...])
        for c in range(h_scr[t].shape[0]):
            h_scr[t][c] = h32[:, c * LANES:(c + 1) * LANES]
        h.append(h32.astype(BF16))
    h_all = jnp.concatenate(h, axis=0)
    for (_, dilation), w_ref, o_ref in zip(DIL_GROUPS, (wd0, wd1, wd2), (d0_ref, d1_ref, d2_ref)):
        hd = h_all if dilation == 1 else jnp.concatenate(
            [_residue_major(h_scr[t], dilation).astype(BF16) for t in tiles], axis=0)
        o_ref[...] = _dot(hd, w_ref[...]).astype(BF16)
    qs_ref[...] = (_dot(h_all, wqs[...]) * (LOG2E / math.sqrt(HEAD_DIM))).astype(BF16)
    k = _dot(h_all, wks[...]).astype(BF16)
    ks_ref[...] = k
    k_norm2 = jnp.max(_dot(jnp.square(k.astype(F32)).astype(BF16), head_sel[...]), axis=0,
                      keepdims=True)
    kn_ref[...] = jnp.broadcast_to(k_norm2, kn_ref.shape)
    vs_ref[...] = _dot(h_all, wvs[...]).astype(BF16)
    ga_ref[...] = _sigmoid(_dot(h_all, wga[...]) + bga[...])
    gb_ref[...] = _sigmoid(_dot(h_all, wgb[...]) + bgb[...])


def _proj(x, g, w_dil, w_sb, w_gate, b_gate):
    b, s, d = x.shape
    tm = PROJ_TILES * DIL_TILE
    tok = lambda w: pl.BlockSpec((None, tm, w), lambda bi, i: (bi, i, 0))
    const = lambda shape, col: pl.BlockSpec(shape, lambda bi, i: (0, col),
                                            pipeline_mode=pl.Buffered(1))
    cols = lambda a, n: [const((a.shape[0], a.shape[1] // n), c) for c in range(n)]
    dil_width = 3 * DIL_OUT_WIDTH
    out_shape = (
        (jax.ShapeDtypeStruct((b, s, dil_width), BF16),) * len(DIL_GROUPS)
        + (jax.ShapeDtypeStruct((b, s, SB_WIDTH), BF16),) * 3
        + (jax.ShapeDtypeStruct((b, s, D_MODEL), F32),) * 2
        + (jax.ShapeDtypeStruct((b, s // tm, SUBLANES, LANES), F32),))
    out_specs = ((tok(dil_width),) * len(DIL_GROUPS) + (tok(SB_WIDTH),) * 3
                 + (tok(D_MODEL),) * 2
                 + (pl.BlockSpec((None, None, SUBLANES, LANES), lambda bi, i: (bi, i, 0, 0)),))
    head_sel = jnp.asarray(np.arange(SB_WIDTH)[:, None] // HEAD_DIM == np.arange(LANES)[None, :],
                           BF16)
    return pl.pallas_call(
        _proj_kernel,
        grid=(b, s // tm),
        in_specs=([tok(d), const(g.shape, 0)] + cols(w_dil, len(DIL_GROUPS)) + cols(w_sb, 3)
                  + cols(w_gate, 2) + cols(b_gate, 2) + [const(head_sel.shape, 0)]),
        out_specs=out_specs,
        out_shape=out_shape,
        scratch_shapes=[pltpu.VMEM((d // LANES, DIL_TILE, LANES), F32)] * PROJ_TILES,
        compiler_params=pltpu.CompilerParams(
            dimension_semantics=("parallel", "parallel"), vmem_limit_bytes=VMEM_LIMIT),
        name="proj",
    )(x, g, *([w_dil] * len(DIL_GROUPS)), *([w_sb] * 3), *([w_gate] * 2), *([b_gate] * 2), head_sel)


DIL_MAX_QB = 16


def _alibi_slope(head):
    return 2.0 ** (-8.0 * (head + 1) / N_DIL_HEADS)


def _dil_kernel(q_ref, kp_ref, kc_ref, vp_ref, vc_ref, o_ref, lse_ref, *, group, dilation, n_steps,
                qb):
    n = pl.program_id(2)
    width = DIL_OUT_WIDTH
    qrows = qb * BLOCK
    n_res = q_ref.shape[1] if len(q_ref.shape) == 4 else 1
    res = range(n_res)
    take = lambda ref, r, n_rows: (ref[:, r] if len(ref.shape) == 4 else ref[...]).reshape(n_rows, width)
    q = [take(q_ref, r, qrows) for r in res]
    k = [jnp.concatenate([take(kp_ref, r, BLOCK), take(kc_ref, r, qrows)], axis=0) for r in res]
    v = [jnp.concatenate([take(vp_ref, r, BLOCK), take(vc_ref, r, qrows)], axis=0) for r in res]
    qi = lax.broadcasted_iota(jnp.int32, (BLOCK, 2 * BLOCK), 0)
    kj = lax.broadcasted_iota(jnp.int32, (BLOCK, 2 * BLOCK), 1)
    steps = qi + BLOCK - kj
    valid = (steps >= 0) & (steps <= n_steps)
    has_prev = (kj >= BLOCK) | (n > 0)
    dist = (steps * dilation).astype(F32)
    low = lax.broadcasted_iota(jnp.int32, (1, LANES), 1) < HEAD_DIM
    sel = (low, jnp.logical_not(low))
    work = [(r, p, j, e) for r in res for p in range(2) for j in range(qb) for e in range(2)]
    q_rows = lambda j: slice(j * BLOCK, (j + 1) * BLOCK)
    k_rows = lambda j: slice(j * BLOCK, (j + 2) * BLOCK)
    pair = lambda t, p: t[:, p * LANES:(p + 1) * LANES]
    qm = {(r, p, e): jnp.where(sel[e], pair(q[r], p), jnp.zeros((qrows, LANES), BF16))
          for r in res for p in range(2) for e in range(2)}
    bias = {(p, e): jnp.where(valid, -_alibi_slope(group * DIL_HEADS_PER_GROUP + 2 * p + e) * dist,
                              NEG_INF) for p in range(2) for e in range(2)}
    logits = {}
    for r in res:
        for p in range(2):
            for j in range(qb):
                both = _dot_nt(jnp.concatenate([qm[r, p, e][q_rows(j)] for e in range(2)], axis=0),
                               pair(k[r], p)[k_rows(j)])
                for e in range(2):
                    lg = both[e * BLOCK:(e + 1) * BLOCK] + bias[p, e]
                    logits[r, p, j, e] = jnp.where(has_prev, lg, NEG_INF) if j == 0 else lg
    ms = {w: jnp.max(logits[w], axis=-1, keepdims=True) for w in work}
    pr = {w: jnp.exp(logits[w] - ms[w]).astype(BF16) for w in work}
    ones = jnp.ones((2 * BLOCK, LANES), BF16)
    t = {(r, p, j): _dot(jnp.concatenate([pr[r, p, j, e] for e in range(2)], axis=0),
                         jnp.concatenate([pair(v[r], p)[k_rows(j)], ones], axis=1))
         for r in res for p in range(2) for j in range(qb)}
    for r in res:
        for j in range(qb):
            o_pairs, lse_pairs = [], []
            for p in range(2):
                head = [t[r, p, j][e * BLOCK:(e + 1) * BLOCK] for e in range(2)]
                num = jnp.where(low, head[0][:, :LANES], head[1][:, :LANES])
                den = jnp.where(low, head[0][:, LANES:], head[1][:, LANES:])
                o_pairs.append(num / den)
                lse_pairs.append(jnp.where(low, ms[r, p, j, 0], ms[r, p, j, 1]) + jnp.log(den))
            for ref, pairs in ((o_ref, o_pairs), (lse_ref, lse_pairs)):
                val = jnp.concatenate(pairs, axis=1)
                if len(ref.shape) == 4:
                    rows = ref.shape[2]
                    tiles = slice(j * BLOCK // rows, (j + 1) * BLOCK // rows)
                    ref[tiles, r] = val.reshape(BLOCK // rows, rows, width)
                else:
                    ref[j * BLOCK:(j + 1) * BLOCK, :] = val


def _dilated_group(qkv, group):
    window, dilation = DIL_GROUPS[group]
    b, s, _ = qkv.shape
    width = DIL_OUT_WIDTH
    qb = min(DIL_MAX_QB, s // dilation // BLOCK)
    n_res = min(dilation, DIL_MAX_QB // qb)
    qrows = qb * BLOCK
    n_grid = s // dilation // qrows
    first_prev = lambda n: jnp.maximum(n * qb - 1, 0)
    if dilation == 1:
        view = qkv
        cur = lambda c: pl.BlockSpec((None, qrows, width), lambda bi, r, n: (bi, n, c))
        prev = lambda c: pl.BlockSpec((None, BLOCK, width), lambda bi, r, n: (bi, first_prev(n), c))
        shp = jax.ShapeDtypeStruct((b, s, width), F32)
    else:
        rows = DIL_TILE // dilation
        view = qkv.reshape(b, s // DIL_TILE, dilation, rows, 3 * width)
        cur = lambda c: pl.BlockSpec((None, qrows // rows, n_res, rows, width),
                                     lambda bi, r, n: (bi, n, r, 0, c))
        prev = lambda c: pl.BlockSpec((None, BLOCK // rows, n_res, rows, width),
                                      lambda bi, r, n: (bi, first_prev(n), r, 0, c))
        shp = jax.ShapeDtypeStruct((b, s // DIL_TILE, dilation, rows, width), F32)
    return pl.pallas_call(
        functools.partial(_dil_kernel, group=group, dilation=dilation,
                          n_steps=window // dilation, qb=qb),
        grid=(b, dilation // n_res, n_grid),
        in_specs=[cur(0), prev(1), cur(1), prev(2), cur(2)],
        out_specs=(cur(0), cur(0)),
        out_shape=(shp, shp),
        compiler_params=pltpu.CompilerParams(
            dimension_semantics=("parallel", "parallel", "parallel"),
            vmem_limit_bytes=VMEM_LIMIT),
        name=f"dilated_d{dilation}",
    )(view, view, view, view, view)


SB_TQ = 64
SB_TK = 256
SB_WINDOW = 256
SB_DEAD = 136.0
SB_NORM_SLACK = 1.0 + 2.0 ** -8
SB_DOT_SLACK = 1.001
SB_BLOCKS_PER_ITER = 10
HEADS_PER_STEP = LANES // HEAD_DIM


def _softplus2(z):
    return jnp.maximum(z, 0.0) + jnp.log2(1.0 + jnp.exp2(-jnp.abs(z)))


def _split_hi_lo(sp):
    parts = []
    for c in range(sp.shape[1] // LANES):
        sp_c = sp[:, c * LANES:(c + 1) * LANES]
        hi = sp_c.astype(BF16)
        parts += [hi, (sp_c - hi.astype(F32)).astype(BF16)]
    return jnp.concatenate(parts, axis=1)


def _sb_suffix_sums(hl, tri):
    return [_dot(hl[:, 2 * c * LANES:2 * (c + 1) * LANES], tri)
            for c in range(hl.shape[1] // (2 * LANES))]


def _sb_weights(z, sums, carry):
    las = [None] * len(sums)
    for c in reversed(range(len(sums))):
        las[c] = z[:, c * LANES:(c + 1) * LANES] - sums[c][:, :LANES] - carry
        carry = carry + sums[c][:, LANES:]
    return jnp.exp2(jnp.concatenate(las, axis=1)).astype(BF16), carry


def _sb_kernel(q_ref, k_ref, v_ref, tri_ref, kn_ref, o_ref, acc_ref, carry_ref):
    s = q_ref.shape[0]
    n_q = s // SB_TQ
    heads = range(HEADS_PER_STEP)
    tri = tri_ref[...]
    lane = lax.broadcasted_iota(jnp.int32, (1, LANES), 1)
    own = [(lane >= h * HEAD_DIM) & (lane < (h + 1) * HEAD_DIM) for h in heads]

    kn = jnp.max(kn_ref[...], axis=(0, 1), keepdims=True).reshape(1, LANES)
    k_max = [jnp.sqrt(jnp.max(jnp.where(lane == HEADS_PER_STEP * pl.program_id(1) + h, kn, 0.0)))
             * SB_NORM_SLACK for h in heads]

    col_minus_row = (lax.broadcasted_iota(jnp.int32, (SB_TQ, SB_WINDOW), 1)
                     - lax.broadcasted_iota(jnp.int32, (SB_TQ, SB_WINDOW), 0))
    lead = SB_WINDOW - SB_TQ

    def query_blocks(blocks):
        items = [(slot, h) for slot in range(len(blocks)) for h in heads]
        rows, start, causal, q, z_max = {}, {}, {}, {}, {}
        for slot, qi in enumerate(blocks):
            first = qi * SB_TQ
            if isinstance(qi, int):
                start[slot] = max(first - lead, 0)
                rows[slot] = pl.ds(first, SB_TQ)
            else:
                start[slot] = pl.multiple_of(jnp.maximum(first - lead, 0), SB_TQ)
                rows[slot] = pl.ds(pl.multiple_of(first, SB_TQ), SB_TQ)
            causal[slot] = col_minus_row < (first - start[slot] if isinstance(qi, int) else lead)
            q_pair = q_ref[rows[slot], :]
            for h in heads:
                q[slot, h] = jnp.where(own[h], q_pair, jnp.zeros_like(q_pair))
                z_max[slot, h] = (jnp.sqrt(jnp.sum(jnp.square(q[slot, h].astype(F32)), axis=1,
                                                   keepdims=True)) * (k_max[h] * SB_DOT_SLACK) + 1.0)
        z = {}
        for slot in range(len(blocks)):
            both = _dot_nt(jnp.concatenate([q[slot, h] for h in heads], axis=0),
                           k_ref[pl.ds(start[slot], SB_WINDOW), :])
            for h in heads:
                z[slot, h] = jnp.where(causal[slot], both[h * SB_TQ:(h + 1) * SB_TQ], NEG_INF)
        hl = {w: _split_hi_lo(_softplus2(z[w])) for w in items}
        n_chunks = SB_WINDOW // LANES
        stacked = _dot(jnp.concatenate(
            [hl[w][:, 2 * c * LANES:2 * (c + 1) * LANES] for w in items for c in range(n_chunks)],
            axis=0), tri)
        sums = {w: [stacked[(i * n_chunks + c) * SB_TQ:(i * n_chunks + c + 1) * SB_TQ]
                    for c in range(n_chunks)] for i, w in enumerate(items)}
        a, carry = {}, {}
        for w in items:
            a[w], carry[w] = _sb_weights(z[w], sums[w], jnp.zeros((SB_TQ, LANES), F32))
        for slot in range(len(blocks)):
            both = _dot(jnp.concatenate([a[slot, h] for h in heads], axis=0),
                        v_ref[pl.ds(start[slot], SB_WINDOW), :])
            for h in heads:
                acc_ref[slot, h] = both[h * SB_TQ:(h + 1) * SB_TQ]
                carry_ref[slot, h] = carry[slot, h]

        def slack(slot, carries):
            per_head = [carries[h] - z_max[slot, h] for h in heads]
            return jnp.min(jnp.minimum(per_head[0], per_head[1]), keepdims=True)

        slacks = [slack(slot, [carry[slot, h] for h in heads]) for slot in range(len(blocks))]

        @pl.when(functools.reduce(jnp.minimum, slacks)[0, 0] < SB_DEAD)
        def _():
            for slot in range(len(blocks)):
                def cond(state):
                    ke, live = state
                    return (ke > 0) & live

                def body(state):
                    ke, _ = state
                    ks = pl.multiple_of(jnp.maximum(ke - SB_TK, 0), SB_TQ)
                    fresh = (lax.broadcasted_iota(jnp.int32, (SB_TQ, SB_TK), 1) + ks) < ke
                    carries = []
                    for h in heads:
                        zt = jnp.where(fresh, _dot_nt(q[slot, h], k_ref[pl.ds(ks, SB_TK), :]),
                                       NEG_INF)
                        at, ct = _sb_weights(
                            zt, _sb_suffix_sums(_split_hi_lo(_softplus2(zt)), tri),
                            carry_ref[slot, h])
                        acc_ref[slot, h] += _dot(at, v_ref[pl.ds(ks, SB_TK), :])
                        carry_ref[slot, h] = ct
                        carries.append(ct)
                    return ks, slack(slot, carries)[0, 0] < SB_DEAD

                lax.while_loop(cond, body, (jnp.asarray(start[slot], jnp.int32),
                                            slacks[slot][0, 0] < SB_DEAD))

        for slot in range(len(blocks)):
            o_ref[rows[slot], :] = jnp.where(own[0], acc_ref[slot, 0],
                                             acc_ref[slot, 1]).astype(o_ref.dtype)

    n_lead = lead // SB_TQ
    query_blocks(list(range(n_lead)))
    n_groups = (n_q - n_lead) // SB_BLOCKS_PER_ITER

    def group(i, _):
        query_blocks([n_lead + i * SB_BLOCKS_PER_ITER + j for j in range(SB_BLOCKS_PER_ITER)])
        return 0

    lax.fori_loop(0, n_groups, group, 0)
    rest = list(range(n_lead + n_groups * SB_BLOCKS_PER_ITER, n_q))
    if rest:
        query_blocks(rest)


def _sb_attention(qs, ks, vs, k_norm2):
    b, s, _ = qs.shape
    assert s % SB_TQ == 0 and s >= SB_WINDOW and SB_WINDOW % LANES == 0 and HEADS_PER_STEP == 2
    half = np.concatenate([np.arange(LANES)[:, None] >= np.arange(LANES)[None, :],
                           np.ones((LANES, LANES), bool)], axis=1)
    tri = jnp.asarray(np.concatenate([half, half], axis=0), BF16)
    nh = HEADS_PER_STEP
    seq = pl.BlockSpec((None, s, LANES), lambda bi, p: (bi, 0, p))
    return pl.pallas_call(
        _sb_kernel,
        grid=(b, SB_WIDTH // LANES),
        in_specs=[seq, seq, seq, pl.BlockSpec(tri.shape, lambda bi, p: (0, 0)),
                  pl.BlockSpec((None,) + k_norm2.shape[1:], lambda bi, p: (bi, 0, 0, 0))],
        out_specs=seq,
        out_shape=jax.ShapeDtypeStruct((b, s, SB_WIDTH), BF16),
        scratch_shapes=[pltpu.VMEM((SB_BLOCKS_PER_ITER, nh, SB_TQ, LANES), F32)] * 2,
        compiler_params=pltpu.CompilerParams(
            dimension_semantics=("parallel", "parallel"), vmem_limit_bytes=VMEM_LIMIT),
        name="stick_breaking",
    )(qs, ks, vs, tri, k_norm2)


TAIL_TILES = 2
MLP_FF_CHUNK = 1024


def _token_major(ref, scr):
    d, rows, _ = ref.shape
    for c in range(scr.shape[0]):
        for r in range(d):
            scr[c, pl.ds(r, rows, stride=d), :] = ref[r, :, c * LANES:(c + 1) * LANES]
    return jnp.concatenate([scr[c] for c in range(scr.shape[0])], axis=1)


def _tail_kernel(x_ref, o0, o1, o2, l0, l1, l2, ob_ref, ga_ref, gb_ref,
                 wud, wus, wout, g_mlp, w1, w2, g_out, out_ref, *scratch, final_norm):
    tiles = range(TAIL_TILES)
    rows = [slice(t * DIL_TILE, (t + 1) * DIL_TILE) for t in tiles]
    o_a = []
    for t in tiles:
        scr = scratch[4 * t:4 * t + 4]
        outs = [o0[rows[t], :], _token_major(o1.at[t], scr[0]), _token_major(o2.at[t], scr[1])]
        lse = [l0[rows[t], :], _token_major(l1.at[t], scr[2]), _token_major(l2.at[t], scr[3])]
        m = jnp.maximum(jnp.maximum(lse[0], lse[1]), lse[2])
        e = [jnp.exp(l - m) for l in lse]
        o_a.append((e[0] * outs[0] + e[1] * outs[1] + e[2] * outs[2]) / (e[0] + e[1] + e[2]))
    up_a = [_dot(o_a[t].astype(BF16), wud[...]) for t in tiles]
    up_b = [_dot(ob_ref[rows[t], :], wus[...]) for t in tiles]
    merged = [ga_ref[rows[t], :] * up_a[t] + gb_ref[rows[t], :] * up_b[t] for t in tiles]
    y = [x_ref[rows[t], :] + _dot(merged[t].astype(BF16), wout[...]) for t in tiles]
    h2 = [_rms(y[t], g_mlp[...]).astype(BF16) for t in tiles]
    for c in range(D_FF // MLP_FF_CHUNK):
        cs = slice(c * MLP_FF_CHUNK, (c + 1) * MLP_FF_CHUNK)
        u = [jnp.maximum(_dot(h2[t], w1[:, cs]), 0.0) for t in tiles]
        y = [y[t] + _dot((u[t] * u[t]).astype(BF16), w2[cs, :]) for t in tiles]
    for t in tiles:
        out_ref[rows[t], :] = _rms(y[t], g_out[...]) if final_norm else y[t]


def _tail(x, oas, lses, o_b, g_a, g_b, wud, wus, wout, g_mlp, w1, w2, g_out, final_norm):
    b, s, d = x.shape
    tm = TAIL_TILES * DIL_TILE
    tok = lambda w: pl.BlockSpec((None, tm, w), lambda bi, i: (bi, i, 0))
    full = lambda a: pl.BlockSpec(a.shape, lambda bi, i: (0,) * a.ndim,
                                  pipeline_mode=pl.Buffered(1))

    def dil(a):
        if a.ndim == 3:
            return tok(DIL_OUT_WIDTH)
        return pl.BlockSpec((None, TAIL_TILES) + a.shape[2:], lambda bi, i: (bi, i, 0, 0, 0))

    weights = (wud, wus, wout, g_mlp, w1, w2, g_out)
    return pl.pallas_call(
        functools.partial(_tail_kernel, final_norm=final_norm),
        grid=(b, s // tm),
        in_specs=[tok(d)] + [dil(a) for a in (*oas, *lses)] + [tok(SB_WIDTH), tok(d), tok(d)]
                 + [full(a) for a in weights],
        out_specs=tok(d),
        out_shape=jax.ShapeDtypeStruct((b, s, d), F32),
        scratch_shapes=[pltpu.VMEM((DIL_OUT_WIDTH // LANES, DIL_TILE, LANES), F32)]
                       * (4 * TAIL_TILES),
        compiler_params=pltpu.CompilerParams(
            dimension_semantics=("parallel", "parallel"), vmem_limit_bytes=VMEM_LIMIT),
        name="tail",
    )(x, *oas, *lses, o_b, g_a, g_b, *weights)


def kernel(x, norm_mix_g, w_in, b_gate, w_up_dil, w_up_sb, w_out, norm_mlp_g, w_mlp_in,
           w_mlp_out, norm_final_g):
    depth = w_in.shape[0]
    row = lambda v: v.reshape(1, -1)
    for layer in range(depth):
        w = w_in[layer]
        c0 = 3 * DIL_WIDTH
        c1 = c0 + 3 * SB_WIDTH
        dil_cols = lambda i, grp: w[:, i * DIL_WIDTH + grp * DIL_OUT_WIDTH:
                                    i * DIL_WIDTH + (grp + 1) * DIL_OUT_WIDTH]
        w_dil = jnp.concatenate(
            [part for grp in range(len(DIL_GROUPS))
             for part in (dil_cols(0, grp) * (1.0 / math.sqrt(HEAD_DIM)), dil_cols(1, grp),
                          dil_cols(2, grp))], axis=1).astype(BF16)
        *qkv, qs, ks, vs, g_a, g_b, k_norm2 = _proj(
            x, row(norm_mix_g[layer]), w_dil, w[:, c0:c1].astype(BF16), w[:, c1:].astype(BF16),
            row(b_gate[layer]))
        dil = [_dilated_group(qkv[grp], grp) for grp in range(len(DIL_GROUPS))]
        o_b = _sb_attention(qs, ks, vs, k_norm2)
        x = _tail(x, [o for o, _ in dil], [l for _, l in dil], o_b, g_a, g_b,
                  w_up_dil[layer].astype(BF16), w_up_sb[layer].astype(BF16),
                  w_out[layer].astype(BF16), row(norm_mlp_g[layer]),
                  w_mlp_in[layer].astype(BF16), w_mlp_out[layer].astype(BF16),
                  row(norm_final_g), final_norm=layer == depth - 1)
    return x
```

```python
import functools
import math

import jax
import jax.numpy as jnp
import numpy as np
from jax import lax
from jax.experimental import pallas as pl
from jax.experimental.pallas import tpu as pltpu

D_MODEL = 1024
HEAD_DIM = 64
DIL_GROUPS = ((128, 1), (512, 4), (2048, 16))
DIL_HEADS_PER_GROUP = 4
N_DIL_HEADS = DIL_HEADS_PER_GROUP * len(DIL_GROUPS)
N_SB_HEADS = 8
DIL_WIDTH = N_DIL_HEADS * HEAD_DIM
DIL_OUT_WIDTH = DIL_HEADS_PER_GROUP * HEAD_DIM
SB_WIDTH = N_SB_HEADS * HEAD_DIM
D_FF = 4 * D_MODEL
BLOCK = 128
RMS_EPS = 1e-6
NEG_INF = -1e30
LOG2E = 1.4426950408889634
LANES = 128
SUBLANES = 8

VMEM_LIMIT = 56 * 1024 * 1024

BF16 = jnp.bfloat16
F32 = jnp.float32


def _dot(a, b):
    return jnp.dot(a, b, preferred_element_type=F32)


def _dot_nt(a, b):
    return lax.dot_general(a, b, (((1,), (1,)), ((), ())), preferred_element_type=F32)


def _rms(x, g):
    return x * lax.rsqrt(jnp.mean(x * x, axis=-1, keepdims=True) + RMS_EPS) * g


def _sigmoid(x):
    return 1.0 / (1.0 + jnp.exp(-x))


DIL_TILE = 256
PROJ_TILES = 2


def _residue_major(h_ref, d):
    rows = DIL_TILE // d
    return jnp.concatenate(
        [jnp.concatenate([h_ref[c, pl.ds(r, rows, stride=d), :] for r in range(d)], axis=0)
         for c in range(h_ref.shape[0])], axis=1)


def _proj_kernel(x_ref, g_ref, wd0, wd1, wd2, wqs, wks, wvs, wga, wgb, bga, bgb, head_sel,
                 d0_ref, d1_ref, d2_ref, qs_ref, ks_ref, vs_ref, ga_ref, gb_ref, kn_ref, *h_scr):
    tiles = range(PROJ_TILES)
    rows = [slice(t * DIL_TILE, (t + 1) * DIL_TILE) for t in tiles]
    h = []
    for t in tiles:
        h32 = _rms(x_ref[rows[t], :], g_ref[...])
        for c in range(h_scr[t].shape[0]):
            h_scr[t][c] = h32[:, c * LANES:(c + 1) * LANES]
        h.append(h32.astype(BF16))
    h_all = jnp.concatenate(h, axis=0)
    for (_, dilation), w_ref, o_ref in zip(DIL_GROUPS, (wd0, wd1, wd2), (d0_ref, d1_ref, d2_ref)):
        hd = h_all if dilation == 1 else jnp.concatenate(
            [_residue_major(h_scr[t], dilation).astype(BF16) for t in tiles], axis=0)
        o_ref[...] = _dot(hd, w_ref[...]).astype(BF16)
    qs_ref[...] = (_dot(h_all, wqs[...]) * (LOG2E / math.sqrt(HEAD_DIM))).astype(BF16)
    k = _dot(h_all, wks[...]).astype(BF16)
    ks_ref[...] = k
    k_norm2 = jnp.max(_dot(jnp.square(k.astype(F32)).astype(BF16), head_sel[...]), axis=0,
                      keepdims=True)
    kn_ref[...] = jnp.broadcast_to(k_norm2, kn_ref.shape)
    vs_ref[...] = _dot(h_all, wvs[...]).astype(BF16)
    ga_ref[...] = _sigmoid(_dot(h_all, wga[...]) + bga[...])
    gb_ref[...] = _sigmoid(_dot(h_all, wgb[...]) + bgb[...])


def _proj(x, g, w_dil, w_sb, w_gate, b_gate):
    b, s, d = x.shape
    tm = PROJ_TILES * DIL_TILE
    tok = lambda w: pl.BlockSpec((None, tm, w), lambda bi, i: (bi, i, 0))
    const = lambda shape, col: pl.BlockSpec(shape, lambda bi, i: (0, col),
                                            pipeline_mode=pl.Buffered(1))
    cols = lambda a, n: [const((a.shape[0], a.shape[1] // n), c) for c in range(n)]
    dil_width = 3 * DIL_OUT_WIDTH
    out_shape = (
        (jax.ShapeDtypeStruct((b, s, dil_width), BF16),) * len(DIL_GROUPS)
        + (jax.ShapeDtypeStruct((b, s, SB_WIDTH), BF16),) * 3
        + (jax.ShapeDtypeStruct((b, s, D_MODEL), F32),) * 2
        + (jax.ShapeDtypeStruct((b, s // tm, SUBLANES, LANES), F32),))
    out_specs = ((tok(dil_width),) * len(DIL_GROUPS) + (tok(SB_WIDTH),) * 3
                 + (tok(D_MODEL),) * 2
                 + (pl.BlockSpec((None, None, SUBLANES, LANES), lambda bi, i: (bi, i, 0, 0)),))
    head_sel = jnp.asarray(np.arange(SB_WIDTH)[:, None] // HEAD_DIM == np.arange(LANES)[None, :],
                           BF16)
    return pl.pallas_call(
        _proj_kernel,
        grid=(b, s // tm),
        in_specs=([tok(d), const(g.shape, 0)] + cols(w_dil, len(DIL_GROUPS)) + cols(w_sb, 3)
                  + cols(w_gate, 2) + cols(b_gate, 2) + [const(head_sel.shape, 0)]),
        out_specs=out_specs,
        out_shape=out_shape,
        scratch_shapes=[pltpu.VMEM((d // LANES, DIL_TILE, LANES), F32)] * PROJ_TILES,
        compiler_params=pltpu.CompilerParams(
            dimension_semantics=("parallel", "parallel"), vmem_limit_bytes=VMEM_LIMIT),
        name="proj",
    )(x, g, *([w_dil] * len(DIL_GROUPS)), *([w_sb] * 3), *([w_gate] * 2), *([b_gate] * 2), head_sel)


DIL_MAX_QB = 16


def _alibi_slope(head):
    return 2.0 ** (-8.0 * (head + 1) / N_DIL_HEADS)


def _dil_kernel(q_ref, kp_ref, kc_ref, vp_ref, vc_ref, o_ref, lse_ref, *, group, dilation, n_steps,
                qb):
    n = pl.program_id(2)
    width = DIL_OUT_WIDTH
    qrows = qb * BLOCK
    n_res = q_ref.shape[1] if len(q_ref.shape) == 4 else 1
    res = range(n_res)
    take = lambda ref, r, n_rows: (ref[:, r] if len(ref.shape) == 4 else ref[...]).reshape(n_rows, width)
    q = [take(q_ref, r, qrows) for r in res]
    k = [jnp.concatenate([take(kp_ref, r, BLOCK), take(kc_ref, r, qrows)], axis=0) for r in res]
    v = [jnp.concatenate([take(vp_ref, r, BLOCK), take(vc_ref, r, qrows)], axis=0) for r in res]
    qi = lax.broadcasted_iota(jnp.int32, (BLOCK, 2 * BLOCK), 0)
    kj = lax.broadcasted_iota(jnp.int32, (BLOCK, 2 * BLOCK), 1)
    steps = qi + BLOCK - kj
    valid = (steps >= 0) & (steps <= n_steps)
    has_prev = (kj >= BLOCK) | (n > 0)
    dist = (steps * dilation).astype(F32)
    low = lax.broadcasted_iota(jnp.int32, (1, LANES), 1) < HEAD_DIM
    sel = (low, jnp.logical_not(low))
    work = [(r, p, j, e) for r in res for p in range(2) for j in range(qb) for e in range(2)]
    q_rows = lambda j: slice(j * BLOCK, (j + 1) * BLOCK)
    k_rows = lambda j: slice(j * BLOCK, (j + 2) * BLOCK)
    pair = lambda t, p: t[:, p * LANES:(p + 1) * LANES]
    qm = {(r, p, e): jnp.where(sel[e], pair(q[r], p), jnp.zeros((qrows, LANES), BF16))
          for r in res for p in range(2) for e in range(2)}
    bias = {(p, e): jnp.where(valid, -_alibi_slope(group * DIL_HEADS_PER_GROUP + 2 * p + e) * dist,
                              NEG_INF) for p in range(2) for e in range(2)}
    logits = {}
    for r in res:
        for p in range(2):
            for j in range(qb):
                both = _dot_nt(jnp.concatenate([qm[r, p, e][q_rows(j)] for e in range(2)], axis=0),
                               pair(k[r], p)[k_rows(j)])
                for e in range(2):
                    lg = both[e * BLOCK:(e + 1) * BLOCK] + bias[p, e]
                    logits[r, p, j, e] = jnp.where(has_prev, lg, NEG_INF) if j == 0 else lg
    ms = {w: jnp.max(logits[w], axis=-1, keepdims=True) for w in work}
    pr = {w: jnp.exp(logits[w] - ms[w]).astype(BF16) for w in work}
    ones = jnp.ones((2 * BLOCK, LANES), BF16)
    t = {(r, p, j): _dot(jnp.concatenate([pr[r, p, j, e] for e in range(2)], axis=0),
                         jnp.concatenate([pair(v[r], p)[k_rows(j)], ones], axis=1))
         for r in res for p in range(2) for j in range(qb)}
    for r in res:
        for j in range(qb):
            o_pairs, lse_pairs = [], []
            for p in range(2):
                head = [t[r, p, j][e * BLOCK:(e + 1) * BLOCK] for e in range(2)]
                num = jnp.where(low, head[0][:, :LANES], head[1][:, :LANES])
                den = jnp.where(low, head[0][:, LANES:], head[1][:, LANES:])
                o_pairs.append(num / den)
                lse_pairs.append(jnp.where(low, ms[r, p, j, 0], ms[r, p, j, 1]) + jnp.log(den))
            for ref, pairs in ((o_ref, o_pairs), (lse_ref, lse_pairs)):
                val = jnp.concatenate(pairs, axis=1)
                if len(ref.shape) == 4:
                    rows = ref.shape[2]
                    tiles = slice(j * BLOCK // rows, (j + 1) * BLOCK // rows)
                    ref[tiles, r] = val.reshape(BLOCK // rows, rows, width)
                else:
                    ref[j * BLOCK:(j + 1) * BLOCK, :] = val


def _dilated_group(qkv, group):
    window, dilation = DIL_GROUPS[group]
    b, s, _ = qkv.shape
    width = DIL_OUT_WIDTH
    qb = min(DIL_MAX_QB, s // dilation // BLOCK)
    n_res = min(dilation, DIL_MAX_QB // qb)
    qrows = qb * BLOCK
    n_grid = s // dilation // qrows
    first_prev = lambda n: jnp.maximum(n * qb - 1, 0)
    if dilation == 1:
        view = qkv
        cur = lambda c: pl.BlockSpec((None, qrows, width), lambda bi, r, n: (bi, n, c))
        prev = lambda c: pl.BlockSpec((None, BLOCK, width), lambda bi, r, n: (bi, first_prev(n), c))
        shp = jax.ShapeDtypeStruct((b, s, width), F32)
    else:
        rows = DIL_TILE // dilation
        view = qkv.reshape(b, s // DIL_TILE, dilation, rows, 3 * width)
        cur = lambda c: pl.BlockSpec((None, qrows // rows, n_res, rows, width),
                                     lambda bi, r, n: (bi, n, r, 0, c))
        prev = lambda c: pl.BlockSpec((None, BLOCK // rows, n_res, rows, width),
                                      lambda bi, r, n: (bi, first_prev(n), r, 0, c))
        shp = jax.ShapeDtypeStruct((b, s // DIL_TILE, dilation, rows, width), F32)
    return pl.pallas_call(
        functools.partial(_dil_kernel, group=group, dilation=dilation,
                          n_steps=window // dilation, qb=qb),
        grid=(b, dilation // n_res, n_grid),
        in_specs=[cur(0), prev(1), cur(1), prev(2), cur(2)],
        out_specs=(cur(0), cur(0)),
        out_shape=(shp, shp),
        compiler_params=pltpu.CompilerParams(
            dimension_semantics=("parallel", "parallel", "parallel"),
            vmem_limit_bytes=VMEM_LIMIT),
        name=f"dilated_d{dilation}",
    )(view, view, view, view, view)


SB_TQ = 64
SB_TK = 256
SB_WINDOW = 256
SB_DEAD = 136.0
SB_NORM_SLACK = 1.0 + 2.0 ** -8
SB_DOT_SLACK = 1.001
SB_BLOCKS_PER_ITER = 10
HEADS_PER_STEP = LANES // HEAD_DIM


def _softplus2(z):
    return jnp.maximum(z, 0.0) + jnp.log2(1.0 + jnp.exp2(-jnp.abs(z)))


def _split_hi_lo(sp):
    parts = []
    for c in range(sp.shape[1] // LANES):
        sp_c = sp[:, c * LANES:(c + 1) * LANES]
        hi = sp_c.astype(BF16)
        parts += [hi, (sp_c - hi.astype(F32)).astype(BF16)]
    return jnp.concatenate(parts, axis=1)


def _sb_suffix_sums(hl, tri):
    return [_dot(hl[:, 2 * c * LANES:2 * (c + 1) * LANES], tri)
            for c in range(hl.shape[1] // (2 * LANES))]


def _sb_weights(z, sums, carry):
    las = [None] * len(sums)
    for c in reversed(range(len(sums))):
        las[c] = z[:, c * LANES:(c + 1) * LANES] - sums[c][:, :LANES] - carry
        carry = carry + sums[c][:, LANES:]
    return jnp.exp2(jnp.concatenate(las, axis=1)).astype(BF16), carry


def _sb_kernel(q_ref, k_ref, v_ref, tri_ref, kn_ref, o_ref, acc_ref, carry_ref):
    s = q_ref.shape[0]
    n_q = s // SB_TQ
    heads = range(HEADS_PER_STEP)
    tri = tri_ref[...]
    lane = lax.broadcasted_iota(jnp.int32, (1, LANES), 1)
    own = [(lane >= h * HEAD_DIM) & (lane < (h + 1) * HEAD_DIM) for h in heads]

    kn = jnp.max(kn_ref[...], axis=(0, 1), keepdims=True).reshape(1, LANES)
    k_max = [jnp.sqrt(jnp.max(jnp.where(lane == HEADS_PER_STEP * pl.program_id(1) + h, kn, 0.0)))
             * SB_NORM_SLACK for h in heads]

    col_minus_row = (lax.broadcasted_iota(jnp.int32, (SB_TQ, SB_WINDOW), 1)
                     - lax.broadcasted_iota(jnp.int32, (SB_TQ, SB_WINDOW), 0))
    lead = SB_WINDOW - SB_TQ

    def query_blocks(blocks):
        items = [(slot, h) for slot in range(len(blocks)) for h in heads]
        rows, start, causal, q, z_max = {}, {}, {}, {}, {}
        for slot, qi in enumerate(blocks):
            first = qi * SB_TQ
            if isinstance(qi, int):
                start[slot] = max(first - lead, 0)
                rows[slot] = pl.ds(first, SB_TQ)
            else:
                start[slot] = pl.multiple_of(jnp.maximum(first - lead, 0), SB_TQ)
                rows[slot] = pl.ds(pl.multiple_of(first, SB_TQ), SB_TQ)
            causal[slot] = col_minus_row < (first - start[slot] if isinstance(qi, int) else lead)
            q_pair = q_ref[rows[slot], :]
            for h in heads:
                q[slot, h] = jnp.where(own[h], q_pair, jnp.zeros_like(q_pair))
                z_max[slot, h] = (jnp.sqrt(jnp.sum(jnp.square(q[slot, h].astype(F32)), axis=1,
                                                   keepdims=True)) * (k_max[h] * SB_DOT_SLACK) + 1.0)
        z = {}
        for slot in range(len(blocks)):
            both = _dot_nt(jnp.concatenate([q[slot, h] for h in heads], axis=0),
                           k_ref[pl.ds(start[slot], SB_WINDOW), :])
            for h in heads:
                z[slot, h] = jnp.where(causal[slot], both[h * SB_TQ:(h + 1) * SB_TQ], NEG_INF)
        hl = {w: _split_hi_lo(_softplus2(z[w])) for w in items}
        n_chunks = SB_WINDOW // LANES
        stacked = _dot(jnp.concatenate(
            [hl[w][:, 2 * c * LANES:2 * (c + 1) * LANES] for w in items for c in range(n_chunks)],
            axis=0), tri)
        sums = {w: [stacked[(i * n_chunks + c) * SB_TQ:(i * n_chunks + c + 1) * SB_TQ]
                    for c in range(n_chunks)] for i, w in enumerate(items)}
        a, carry = {}, {}
        for w in items:
            a[w], carry[w] = _sb_weights(z[w], sums[w], jnp.zeros((SB_TQ, LANES), F32))
        for slot in range(len(blocks)):
            both = _dot(jnp.concatenate([a[slot, h] for h in heads], axis=0),
                        v_ref[pl.ds(start[slot], SB_WINDOW), :])
            for h in heads:
                acc_ref[slot, h] = both[h * SB_TQ:(h + 1) * SB_TQ]
                carry_ref[slot, h] = carry[slot, h]

        def slack(slot, carries):
            per_head = [carries[h] - z_max[slot, h] for h in heads]
            return jnp.min(jnp.minimum(per_head[0], per_head[1]), keepdims=True)

        slacks = [slack(slot, [carry[slot, h] for h in heads]) for slot in range(len(blocks))]

        @pl.when(functools.reduce(jnp.minimum, slacks)[0, 0] < SB_DEAD)
        def _():
            for slot in range(len(blocks)):
                def cond(state):
                    ke, live = state
                    return (ke > 0) & live

                def body(state):
                    ke, _ = state
                    ks = pl.multiple_of(jnp.maximum(ke - SB_TK, 0), SB_TQ)
                    fresh = (lax.broadcasted_iota(jnp.int32, (SB_TQ, SB_TK), 1) + ks) < ke
                    carries = []
                    for h in heads:
                        zt = jnp.where(fresh, _dot_nt(q[slot, h], k_ref[pl.ds(ks, SB_TK), :]),
                                       NEG_INF)
                        at, ct = _sb_weights(
                            zt, _sb_suffix_sums(_split_hi_lo(_softplus2(zt)), tri),
                            carry_ref[slot, h])
                        acc_ref[slot, h] += _dot(at, v_ref[pl.ds(ks, SB_TK), :])
                        carry_ref[slot, h] = ct
                        carries.append(ct)
                    return ks, slack(slot, carries)[0, 0] < SB_DEAD

                lax.while_loop(cond, body, (jnp.asarray(start[slot], jnp.int32),
                                            slacks[slot][0, 0] < SB_DEAD))

        for slot in range(len(blocks)):
            o_ref[rows[slot], :] = jnp.where(own[0], acc_ref[slot, 0],
                                             acc_ref[slot, 1]).astype(o_ref.dtype)

    n_lead = lead // SB_TQ
    query_blocks(list(range(n_lead)))
    n_groups = (n_q - n_lead) // SB_BLOCKS_PER_ITER

    def group(i, _):
        query_blocks([n_lead + i * SB_BLOCKS_PER_ITER + j for j in range(SB_BLOCKS_PER_ITER)])
        return 0

    lax.fori_loop(0, n_groups, group, 0)
    rest = list(range(n_lead + n_groups * SB_BLOCKS_PER_ITER, n_q))
    if rest:
        query_blocks(rest)


def _sb_attention(qs, ks, vs, k_norm2):
    b, s, _ = qs.shape
    assert s % SB_TQ == 0 and s >= SB_WINDOW and SB_WINDOW % LANES == 0 and HEADS_PER_STEP == 2
    half = np.concatenate([np.arange(LANES)[:, None] >= np.arange(LANES)[None, :],
                           np.ones((LANES, LANES), bool)], axis=1)
    tri = jnp.asarray(np.concatenate([half, half], axis=0), BF16)
    nh = HEADS_PER_STEP
    seq = pl.BlockSpec((None, s, LANES), lambda bi, p: (bi, 0, p))
    return pl.pallas_call(
        _sb_kernel,
        grid=(b, SB_WIDTH // LANES),
        in_specs=[seq, seq, seq, pl.BlockSpec(tri.shape, lambda bi, p: (0, 0)),
                  pl.BlockSpec((None,) + k_norm2.shape[1:], lambda bi, p: (bi, 0, 0, 0))],
        out_specs=seq,
        out_shape=jax.ShapeDtypeStruct((b, s, SB_WIDTH), BF16),
        scratch_shapes=[pltpu.VMEM((SB_BLOCKS_PER_ITER, nh, SB_TQ, LANES), F32)] * 2,
        compiler_params=pltpu.CompilerParams(
            dimension_semantics=("parallel", "parallel"), vmem_limit_bytes=VMEM_LIMIT),
        name="stick_breaking",
    )(qs, ks, vs, tri, k_norm2)


TAIL_TILES = 2
MLP_FF_CHUNK = 1024


def _token_major(ref, scr):
    d, rows, _ = ref.shape
    for c in range(scr.shape[0]):
        for r in range(d):
            scr[c, pl.ds(r, rows, stride=d), :] = ref[r, :, c * LANES:(c + 1) * LANES]
    return jnp.concatenate([scr[c] for c in range(scr.shape[0])], axis=1)


def _tail_kernel(x_ref, o0, o1, o2, l0, l1, l2, ob_ref, ga_ref, gb_ref,
                 wud, wus, wout, g_mlp, w1, w2, g_out, out_ref, *scratch, final_norm):
    tiles = range(TAIL_TILES)
    rows = [slice(t * DIL_TILE, (t + 1) * DIL_TILE) for t in tiles]
    o_a = []
    for t in tiles:
        scr = scratch[4 * t:4 * t + 4]
        outs = [o0[rows[t], :], _token_major(o1.at[t], scr[0]), _token_major(o2.at[t], scr[1])]
        lse = [l0[rows[t], :], _token_major(l1.at[t], scr[2]), _token_major(l2.at[t], scr[3])]
        m = jnp.maximum(jnp.maximum(lse[0], lse[1]), lse[2])
        e = [jnp.exp(l - m) for l in lse]
        o_a.append((e[0] * outs[0] + e[1] * outs[1] + e[2] * outs[2]) / (e[0] + e[1] + e[2]))
    up_a = _dot(jnp.concatenate(o_a, axis=0).astype(BF16), wud[...])
    up_b = _dot(ob_ref[...], wus[...])
    merged = ga_ref[...] * up_a + gb_ref[...] * up_b
    y = x_ref[...] + _dot(merged.astype(BF16), wout[...])
    h2 = _rms(y, g_mlp[...]).astype(BF16)
    chunks = [slice(c * MLP_FF_CHUNK, (c + 1) * MLP_FF_CHUNK) for c in range(D_FF // MLP_FF_CHUNK)]
    u = _dot(h2, w1[:, chunks[0]])
    for c, cs in enumerate(chunks):
        u_next = _dot(h2, w1[:, chunks[c + 1]]) if c + 1 < len(chunks) else None
        u = jnp.maximum(u, 0.0)
        y = y + _dot((u * u).astype(BF16), w2[cs, :])
        u = u_next
    out_ref[...] = _rms(y, g_out[...]) if final_norm else y


def _tail(x, oas, lses, o_b, g_a, g_b, wud, wus, wout, g_mlp, w1, w2, g_out, final_norm):
    b, s, d = x.shape
    tm = TAIL_TILES * DIL_TILE
    tok = lambda w: pl.BlockSpec((None, tm, w), lambda bi, i: (bi, i, 0))
    full = lambda a: pl.BlockSpec(a.shape, lambda bi, i: (0,) * a.ndim,
                                  pipeline_mode=pl.Buffered(1))

    def dil(a):
        if a.ndim == 3:
            return tok(DIL_OUT_WIDTH)
        return pl.BlockSpec((None, TAIL_TILES) + a.shape[2:], lambda bi, i: (bi, i, 0, 0, 0))

    weights = (wud, wus, wout, g_mlp, w1, w2, g_out)
    return pl.pallas_call(
        functools.partial(_tail_kernel, final_norm=final_norm),
        grid=(b, s // tm),
        in_specs=[tok(d)] + [dil(a) for a in (*oas, *lses)] + [tok(SB_WIDTH), tok(d), tok(d)]
                 + [full(a) for a in weights],
        out_specs=tok(d),
        out_shape=jax.ShapeDtypeStruct((b, s, d), F32),
        scratch_shapes=[pltpu.VMEM((DIL_OUT_WIDTH // LANES, DIL_TILE, LANES), F32)]
                       * (4 * TAIL_TILES),
        compiler_params=pltpu.CompilerParams(
            dimension_semantics=("parallel", "parallel"), vmem_limit_bytes=VMEM_LIMIT),
        name="tail",
    )(x, *oas, *lses, o_b, g_a, g_b, *weights)


def kernel(x, norm_mix_g, w_in, b_gate, w_up_dil, w_up_sb, w_out, norm_mlp_g, w_mlp_in,
           w_mlp_out, norm_final_g):
    depth = w_in.shape[0]
    row = lambda v: v.reshape(1, -1)
    for layer in range(depth):
        w = w_in[layer]
        c0 = 3 * DIL_WIDTH
        c1 = c0 + 3 * SB_WIDTH
        dil_cols = lambda i, grp: w[:, i * DIL_WIDTH + grp * DIL_OUT_WIDTH:
                                    i * DIL_WIDTH + (grp + 1) * DIL_OUT_WIDTH]
        w_dil = jnp.concatenate(
            [part for grp in range(len(DIL_GROUPS))
             for part in (dil_cols(0, grp) * (1.0 / math.sqrt(HEAD_DIM)), dil_cols(1, grp),
                          dil_cols(2, grp))], axis=1).astype(BF16)
        *qkv, qs, ks, vs, g_a, g_b, k_norm2 = _proj(
            x, row(norm_mix_g[layer]), w_dil, w[:, c0:c1].astype(BF16), w[:, c1:].astype(BF16),
            row(b_gate[layer]))
        dil = [_dilated_group(qkv[grp], grp) for grp in range(len(DIL_GROUPS))]
        o_b = _sb_attention(qs, ks, vs, k_norm2)
        x = _tail(x, [o for o, _ in dil], [l for _, l in dil], o_b, g_a, g_b,
                  w_up_dil[layer].astype(BF16), w_up_sb[layer].astype(BF16),
                  w_out[layer].astype(BF16), row(norm_mlp_g[layer]),
                  w_mlp_in[layer].astype(BF16), w_mlp_out[layer].astype(BF16),
                  row(norm_final_g), final_norm=layer == depth - 1)
    return x
```

```python
import functools
import math

import jax
import jax.numpy as jnp
import numpy as np
from jax import lax
from jax.experimental import pallas as pl
from jax.experimental.pallas import tpu as pltpu

D_MODEL = 1024
HEAD_DIM = 64
DIL_GROUPS = ((128, 1), (512, 4), (2048, 16))
DIL_HEADS_PER_GROUP = 4
N_DIL_HEADS = DIL_HEADS_PER_GROUP * len(DIL_GROUPS)
N_SB_HEADS = 8
DIL_WIDTH = N_DIL_HEADS * HEAD_DIM
DIL_OUT_WIDTH = DIL_HEADS_PER_GROUP * HEAD_DIM
SB_WIDTH = N_SB_HEADS * HEAD_DIM
D_FF = 4 * D_MODEL
BLOCK = 128
RMS_EPS = 1e-6
NEG_INF = -1e30
LOG2E = 1.4426950408889634
LANES = 128
SUBLANES = 8

VMEM_LIMIT = 56 * 1024 * 1024

BF16 = jnp.bfloat16
F32 = jnp.float32


def _dot(a, b):
    return jnp.dot(a, b, preferred_element_type=F32)


def _dot_nt(a, b):
    return lax.dot_general(a, b, (((1,), (1,)), ((), ())), preferred_element_type=F32)


def _rms(x, g):
    return x * lax.rsqrt(jnp.mean(x * x, axis=-1, keepdims=True) + RMS_EPS) * g


def _sigmoid(x):
    return 1.0 / (1.0 + jnp.exp(-x))


DIL_TILE = 256
PROJ_TILES = 2


def _residue_major(h_ref, d):
    rows = DIL_TILE // d
    return jnp.concatenate(
        [jnp.concatenate([h_ref[c, pl.ds(r, rows, stride=d), :] for r in range(d)], axis=0)
         for c in range(h_ref.shape[0])], axis=1)


def _proj_kernel(x_ref, g_ref, wd0, wd1, wd2, wqs, wks, wvs, wga, wgb, bga, bgb, head_sel,
                 d0_ref, d1_ref, d2_ref, qs_ref, ks_ref, vs_ref, ga_ref, gb_ref, kn_ref, *h_scr):
    tiles = range(PROJ_TILES)
    rows = [slice(t * DIL_TILE, (t + 1) * DIL_TILE) for t in tiles]
    h = []
    for t in tiles:
        h32 = _rms(x_ref[rows[t], :], g_ref[...])
        for c in range(h_scr[t].shape[0]):
            h_scr[t][c] = h32[:, c * LANES:(c + 1) * LANES]
        h.append(h32.astype(BF16))
    h_all = jnp.concatenate(h, axis=0)
    for (_, dilation), w_ref, o_ref in zip(DIL_GROUPS, (wd0, wd1, wd2), (d0_ref, d1_ref, d2_ref)):
        hd = h_all if dilation == 1 else jnp.concatenate(
            [_residue_major(h_scr[t], dilation).astype(BF16) for t in tiles], axis=0)
        o_ref[...] = _dot(hd, w_ref[...]).astype(BF16)
    qs_ref[...] = (_dot(h_all, wqs[...]) * (LOG2E / math.sqrt(HEAD_DIM))).astype(BF16)
    k = _dot(h_all, wks[...]).astype(BF16)
    ks_ref[...] = k
    k_norm2 = jnp.max(_dot(jnp.square(k.astype(F32)).astype(BF16), head_sel[...]), axis=0,
                      keepdims=True)
    kn_ref[...] = jnp.broadcast_to(k_norm2, kn_ref.shape)
    vs_ref[...] = _dot(h_all, wvs[...]).astype(BF16)
    ga_ref[...] = _sigmoid(_dot(h_all, wga[...]) + bga[...])
    gb_ref[...] = _sigmoid(_dot(h_all, wgb[...]) + bgb[...])


def _proj(x, g, w_dil, w_sb, w_gate, b_gate):
    b, s, d = x.shape
    tm = PROJ_TILES * DIL_TILE
    tok = lambda w: pl.BlockSpec((None, tm, w), lambda bi, i: (bi, i, 0))
    const = lambda shape, col: pl.BlockSpec(shape, lambda bi, i: (0, col),
                                            pipeline_mode=pl.Buffered(1))
    cols = lambda a, n: [const((a.shape[0], a.shape[1] // n), c) for c in range(n)]
    dil_width = 3 * DIL_OUT_WIDTH
    out_shape = (
        (jax.ShapeDtypeStruct((b, s, dil_width), BF16),) * len(DIL_GROUPS)
        + (jax.ShapeDtypeStruct((b, s, SB_WIDTH), BF16),) * 3
        + (jax.ShapeDtypeStruct((b, s, D_MODEL), F32),) * 2
        + (jax.ShapeDtypeStruct((b, s // tm, SUBLANES, LANES), F32),))
    out_specs = ((tok(dil_width),) * len(DIL_GROUPS) + (tok(SB_WIDTH),) * 3
                 + (tok(D_MODEL),) * 2
                 + (pl.BlockSpec((None, None, SUBLANES, LANES), lambda bi, i: (bi, i, 0, 0)),))
    head_sel = jnp.asarray(np.arange(SB_WIDTH)[:, None] // HEAD_DIM == np.arange(LANES)[None, :],
                           BF16)
    return pl.pallas_call(
        _proj_kernel,
        grid=(b, s // tm),
        in_specs=([tok(d), const(g.shape, 0)] + cols(w_dil, len(DIL_GROUPS)) + cols(w_sb, 3)
                  + cols(w_gate, 2) + cols(b_gate, 2) + [const(head_sel.shape, 0)]),
        out_specs=out_specs,
        out_shape=out_shape,
        scratch_shapes=[pltpu.VMEM((d // LANES, DIL_TILE, LANES), F32)] * PROJ_TILES,
        compiler_params=pltpu.CompilerParams(
            dimension_semantics=("parallel", "parallel"), vmem_limit_bytes=VMEM_LIMIT),
        name="proj",
    )(x, g, *([w_dil] * len(DIL_GROUPS)), *([w_sb] * 3), *([w_gate] * 2), *([b_gate] * 2), head_sel)


DIL_MAX_QB = 16


def _alibi_slope(head):
    return 2.0 ** (-8.0 * (head + 1) / N_DIL_HEADS)


def _dil_kernel(q_ref, kp_ref, kc_ref, vp_ref, vc_ref, o_ref, lse_ref, *, group, dilation, n_steps,
                qb):
    n = pl.program_id(2)
    width = DIL_OUT_WIDTH
    qrows = qb * BLOCK
    n_res = q_ref.shape[1] if len(q_ref.shape) == 4 else 1
    res = range(n_res)
    take = lambda ref, r, n_rows: (ref[:, r] if len(ref.shape) == 4 else ref[...]).reshape(n_rows, width)
    q = [take(q_ref, r, qrows) for r in res]
    k = [jnp.concatenate([take(kp_ref, r, BLOCK), take(kc_ref, r, qrows)], axis=0) for r in res]
    v = [jnp.concatenate([take(vp_ref, r, BLOCK), take(vc_ref, r, qrows)], axis=0) for r in res]
    qi = lax.broadcasted_iota(jnp.int32, (BLOCK, 2 * BLOCK), 0)
    kj = lax.broadcasted_iota(jnp.int32, (BLOCK, 2 * BLOCK), 1)
    steps = qi + BLOCK - kj
    valid = (steps >= 0) & (steps <= n_steps)
    has_prev = (kj >= BLOCK) | (n > 0)
    dist = (steps * dilation).astype(F32)
    low = lax.broadcasted_iota(jnp.int32, (1, LANES), 1) < HEAD_DIM
    sel = (low, jnp.logical_not(low))
    work = [(r, p, j, e) for r in res for p in range(2) for j in range(qb) for e in range(2)]
    q_rows = lambda j: slice(j * BLOCK, (j + 1) * BLOCK)
    k_rows = lambda j: slice(j * BLOCK, (j + 2) * BLOCK)
    pair = lambda t, p: t[:, p * LANES:(p + 1) * LANES]
    qm = {(r, p, e): jnp.where(sel[e], pair(q[r], p), jnp.zeros((qrows, LANES), BF16))
          for r in res for p in range(2) for e in range(2)}
    bias = {(p, e): jnp.where(valid, -_alibi_slope(group * DIL_HEADS_PER_GROUP + 2 * p + e) * dist,
                              NEG_INF) for p in range(2) for e in range(2)}
    logits = {}
    for r in res:
        for p in range(2):
            for j in range(qb):
                both = _dot_nt(jnp.concatenate([qm[r, p, e][q_rows(j)] for e in range(2)], axis=0),
                               pair(k[r], p)[k_rows(j)])
                for e in range(2):
                    lg = both[e * BLOCK:(e + 1) * BLOCK] + bias[p, e]
                    logits[r, p, j, e] = jnp.where(has_prev, lg, NEG_INF) if j == 0 else lg
    ms = {w: jnp.max(logits[w], axis=-1, keepdims=True) for w in work}
    pr = {w: jnp.exp(logits[w] - ms[w]).astype(BF16) for w in work}
    ones = jnp.ones((2 * BLOCK, LANES), BF16)
    t = {(r, p, j): _dot(jnp.concatenate([pr[r, p, j, e] for e in range(2)], axis=0),
                         jnp.concatenate([pair(v[r], p)[k_rows(j)], ones], axis=1))
         for r in res for p in range(2) for j in range(qb)}
    for r in res:
        for j in range(qb):
            o_pairs, lse_pairs = [], []
            for p in range(2):
                head = [t[r, p, j][e * BLOCK:(e + 1) * BLOCK] for e in range(2)]
                num = jnp.where(low, head[0][:, :LANES], head[1][:, :LANES])
                den = jnp.where(low, head[0][:, LANES:], head[1][:, LANES:])
                o_pairs.append(num / den)
                lse_pairs.append(jnp.where(low, ms[r, p, j, 0], ms[r, p, j, 1]) + jnp.log(den))
            for ref, pairs in ((o_ref, o_pairs), (lse_ref, lse_pairs)):
                val = jnp.concatenate(pairs, axis=1)
                if len(ref.shape) == 4:
                    rows = ref.shape[2]
                    tiles = slice(j * BLOCK // rows, (j + 1) * BLOCK // rows)
                    ref[tiles, r] = val.reshape(BLOCK // rows, rows, width)
                else:
                    ref[j * BLOCK:(j + 1) * BLOCK, :] = val


def _dilated_group(qkv, group):
    window, dilation = DIL_GROUPS[group]
    b, s, _ = qkv.shape
    width = DIL_OUT_WIDTH
    qb = min(DIL_MAX_QB, s // dilation // BLOCK)
    n_res = min(dilation, DIL_MAX_QB // qb)
    qrows = qb * BLOCK
    n_grid = s // dilation // qrows
    first_prev = lambda n: jnp.maximum(n * qb - 1, 0)
    if dilation == 1:
        view = qkv
        cur = lambda c: pl.BlockSpec((None, qrows, width), lambda bi, r, n: (bi, n, c))
        prev = lambda c: pl.BlockSpec((None, BLOCK, width), lambda bi, r, n: (bi, first_prev(n), c))
        shp = jax.ShapeDtypeStruct((b, s, width), F32)
    else:
        rows = DIL_TILE // dilation
        view = qkv.reshape(b, s // DIL_TILE, dilation, rows, 3 * width)
        cur = lambda c: pl.BlockSpec((None, qrows // rows, n_res, rows, width),
                                     lambda bi, r, n: (bi, n, r, 0, c))
        prev = lambda c: pl.BlockSpec((None, BLOCK // rows, n_res, rows, width),
                                      lambda bi, r, n: (bi, first_prev(n), r, 0, c))
        shp = jax.ShapeDtypeStruct((b, s // DIL_TILE, dilation, rows, width), F32)
    return pl.pallas_call(
        functools.partial(_dil_kernel, group=group, dilation=dilation,
                          n_steps=window // dilation, qb=qb),
        grid=(b, dilation // n_res, n_grid),
        in_specs=[cur(0), prev(1), cur(1), prev(2), cur(2)],
        out_specs=(cur(0), cur(0)),
        out_shape=(shp, shp),
        compiler_params=pltpu.CompilerParams(
            dimension_semantics=("parallel", "parallel", "parallel"),
            vmem_limit_bytes=VMEM_LIMIT),
        name=f"dilated_d{dilation}",
    )(view, view, view, view, view)


SB_TQ = 64
SB_TK = 256
SB_WINDOW = 256
SB_DEAD = 136.0
SB_NORM_SLACK = 1.0 + 2.0 ** -8
SB_DOT_SLACK = 1.001
SB_BLOCKS_PER_ITER = 10
HEADS_PER_STEP = LANES // HEAD_DIM


def _softplus2(z):
    return jnp.maximum(z, 0.0) + jnp.log2(1.0 + jnp.exp2(-jnp.abs(z)))


def _split_hi_lo(sp):
    parts = []
    for c in range(sp.shape[1] // LANES):
        sp_c = sp[:, c * LANES:(c + 1) * LANES]
        hi = sp_c.astype(BF16)
        parts += [hi, (sp_c - hi.astype(F32)).astype(BF16)]
    return jnp.concatenate(parts, axis=1)


def _sb_suffix_sums(hl, tri):
    return [_dot(hl[:, 2 * c * LANES:2 * (c + 1) * LANES], tri)
            for c in range(hl.shape[1] // (2 * LANES))]


def _sb_weights(z, sums, carry):
    las = [None] * len(sums)
    for c in reversed(range(len(sums))):
        las[c] = z[:, c * LANES:(c + 1) * LANES] - sums[c][:, :LANES] - carry
        carry = carry + sums[c][:, LANES:]
    return jnp.exp2(jnp.concatenate(las, axis=1)).astype(BF16), carry


def _sb_kernel(q_ref, k_ref, v_ref, tri_ref, kn_ref, o_ref, acc_ref, carry_ref):
    s = q_ref.shape[0]
    n_q = s // SB_TQ
    heads = range(HEADS_PER_STEP)
    tri = tri_ref[...]
    lane = lax.broadcasted_iota(jnp.int32, (1, LANES), 1)
    own = [(lane >= h * HEAD_DIM) & (lane < (h + 1) * HEAD_DIM) for h in heads]

    kn = jnp.max(kn_ref[...], axis=(0, 1), keepdims=True).reshape(1, LANES)
    k_max = [jnp.sqrt(jnp.max(jnp.where(lane == HEADS_PER_STEP * pl.program_id(1) + h, kn, 0.0)))
             * SB_NORM_SLACK for h in heads]

    col_minus_row = (lax.broadcasted_iota(jnp.int32, (SB_TQ, SB_WINDOW), 1)
                     - lax.broadcasted_iota(jnp.int32, (SB_TQ, SB_WINDOW), 0))
    lead = SB_WINDOW - SB_TQ

    def query_blocks(blocks):
        items = [(slot, h) for slot in range(len(blocks)) for h in heads]
        rows, start, causal, q, z_max = {}, {}, {}, {}, {}
        for slot, qi in enumerate(blocks):
            first = qi * SB_TQ
            if isinstance(qi, int):
                start[slot] = max(first - lead, 0)
                rows[slot] = pl.ds(first, SB_TQ)
            else:
                start[slot] = pl.multiple_of(jnp.maximum(first - lead, 0), SB_TQ)
                rows[slot] = pl.ds(pl.multiple_of(first, SB_TQ), SB_TQ)
            causal[slot] = col_minus_row < (first - start[slot] if isinstance(qi, int) else lead)
            q_pair = q_ref[rows[slot], :]
            for h in heads:
                q[slot, h] = jnp.where(own[h], q_pair, jnp.zeros_like(q_pair))
                z_max[slot, h] = (jnp.sqrt(jnp.sum(jnp.square(q[slot, h].astype(F32)), axis=1,
                                                   keepdims=True)) * (k_max[h] * SB_DOT_SLACK) + 1.0)
        z = {}
        for slot in range(len(blocks)):
            both = _dot_nt(jnp.concatenate([q[slot, h] for h in heads], axis=0),
                           k_ref[pl.ds(start[slot], SB_WINDOW), :])
            for h in heads:
                z[slot, h] = jnp.where(causal[slot], both[h * SB_TQ:(h + 1) * SB_TQ], NEG_INF)
        hl = {w: _split_hi_lo(_softplus2(z[w])) for w in items}
        n_chunks = SB_WINDOW // LANES
        stacked = _dot(jnp.concatenate(
            [hl[w][:, 2 * c * LANES:2 * (c + 1) * LANES] for w in items for c in range(n_chunks)],
            axis=0), tri)
        sums = {w: [stacked[(i * n_chunks + c) * SB_TQ:(i * n_chunks + c + 1) * SB_TQ]
                    for c in range(n_chunks)] for i, w in enumerate(items)}
        a, carry = {}, {}
        for w in items:
            a[w], carry[w] = _sb_weights(z[w], sums[w], jnp.zeros((SB_TQ, LANES), F32))
        for slot in range(len(blocks)):
            both = _dot(jnp.concatenate([a[slot, h] for h in heads], axis=0),
                        v_ref[pl.ds(start[slot], SB_WINDOW), :])
            for h in heads:
                acc_ref[slot, h] = both[h * SB_TQ:(h + 1) * SB_TQ]
                carry_ref[slot, h] = carry[slot, h]

        def slack(slot, carries):
            per_head = [carries[h] - z_max[slot, h] for h in heads]
            return jnp.min(jnp.minimum(per_head[0], per_head[1]), keepdims=True)

        slacks = [slack(slot, [carry[slot, h] for h in heads]) for slot in range(len(blocks))]

        @pl.when(functools.reduce(jnp.minimum, slacks)[0, 0] < SB_DEAD)
        def _():
            for slot in range(len(blocks)):
                def cond(state):
                    ke, live = state
                    return (ke > 0) & live

                def body(state):
                    ke, _ = state
                    ks = pl.multiple_of(jnp.maximum(ke - SB_TK, 0), SB_TQ)
                    fresh = (lax.broadcasted_iota(jnp.int32, (SB_TQ, SB_TK), 1) + ks) < ke
                    carries = []
                    for h in heads:
                        zt = jnp.where(fresh, _dot_nt(q[slot, h], k_ref[pl.ds(ks, SB_TK), :]),
                                       NEG_INF)
                        at, ct = _sb_weights(
                            zt, _sb_suffix_sums(_split_hi_lo(_softplus2(zt)), tri),
                            carry_ref[slot, h])
                        acc_ref[slot, h] += _dot(at, v_ref[pl.ds(ks, SB_TK), :])
                        carry_ref[slot, h] = ct
                        carries.append(ct)
                    return ks, slack(slot, carries)[0, 0] < SB_DEAD

                lax.while_loop(cond, body, (jnp.asarray(start[slot], jnp.int32),
                                            slacks[slot][0, 0] < SB_DEAD))

        for slot in range(len(blocks)):
            o_ref[rows[slot], :] = jnp.where(own[0], acc_ref[slot, 0],
                                             acc_ref[slot, 1]).astype(o_ref.dtype)

    n_lead = lead // SB_TQ
    query_blocks(list(range(n_lead)))
    n_groups = (n_q - n_lead) // SB_BLOCKS_PER_ITER

    def group(i, _):
        query_blocks([n_lead + i * SB_BLOCKS_PER_ITER + j for j in range(SB_BLOCKS_PER_ITER)])
        return 0

    lax.fori_loop(0, n_groups, group, 0)
    rest = list(range(n_lead + n_groups * SB_BLOCKS_PER_ITER, n_q))
    if rest:
        query_blocks(rest)


def _sb_attention(qs, ks, vs, k_norm2):
    b, s, _ = qs.shape
    assert s % SB_TQ == 0 and s >= SB_WINDOW and SB_WINDOW % LANES == 0 and HEADS_PER_STEP == 2
    half = np.concatenate([np.arange(LANES)[:, None] >= np.arange(LANES)[None, :],
                           np.ones((LANES, LANES), bool)], axis=1)
    tri = jnp.asarray(np.concatenate([half, half], axis=0), BF16)
    nh = HEADS_PER_STEP
    seq = pl.BlockSpec((None, s, LANES), lambda bi, p: (bi, 0, p))
    return pl.pallas_call(
        _sb_kernel,
        grid=(b, SB_WIDTH // LANES),
        in_specs=[seq, seq, seq, pl.BlockSpec(tri.shape, lambda bi, p: (0, 0)),
                  pl.BlockSpec((None,) + k_norm2.shape[1:], lambda bi, p: (bi, 0, 0, 0))],
        out_specs=seq,
        out_shape=jax.ShapeDtypeStruct((b, s, SB_WIDTH), BF16),
        scratch_shapes=[pltpu.VMEM((SB_BLOCKS_PER_ITER, nh, SB_TQ, LANES), F32)] * 2,
        compiler_params=pltpu.CompilerParams(
            dimension_semantics=("parallel", "parallel"), vmem_limit_bytes=VMEM_LIMIT),
        name="stick_breaking",
    )(qs, ks, vs, tri, k_norm2)


TAIL_TILES = 2
MLP_FF_CHUNK = 1024


def _token_major(ref, scr):
    d, rows, _ = ref.shape
    for c in range(scr.shape[0]):
        for r in range(d):
            scr[c, pl.ds(r, rows, stride=d), :] = ref[r, :, c * LANES:(c + 1) * LANES]
    return jnp.concatenate([scr[c] for c in range(scr.shape[0])], axis=1)


def _tail_kernel(x_ref, o0, o1, o2, l0, l1, l2, ob_ref, ga_ref, gb_ref,
                 wud, wus, wout, g_mlp, w1, w2, g_out, out_ref, *scratch, final_norm):
    tiles = range(TAIL_TILES)
    rows = [slice(t * DIL_TILE, (t + 1) * DIL_TILE) for t in tiles]
    o_a = []
    for t in tiles:
        scr = scratch[4 * t:4 * t + 4]
        outs = [o0[rows[t], :], _token_major(o1.at[t], scr[0]), _token_major(o2.at[t], scr[1])]
        lse = [l0[rows[t], :], _token_major(l1.at[t], scr[2]), _token_major(l2.at[t], scr[3])]
        m = jnp.maximum(jnp.maximum(lse[0], lse[1]), lse[2])
        e = [jnp.exp(l - m) for l in lse]
        o_a.append((e[0] * outs[0] + e[1] * outs[1] + e[2] * outs[2]) / (e[0] + e[1] + e[2]))
    up_a = [_dot(o_a[t].astype(BF16), wud[...]) for t in tiles]
    up_b = [_dot(ob_ref[rows[t], :], wus[...]) for t in tiles]
    merged = [ga_ref[rows[t], :] * up_a[t] + gb_ref[rows[t], :] * up_b[t] for t in tiles]
    y = [x_ref[rows[t], :] + _dot(merged[t].astype(BF16), wout[...]) for t in tiles]
    h2 = [_rms(y[t], g_mlp[...]).astype(BF16) for t in tiles]
    for c in range(D_FF // MLP_FF_CHUNK):
        cs = slice(c * MLP_FF_CHUNK, (c + 1) * MLP_FF_CHUNK)
        u = [jnp.maximum(_dot(h2[t], w1[:, cs]), 0.0) for t in tiles]
        y = [y[t] + _dot((u[t] * u[t]).astype(BF16), w2[cs, :]) for t in tiles]
    for t in tiles:
        out_ref[rows[t], :] = _rms(y[t], g_out[...]) if final_norm else y[t]


def _tail(x, oas, lses, o_b, g_a, g_b, wud, wus, wout, g_mlp, w1, w2, g_out, final_norm):
    b, s, d = x.shape
    tm = TAIL_TILES * DIL_TILE
    tok = lambda w: pl.BlockSpec((None, tm, w), lambda bi, i: (bi, i, 0))
    full = lambda a: pl.BlockSpec(a.shape, lambda bi, i: (0,) * a.ndim,
                                  pipeline_mode=pl.Buffered(1))

    def dil(a):
        if a.ndim == 3:
            return tok(DIL_OUT_WIDTH)
        return pl.BlockSpec((None, TAIL_TILES) + a.shape[2:], lambda bi, i: (bi, i, 0, 0, 0))

    weights = (wud, wus, wout, g_mlp, w1, w2, g_out)
    return pl.pallas_call(
        functools.partial(_tail_kernel, final_norm=final_norm),
        grid=(b, s // tm),
        in_specs=[tok(d)] + [dil(a) for a in (*oas, *lses)] + [tok(SB_WIDTH), tok(d), tok(d)]
                 + [full(a) for a in weights],
        out_specs=tok(d),
        out_shape=jax.ShapeDtypeStruct((b, s, d), F32),
        scratch_shapes=[pltpu.VMEM((DIL_OUT_WIDTH // LANES, DIL_TILE, LANES), F32)]
                       * (4 * TAIL_TILES),
        compiler_params=pltpu.CompilerParams(
            dimension_semantics=("parallel", "parallel"), vmem_limit_bytes=VMEM_LIMIT),
        name="tail",
    )(x, *oas, *lses, o_b, g_a, g_b, *weights)


def kernel(x, norm_mix_g, w_in, b_gate, w_up_dil, w_up_sb, w_out, norm_mlp_g, w_mlp_in,
           w_mlp_out, norm_final_g):
    depth = w_in.shape[0]
    row = lambda v: v.reshape(1, -1)
    for layer in range(depth):
        w = w_in[layer]
        c0 = 3 * DIL_WIDTH
        c1 = c0 + 3 * SB_WIDTH
        dil_cols = lambda i, grp: w[:, i * DIL_WIDTH + grp * DIL_OUT_WIDTH:
                                    i * DIL_WIDTH + (grp + 1) * DIL_OUT_WIDTH]
        w_dil = jnp.concatenate(
            [part for grp in range(len(DIL_GROUPS))
             for part in (dil_cols(0, grp) * (1.0 / math.sqrt(HEAD_DIM)), dil_cols(1, grp),
                          dil_cols(2, grp))], axis=1).astype(BF16)
        *qkv, qs, ks, vs, g_a, g_b, k_norm2 = _proj(
            x, row(norm_mix_g[layer]), w_dil, w[:, c0:c1].astype(BF16), w[:, c1:].astype(BF16),
            row(b_gate[layer]))
        dil = [_dilated_group(qkv[grp], grp) for grp in range(len(DIL_GROUPS))]
        o_b = _sb_attention(qs, ks, vs, k_norm2)
        x = _tail(x, [o for o, _ in dil], [l for _, l in dil], o_b, g_a, g_b,
                  w_up_dil[layer].astype(BF16), w_up_sb[layer].astype(BF16),
                  w_out[layer].astype(BF16), row(norm_mlp_g[layer]),
                  w_mlp_in[layer].astype(BF16), w_mlp_out[layer].astype(BF16),
                  row(norm_final_g), final_norm=layer == depth - 1)
    return x
```

```python
import functools
import math

import jax
import jax.numpy as jnp
import numpy as np
from jax import lax
from jax.experimental import pallas as pl
from jax.experimental.pallas import tpu as pltpu

D_MODEL = 1024
HEAD_DIM = 64
DIL_GROUPS = ((128, 1), (512, 4), (2048, 16))
DIL_HEADS_PER_GROUP = 4
N_DIL_HEADS = DIL_HEADS_PER_GROUP * len(DIL_GROUPS)
N_SB_HEADS = 8
DIL_WIDTH = N_DIL_HEADS * HEAD_DIM
DIL_OUT_WIDTH = DIL_HEADS_PER_GROUP * HEAD_DIM
SB_WIDTH = N_SB_HEADS * HEAD_DIM
D_FF = 4 * D_MODEL
BLOCK = 128
RMS_EPS = 1e-6
NEG_INF = -1e30
LOG2E = 1.4426950408889634
LANES = 128
SUBLANES = 8

VMEM_LIMIT = 56 * 1024 * 1024

BF16 = jnp.bfloat16
F32 = jnp.float32


def _dot(a, b):
    return jnp.dot(a, b, preferred_element_type=F32)


def _dot_nt(a, b):
    return lax.dot_general(a, b, (((1,), (1,)), ((), ())), preferred_element_type=F32)


def _rms(x, g):
    return x * lax.rsqrt(jnp.mean(x * x, axis=-1, keepdims=True) + RMS_EPS) * g


def _sigmoid(x):
    return 1.0 / (1.0 + jnp.exp(-x))


DIL_TILE = 256
PROJ_TILES = 2


def _residue_major(h_ref, d):
    rows = DIL_TILE // d
    return jnp.concatenate(
        [jnp.concatenate([h_ref[c, pl.ds(r, rows, stride=d), :] for r in range(d)], axis=0)
         for c in range(h_ref.shape[0])], axis=1)


def _proj_kernel(x_ref, g_ref, wd0, wd1, wd2, wqs, wks, wvs, wga, wgb, bga, bgb, head_sel,
                 d0_ref, d1_ref, d2_ref, qs_ref, ks_ref, vs_ref, ga_ref, gb_ref, kn_ref, *h_scr):
    tiles = range(PROJ_TILES)
    rows = [slice(t * DIL_TILE, (t + 1) * DIL_TILE) for t in tiles]
    h = []
    for t in tiles:
        h32 = _rms(x_ref[rows[t], :], g_ref[...])
        for c in range(h_scr[t].shape[0]):
            h_scr[t][c] = h32[:, c * LANES:(c + 1) * LANES]
        h.append(h32.astype(BF16))
    h_all = jnp.concatenate(h, axis=0)
    for (_, dilation), w_ref, o_ref in zip(DIL_GROUPS, (wd0, wd1, wd2), (d0_ref, d1_ref, d2_ref)):
        hd = h_all if dilation == 1 else jnp.concatenate(
            [_residue_major(h_scr[t], dilation).astype(BF16) for t in tiles], axis=0)
        o_ref[...] = _dot(hd, w_ref[...]).astype(BF16)
    qs_ref[...] = (_dot(h_all, wqs[...]) * (LOG2E / math.sqrt(HEAD_DIM))).astype(BF16)
    k = _dot(h_all, wks[...]).astype(BF16)
    ks_ref[...] = k
    k_norm2 = jnp.max(_dot(jnp.square(k.astype(F32)).astype(BF16), head_sel[...]), axis=0,
                      keepdims=True)
    kn_ref[...] = jnp.broadcast_to(k_norm2, kn_ref.shape)
    vs_ref[...] = _dot(h_all, wvs[...]).astype(BF16)
    ga_ref[...] = _sigmoid(_dot(h_all, wga[...]) + bga[...])
    gb_ref[...] = _sigmoid(_dot(h_all, wgb[...]) + bgb[...])


def _proj(x, g, w_dil, w_sb, w_gate, b_gate):
    b, s, d = x.shape
    tm = PROJ_TILES * DIL_TILE
    tok = lambda w: pl.BlockSpec((None, tm, w), lambda bi, i: (bi, i, 0))
    const = lambda shape, col: pl.BlockSpec(shape, lambda bi, i: (0, col),
                                            pipeline_mode=pl.Buffered(1))
    cols = lambda a, n: [const((a.shape[0], a.shape[1] // n), c) for c in range(n)]
    dil_width = 3 * DIL_OUT_WIDTH
    out_shape = (
        (jax.ShapeDtypeStruct((b, s, dil_width), BF16),) * len(DIL_GROUPS)
        + (jax.ShapeDtypeStruct((b, s, SB_WIDTH), BF16),) * 3
        + (jax.ShapeDtypeStruct((b, s, D_MODEL), F32),) * 2
        + (jax.ShapeDtypeStruct((b, s // tm, SUBLANES, LANES), F32),))
    out_specs = ((tok(dil_width),) * len(DIL_GROUPS) + (tok(SB_WIDTH),) * 3
                 + (tok(D_MODEL),) * 2
                 + (pl.BlockSpec((None, None, SUBLANES, LANES), lambda bi, i: (bi, i, 0, 0)),))
    head_sel = jnp.asarray(np.arange(SB_WIDTH)[:, None] // HEAD_DIM == np.arange(LANES)[None, :],
                           BF16)
    return pl.pallas_call(
        _proj_kernel,
        grid=(b, s // tm),
        in_specs=([tok(d), const(g.shape, 0)] + cols(w_dil, len(DIL_GROUPS)) + cols(w_sb, 3)
                  + cols(w_gate, 2) + cols(b_gate, 2) + [const(head_sel.shape, 0)]),
        out_specs=out_specs,
        out_shape=out_shape,
        scratch_shapes=[pltpu.VMEM((d // LANES, DIL_TILE, LANES), F32)] * PROJ_TILES,
        compiler_params=pltpu.CompilerParams(
            dimension_semantics=("parallel", "parallel"), vmem_limit_bytes=VMEM_LIMIT),
        name="proj",
    )(x, g, *([w_dil] * len(DIL_GROUPS)), *([w_sb] * 3), *([w_gate] * 2), *([b_gate] * 2), head_sel)


DIL_MAX_QB = 16


def _alibi_slope(head):
    return 2.0 ** (-8.0 * (head + 1) / N_DIL_HEADS)


def _dil_group(n, q_ref, kp_ref, kc_ref, vp_ref, vc_ref, o_ref, lse_ref, *, group, dilation,
               n_steps, qb):
    width = DIL_OUT_WIDTH
    qrows = qb * BLOCK
    n_res = q_ref.shape[1] if len(q_ref.shape) == 4 else 1
    res = range(n_res)
    take = lambda ref, r, n_rows: (ref[:, r] if len(ref.shape) == 4 else ref[...]).reshape(n_rows, width)
    q = [take(q_ref, r, qrows) for r in res]
    k = [jnp.concatenate([take(kp_ref, r, BLOCK), take(kc_ref, r, qrows)], axis=0) for r in res]
    v = [jnp.concatenate([take(vp_ref, r, BLOCK), take(vc_ref, r, qrows)], axis=0) for r in res]
    qi = lax.broadcasted_iota(jnp.int32, (BLOCK, 2 * BLOCK), 0)
    kj = lax.broadcasted_iota(jnp.int32, (BLOCK, 2 * BLOCK), 1)
    steps = qi + BLOCK - kj
    valid = (steps >= 0) & (steps <= n_steps)
    has_prev = (kj >= BLOCK) | (n > 0)
    dist = (steps * dilation).astype(F32)
    low = lax.broadcasted_iota(jnp.int32, (1, LANES), 1) < HEAD_DIM
    sel = (low, jnp.logical_not(low))
    work = [(r, p, j, e) for r in res for p in range(2) for j in range(qb) for e in range(2)]
    q_rows = lambda j: slice(j * BLOCK, (j + 1) * BLOCK)
    k_rows = lambda j: slice(j * BLOCK, (j + 2) * BLOCK)
    pair = lambda t, p: t[:, p * LANES:(p + 1) * LANES]
    qm = {(r, p, e): jnp.where(sel[e], pair(q[r], p), jnp.zeros((qrows, LANES), BF16))
          for r in res for p in range(2) for e in range(2)}
    bias = {(p, e): jnp.where(valid, -_alibi_slope(group * DIL_HEADS_PER_GROUP + 2 * p + e) * dist,
                              NEG_INF) for p in range(2) for e in range(2)}
    logits = {}
    for r in res:
        for p in range(2):
            for j in range(qb):
                both = _dot_nt(jnp.concatenate([qm[r, p, e][q_rows(j)] for e in range(2)], axis=0),
                               pair(k[r], p)[k_rows(j)])
                for e in range(2):
                    lg = both[e * BLOCK:(e + 1) * BLOCK] + bias[p, e]
                    logits[r, p, j, e] = jnp.where(has_prev, lg, NEG_INF) if j == 0 else lg
    ms = {w: jnp.max(logits[w], axis=-1, keepdims=True) for w in work}
    pr = {w: jnp.exp(logits[w] - ms[w]).astype(BF16) for w in work}
    ones = jnp.ones((2 * BLOCK, LANES), BF16)
    t = {(r, p, j): _dot(jnp.concatenate([pr[r, p, j, e] for e in range(2)], axis=0),
                         jnp.concatenate([pair(v[r], p)[k_rows(j)], ones], axis=1))
         for r in res for p in range(2) for j in range(qb)}
    for r in res:
        for j in range(qb):
            o_pairs, lse_pairs = [], []
            for p in range(2):
                head = [t[r, p, j][e * BLOCK:(e + 1) * BLOCK] for e in range(2)]
                num = jnp.where(low, head[0][:, :LANES], head[1][:, :LANES])
                den = jnp.where(low, head[0][:, LANES:], head[1][:, LANES:])
                o_pairs.append(num / den)
                lse_pairs.append(jnp.where(low, ms[r, p, j, 0], ms[r, p, j, 1]) + jnp.log(den))
            for ref, pairs in ((o_ref, o_pairs), (lse_ref, lse_pairs)):
                val = jnp.concatenate(pairs, axis=1)
                if len(ref.shape) == 4:
                    rows = ref.shape[2]
                    tiles = slice(j * BLOCK // rows, (j + 1) * BLOCK // rows)
                    ref[tiles, r] = val.reshape(BLOCK // rows, rows, width)
                else:
                    ref[j * BLOCK:(j + 1) * BLOCK, :] = val


def _dil_kernel(*refs, groups):
    i = pl.program_id(1)
    n_in = 5 * len(groups)
    for g, (n_grid, kwargs) in enumerate(groups):
        _dil_group(i % n_grid, *refs[5 * g:5 * g + 5], *refs[n_in + 2 * g:n_in + 2 * g + 2],
                   **kwargs)


def _dilated(qkvs):
    b, s, _ = qkvs[0].shape
    width = DIL_OUT_WIDTH
    groups, operands, in_specs, out_specs, out_shape, steps = [], [], [], [], [], set()
    for group, ((window, dilation), qkv) in enumerate(zip(DIL_GROUPS, qkvs)):
        qb = min(DIL_MAX_QB, s // dilation // BLOCK)
        n_res = min(dilation, DIL_MAX_QB // qb)
        qrows = qb * BLOCK
        n_grid = s // dilation // qrows
        steps.add(dilation // n_res * n_grid)

        def index(block, i, n_grid=n_grid, qb=qb):
            n = i % n_grid
            return (n if block == "cur" else jnp.maximum(n * qb - 1, 0)), i // n_grid

        if dilation == 1:
            view = qkv
            spec = lambda block, c, index=index, qrows=qrows: pl.BlockSpec(
                (None, qrows if block == "cur" else BLOCK, width),
                lambda bi, i: (bi, index(block, i)[0], c))
            shp = jax.ShapeDtypeStruct((b, s, width), F32)
        else:
            rows = DIL_TILE // dilation
            view = qkv.reshape(b, s // DIL_TILE, dilation, rows, 3 * width)
            spec = lambda block, c, index=index, qrows=qrows, rows=rows, n_res=n_res: pl.BlockSpec(
                (None, (qrows if block == "cur" else BLOCK) // rows, n_res, rows, width),
                lambda bi, i: (bi, *index(block, i), 0, c))
            shp = jax.ShapeDtypeStruct((b, s // DIL_TILE, dilation, rows, width), F32)
        groups.append((n_grid, dict(group=group, dilation=dilation, n_steps=window // dilation,
                                    qb=qb)))
        operands += [view] * 5
        in_specs += [spec("cur", 0), spec("prev", 1), spec("cur", 1), spec("prev", 2),
                     spec("cur", 2)]
        out_specs += [spec("cur", 0)] * 2
        out_shape += [shp] * 2
    assert len(steps) == 1, "every dilation group needs the same number of grid steps"
    outs = pl.pallas_call(
        functools.partial(_dil_kernel, groups=tuple(groups)),
        grid=(b, steps.pop()),
        in_specs=in_specs,
        out_specs=tuple(out_specs),
        out_shape=tuple(out_shape),
        compiler_params=pltpu.CompilerParams(
            dimension_semantics=("parallel", "parallel"), vmem_limit_bytes=VMEM_LIMIT),
        name="dilated",
    )(*operands)
    return [(outs[2 * g], outs[2 * g + 1]) for g in range(len(groups))]


SB_TQ = 64
SB_TK = 256
SB_WINDOW = 256
SB_DEAD = 136.0
SB_NORM_SLACK = 1.0 + 2.0 ** -8
SB_DOT_SLACK = 1.001
SB_BLOCKS_PER_ITER = 10
HEADS_PER_STEP = LANES // HEAD_DIM


def _softplus2(z):
    return jnp.maximum(z, 0.0) + jnp.log2(1.0 + jnp.exp2(-jnp.abs(z)))


def _split_hi_lo(sp):
    parts = []
    for c in range(sp.shape[1] // LANES):
        sp_c = sp[:, c * LANES:(c + 1) * LANES]
        hi = sp_c.astype(BF16)
        parts += [hi, (sp_c - hi.astype(F32)).astype(BF16)]
    return jnp.concatenate(parts, axis=1)


def _sb_suffix_sums(hl, tri):
    return [_dot(hl[:, 2 * c * LANES:2 * (c + 1) * LANES], tri)
            for c in range(hl.shape[1] // (2 * LANES))]


def _sb_weights(z, sums, carry):
    las = [None] * len(sums)
    for c in reversed(range(len(sums))):
        las[c] = z[:, c * LANES:(c + 1) * LANES] - sums[c][:, :LANES] - carry
        carry = carry + sums[c][:, LANES:]
    return jnp.exp2(jnp.concatenate(las, axis=1)).astype(BF16), carry


def _sb_kernel(q_ref, k_ref, v_ref, tri_ref, kn_ref, o_ref, acc_ref, carry_ref):
    s = q_ref.shape[0]
    n_q = s // SB_TQ
    heads = range(HEADS_PER_STEP)
    tri = tri_ref[...]
    lane = lax.broadcasted_iota(jnp.int32, (1, LANES), 1)
    own = [(lane >= h * HEAD_DIM) & (lane < (h + 1) * HEAD_DIM) for h in heads]

    kn = jnp.max(kn_ref[...], axis=(0, 1), keepdims=True).reshape(1, LANES)
    k_max = [jnp.sqrt(jnp.max(jnp.where(lane == HEADS_PER_STEP * pl.program_id(1) + h, kn, 0.0)))
             * SB_NORM_SLACK for h in heads]

    col_minus_row = (lax.broadcasted_iota(jnp.int32, (SB_TQ, SB_WINDOW), 1)
                     - lax.broadcasted_iota(jnp.int32, (SB_TQ, SB_WINDOW), 0))
    lead = SB_WINDOW - SB_TQ

    def query_blocks(blocks):
        items = [(slot, h) for slot in range(len(blocks)) for h in heads]
        rows, start, causal, q, z_max = {}, {}, {}, {}, {}
        for slot, qi in enumerate(blocks):
            first = qi * SB_TQ
            if isinstance(qi, int):
                start[slot] = max(first - lead, 0)
                rows[slot] = pl.ds(first, SB_TQ)
            else:
                start[slot] = pl.multiple_of(jnp.maximum(first - lead, 0), SB_TQ)
                rows[slot] = pl.ds(pl.multiple_of(first, SB_TQ), SB_TQ)
            causal[slot] = col_minus_row < (first - start[slot] if isinstance(qi, int) else lead)
            q_pair = q_ref[rows[slot], :]
            for h in heads:
                q[slot, h] = jnp.where(own[h], q_pair, jnp.zeros_like(q_pair))
                z_max[slot, h] = (jnp.sqrt(jnp.sum(jnp.square(q[slot, h].astype(F32)), axis=1,
                                                   keepdims=True)) * (k_max[h] * SB_DOT_SLACK) + 1.0)
        z = {}
        for slot in range(len(blocks)):
            both = _dot_nt(jnp.concatenate([q[slot, h] for h in heads], axis=0),
                           k_ref[pl.ds(start[slot], SB_WINDOW), :])
            for h in heads:
                z[slot, h] = jnp.where(causal[slot], both[h * SB_TQ:(h + 1) * SB_TQ], NEG_INF)
        hl = {w: _split_hi_lo(_softplus2(z[w])) for w in items}
        n_chunks = SB_WINDOW // LANES
        stacked = _dot(jnp.concatenate(
            [hl[w][:, 2 * c * LANES:2 * (c + 1) * LANES] for w in items for c in range(n_chunks)],
            axis=0), tri)
        sums = {w: [stacked[(i * n_chunks + c) * SB_TQ:(i * n_chunks + c + 1) * SB_TQ]
                    for c in range(n_chunks)] for i, w in enumerate(items)}
        a, carry = {}, {}
        for w in items:
            a[w], carry[w] = _sb_weights(z[w], sums[w], jnp.zeros((SB_TQ, LANES), F32))
        for slot in range(len(blocks)):
            both = _dot(jnp.concatenate([a[slot, h] for h in heads], axis=0),
                        v_ref[pl.ds(start[slot], SB_WINDOW), :])
            for h in heads:
                acc_ref[slot, h] = both[h * SB_TQ:(h + 1) * SB_TQ]
                carry_ref[slot, h] = carry[slot, h]

        def slack(slot, carries):
            per_head = [carries[h] - z_max[slot, h] for h in heads]
            return jnp.min(jnp.minimum(per_head[0], per_head[1]), keepdims=True)

        slacks = [slack(slot, [carry[slot, h] for h in heads]) for slot in range(len(blocks))]

        @pl.when(functools.reduce(jnp.minimum, slacks)[0, 0] < SB_DEAD)
        def _():
            for slot in range(len(blocks)):
                def cond(state):
                    ke, live = state
                    return (ke > 0) & live

                def body(state):
                    ke, _ = state
                    ks = pl.multiple_of(jnp.maximum(ke - SB_TK, 0), SB_TQ)
                    fresh = (lax.broadcasted_iota(jnp.int32, (SB_TQ, SB_TK), 1) + ks) < ke
                    carries = []
                    for h in heads:
                        zt = jnp.where(fresh, _dot_nt(q[slot, h], k_ref[pl.ds(ks, SB_TK), :]),
                                       NEG_INF)
                        at, ct = _sb_weights(
                            zt, _sb_suffix_sums(_split_hi_lo(_softplus2(zt)), tri),
                            carry_ref[slot, h])
                        acc_ref[slot, h] += _dot(at, v_ref[pl.ds(ks, SB_TK), :])
                        carry_ref[slot, h] = ct
                        carries.append(ct)
                    return ks, slack(slot, carries)[0, 0] < SB_DEAD

                lax.while_loop(cond, body, (jnp.asarray(start[slot], jnp.int32),
                                            slacks[slot][0, 0] < SB_DEAD))

        for slot in range(len(blocks)):
            o_ref[rows[slot], :] = jnp.where(own[0], acc_ref[slot, 0],
                                             acc_ref[slot, 1]).astype(o_ref.dtype)

    n_lead = lead // SB_TQ
    query_blocks(list(range(n_lead)))
    n_groups = (n_q - n_lead) // SB_BLOCKS_PER_ITER

    def group(i, _):
        query_blocks([n_lead + i * SB_BLOCKS_PER_ITER + j for j in range(SB_BLOCKS_PER_ITER)])
        return 0

    lax.fori_loop(0, n_groups, group, 0)
    rest = list(range(n_lead + n_groups * SB_BLOCKS_PER_ITER, n_q))
    if rest:
        query_blocks(rest)


def _sb_attention(qs, ks, vs, k_norm2):
    b, s, _ = qs.shape
    assert s % SB_TQ == 0 and s >= SB_WINDOW and SB_WINDOW % LANES == 0 and HEADS_PER_STEP == 2
    half = np.concatenate([np.arange(LANES)[:, None] >= np.arange(LANES)[None, :],
                           np.ones((LANES, LANES), bool)], axis=1)
    tri = jnp.asarray(np.concatenate([half, half], axis=0), BF16)
    nh = HEADS_PER_STEP
    seq = pl.BlockSpec((None, s, LANES), lambda bi, p: (bi, 0, p))
    return pl.pallas_call(
        _sb_kernel,
        grid=(b, SB_WIDTH // LANES),
        in_specs=[seq, seq, seq, pl.BlockSpec(tri.shape, lambda bi, p: (0, 0)),
                  pl.BlockSpec((None,) + k_norm2.shape[1:], lambda bi, p: (bi, 0, 0, 0))],
        out_specs=seq,
        out_shape=jax.ShapeDtypeStruct((b, s, SB_WIDTH), BF16),
        scratch_shapes=[pltpu.VMEM((SB_BLOCKS_PER_ITER, nh, SB_TQ, LANES), F32)] * 2,
        compiler_params=pltpu.CompilerParams(
            dimension_semantics=("parallel", "parallel"), vmem_limit_bytes=VMEM_LIMIT),
        name="stick_breaking",
    )(qs, ks, vs, tri, k_norm2)


TAIL_TILES = 2
MLP_FF_CHUNK = 1024


def _token_major(ref, scr):
    d, rows, _ = ref.shape
    for c in range(scr.shape[0]):
        for r in range(d):
            scr[c, pl.ds(r, rows, stride=d), :] = ref[r, :, c * LANES:(c + 1) * LANES]
    return jnp.concatenate([scr[c] for c in range(scr.shape[0])], axis=1)


def _tail_kernel(x_ref, o0, o1, o2, l0, l1, l2, ob_ref, ga_ref, gb_ref,
                 wud, wus, wout, g_mlp, w1, w2, g_out, out_ref, *scratch, final_norm):
    tiles = range(TAIL_TILES)
    rows = [slice(t * DIL_TILE, (t + 1) * DIL_TILE) for t in tiles]
    o_a = []
    for t in tiles:
        scr = scratch[4 * t:4 * t + 4]
        outs = [o0[rows[t], :], _token_major(o1.at[t], scr[0]), _token_major(o2.at[t], scr[1])]
        lse = [l0[rows[t], :], _token_major(l1.at[t], scr[2]), _token_major(l2.at[t], scr[3])]
        m = jnp.maximum(jnp.maximum(lse[0], lse[1]), lse[2])
        e = [jnp.exp(l - m) for l in lse]
        o_a.append((e[0] * outs[0] + e[1] * outs[1] + e[2] * outs[2]) / (e[0] + e[1] + e[2]))
    up_a = [_dot(o_a[t].astype(BF16), wud[...]) for t in tiles]
    up_b = [_dot(ob_ref[rows[t], :], wus[...]) for t in tiles]
    merged = [ga_ref[rows[t], :] * up_a[t] + gb_ref[rows[t], :] * up_b[t] for t in tiles]
    y = [x_ref[rows[t], :] + _dot(merged[t].astype(BF16), wout[...]) for t in tiles]
    h2 = [_rms(y[t], g_mlp[...]).astype(BF16) for t in tiles]
    for c in range(D_FF // MLP_FF_CHUNK):
        cs = slice(c * MLP_FF_CHUNK, (c + 1) * MLP_FF_CHUNK)
        u = [jnp.maximum(_dot(h2[t], w1[:, cs]), 0.0) for t in tiles]
        y = [y[t] + _dot((u[t] * u[t]).astype(BF16), w2[cs, :]) for t in tiles]
    for t in tiles:
        out_ref[rows[t], :] = _rms(y[t], g_out[...]) if final_norm else y[t]


def _tail(x, oas, lses, o_b, g_a, g_b, wud, wus, wout, g_mlp, w1, w2, g_out, final_norm):
    b, s, d = x.shape
    tm = TAIL_TILES * DIL_TILE
    tok = lambda w: pl.BlockSpec((None, tm, w), lambda bi, i: (bi, i, 0))
    full = lambda a: pl.BlockSpec(a.shape, lambda bi, i: (0,) * a.ndim,
                                  pipeline_mode=pl.Buffered(1))

    def dil(a):
        if a.ndim == 3:
            return tok(DIL_OUT_WIDTH)
        return pl.BlockSpec((None, TAIL_TILES) + a.shape[2:], lambda bi, i: (bi, i, 0, 0, 0))

    weights = (wud, wus, wout, g_mlp, w1, w2, g_out)
    return pl.pallas_call(
        functools.partial(_tail_kernel, final_norm=final_norm),
        grid=(b, s // tm),
        in_specs=[tok(d)] + [dil(a) for a in (*oas, *lses)] + [tok(SB_WIDTH), tok(d), tok(d)]
                 + [full(a) for a in weights],
        out_specs=tok(d),
        out_shape=jax.ShapeDtypeStruct((b, s, d), F32),
        scratch_shapes=[pltpu.VMEM((DIL_OUT_WIDTH // LANES, DIL_TILE, LANES), F32)]
                       * (4 * TAIL_TILES),
        compiler_params=pltpu.CompilerParams(
            dimension_semantics=("parallel", "parallel"), vmem_limit_bytes=VMEM_LIMIT),
        name="tail",
    )(x, *oas, *lses, o_b, g_a, g_b, *weights)


def kernel(x, norm_mix_g, w_in, b_gate, w_up_dil, w_up_sb, w_out, norm_mlp_g, w_mlp_in,
           w_mlp_out, norm_final_g):
    depth = w_in.shape[0]
    row = lambda v: v.reshape(1, -1)
    for layer in range(depth):
        w = w_in[layer]
        c0 = 3 * DIL_WIDTH
        c1 = c0 + 3 * SB_WIDTH
        dil_cols = lambda i, grp: w[:, i * DIL_WIDTH + grp * DIL_OUT_WIDTH:
                                    i * DIL_WIDTH + (grp + 1) * DIL_OUT_WIDTH]
        w_dil = jnp.concatenate(
            [part for grp in range(len(DIL_GROUPS))
             for part in (dil_cols(0, grp) * (1.0 / math.sqrt(HEAD_DIM)), dil_cols(1, grp),
                          dil_cols(2, grp))], axis=1).astype(BF16)
        *qkv, qs, ks, vs, g_a, g_b, k_norm2 = _proj(
            x, row(norm_mix_g[layer]), w_dil, w[:, c0:c1].astype(BF16), w[:, c1:].astype(BF16),
            row(b_gate[layer]))
        dil = _dilated(qkv)
        o_b = _sb_attention(qs, ks, vs, k_norm2)
        x = _tail(x, [o for o, _ in dil], [l for _, l in dil], o_b, g_a, g_b,
                  w_up_dil[layer].astype(BF16), w_up_sb[layer].astype(BF16),
                  w_out[layer].astype(BF16), row(norm_mlp_g[layer]),
                  w_mlp_in[layer].astype(BF16), w_mlp_out[layer].astype(BF16),
                  row(norm_final_g), final_norm=layer == depth - 1)
    return x
```
